```python
import jax, jax.numpy as jnp
from jax import lax
import numpy as np

D_MODEL = 1024
BATCH = 4
SEQ = 8192
DEPTH = 4

N_MIXERS = 3
HEAD_DIM = 64
EPS = 1e-6
NEG_INF = -1e30

A_HEADS = D_MODEL // HEAD_DIM
A_WINDOWS = (128, 512, 2048)
A_DILATIONS = (1, 4, 16)
A_GROUPS = len(A_WINDOWS)
ROPE_THETA = 500000.0
ROPE_DIMS = HEAD_DIM // 4

B_CONV_WIDTH = 3

C_Q_HEADS = D_MODEL // HEAD_DIM
C_KV_HEADS = 4
C_THETA = 10000.0
Q_BLOCK = 128
GRID_W = 64

D_FF = 4 * D_MODEL

N_A = len(range(0, DEPTH, N_MIXERS))
N_B = len(range(1, DEPTH, N_MIXERS))
N_C = len(range(2, DEPTH, N_MIXERS))

kernel_name = "hybrid_dilated_conv_axial_gqa_encoder"


def rms_norm(x, g):
    xf = x.astype(jnp.float32)
    y = xf * lax.rsqrt(jnp.mean(xf * xf, axis=-1, keepdims=True) + EPS)
    return (y * g.astype(jnp.float32)).astype(x.dtype)


def rope_angles(pos, dim, theta):
    inv = theta ** (-jnp.arange(0, dim, 2, dtype=jnp.float32) / dim)
    ang = pos.astype(jnp.float32)[:, None] * inv[None, :]
    return jnp.cos(ang), jnp.sin(ang)


def apply_rope(x, cos, sin):
    half = x.shape[-1] // 2
    xf = x.astype(jnp.float32)
    x1, x2 = xf[..., :half], xf[..., half:]
    c, s = cos[:, None, :], sin[:, None, :]
    return jnp.concatenate([x1 * c - x2 * s, x2 * c + x1 * s], axis=-1).astype(x.dtype)


def partial_rope(x, cos, sin):
    return jnp.concatenate([apply_rope(x[..., :ROPE_DIMS], cos, sin), x[..., ROPE_DIMS:]], axis=-1)


def dilated_group(q, k, v, dilation, radius):
    B, S, H, Dh = q.shape
    L = S // dilation
    nb = -(-L // radius)
    Lp = nb * radius

    def split(t):
        t = jnp.moveaxis(t.reshape(B, L, dilation, H, Dh), 2, 1)
        return jnp.pad(t, ((0, 0), (0, 0), (0, Lp - L), (0, 0), (0, 0)))

    def band(t):
        tp = jnp.pad(t, ((0, 0), (0, 0), (radius, radius), (0, 0), (0, 0)))
        tb = tp.reshape(B, dilation, nb + 2, radius, H, Dh)
        return jnp.concatenate([tb[:, :, :-2], tb[:, :, 1:-1], tb[:, :, 2:]], axis=3)

    qb = split(q).reshape(B, dilation, nb, radius, H, Dh)
    kb = band(split(k))
    vb = band(split(v))

    q_idx = jnp.arange(Lp).reshape(nb, radius)
    k_idx = jnp.arange(nb)[:, None] * radius - radius + jnp.arange(3 * radius)[None, :]
    mask = ((jnp.abs(q_idx[:, :, None] - k_idx[:, None, :]) <= radius)
            & (k_idx[:, None, :] >= 0) & (k_idx[:, None, :] < L))

    s = jnp.einsum('bdnqhe,bdnkhe->bdnqhk', qb, kb).astype(jnp.float32) * (HEAD_DIM ** -0.5)
    s = jnp.where(mask[None, None, :, :, None, :], s, NEG_INF)
    mx = jnp.max(s, axis=-1)
    p = jnp.exp(s - mx[..., None])
    den = jnp.sum(p, axis=-1)
    num = jnp.einsum('bdnqhk,bdnkhe->bdnqhe', p, vb.astype(jnp.float32))

    def unsplit(t):
        rest = t.shape[4:]
        t = t.reshape((B, dilation, Lp) + rest)[:, :, :L]
        return jnp.moveaxis(t, 1, 2).reshape((B, S) + rest)

    return unsplit(num), unsplit(den), unsplit(mx)


def mixer_a(h, w_qkv, q_gain, k_gain, w_o, cos, sin):
    B, S, _ = h.shape
    qkv = (h @ w_qkv).reshape(B, S, A_GROUPS, 3, A_HEADS, HEAD_DIM)
    nums, dens, mxs = [], [], []
    for g in range(A_GROUPS):
        dil = A_DILATIONS[g]
        radius = A_WINDOWS[g] // (2 * dil)
        q = partial_rope(rms_norm(qkv[:, :, g, 0], q_gain[g]), cos, sin)
        k = partial_rope(rms_norm(qkv[:, :, g, 1], k_gain[g]), cos, sin)
        v = qkv[:, :, g, 2]
        num, den, mx = dilated_group(q, k, v, dil, radius)
        nums.append(num); dens.append(den); mxs.append(mx)
    mx = jnp.stack(mxs, 0)
    wts = jnp.exp(mx - jnp.max(mx, axis=0, keepdims=True))
    num = jnp.sum(wts[..., None] * jnp.stack(nums, 0), axis=0)
    den = jnp.sum(wts * jnp.stack(dens, 0), axis=0)
    o = (num / den[..., None]).astype(h.dtype).reshape(B, S, D_MODEL)
    return o @ w_o


def mixer_b(h, w_in, conv_w, w_out):
    b_gate, c_gate, xt = jnp.split(h @ w_in, 3, axis=-1)
    u = c_gate * xt
    y = lax.conv_general_dilated(u, conv_w[:, None, :].astype(u.dtype), window_strides=(1,),
                                 padding=((1, 1),), dimension_numbers=('NWC', 'WIO', 'NWC'),
                                 feature_group_count=D_MODEL)
    return (b_gate * y) @ w_out


def axial_rope(x, cos_r, sin_r, cos_c, sin_c):
    half = HEAD_DIM // 2
    return jnp.concatenate([apply_rope(x[..., :half], cos_r, sin_r),
                            apply_rope(x[..., half:], cos_c, sin_c)], axis=-1)


def mixer_c(h, w_qkv, q_gain, k_gain, w_o):
    B, S, _ = h.shape
    rows = S // GRID_W
    row_id = jnp.repeat(jnp.arange(rows), GRID_W)
    col_id = jnp.tile(jnp.arange(GRID_W), rows)
    cos_r, sin_r = rope_angles(row_id, HEAD_DIM // 2, C_THETA)
    cos_c, sin_c = rope_angles(col_id, HEAD_DIM // 2, C_THETA)

    qkv = h @ w_qkv
    nq, nk = C_Q_HEADS * HEAD_DIM, C_KV_HEADS * HEAD_DIM
    q = qkv[..., :nq].reshape(B, S, C_Q_HEADS, HEAD_DIM)
    k = qkv[..., nq:nq + nk].reshape(B, S, C_KV_HEADS, HEAD_DIM)
    v = qkv[..., nq + nk:].reshape(B, S, C_KV_HEADS, HEAD_DIM)
    q = axial_rope(rms_norm(q, q_gain), cos_r, sin_r, cos_c, sin_c)
    k = axial_rope(rms_norm(k, k_gain), cos_r, sin_r, cos_c, sin_c)

    grp = C_Q_HEADS // C_KV_HEADS
    nblk = S // Q_BLOCK
    qb = jnp.moveaxis(q.reshape(B, nblk, Q_BLOCK, C_KV_HEADS, grp, HEAD_DIM), 1, 0)

    def attend(qblk):
        s = jnp.einsum('bqhgd,bkhd->bhgqk', qblk, k).astype(jnp.float32) * (HEAD_DIM ** -0.5)
        p = jax.nn.softmax(s, axis=-1)
        return jnp.einsum('bhgqk,bkhd->bqhgd', p.astype(v.dtype), v)

    o = lax.map(attend, qb)
    o = jnp.moveaxis(o, 0, 1).reshape(B, S, D_MODEL)
    return o @ w_o


def mlp_sq_relu(h, w1, w2):
    return jnp.square(jax.nn.relu(h @ w1)) @ w2


def setup_inputs(seed: int = 0) -> dict:
    key = jax.random.key(seed)
    ks = jax.random.split(key, 20)
    D, Dh = D_MODEL, HEAD_DIM
    nrm = lambda k, shape, fan_in: jax.random.normal(k, shape, jnp.float32) * (fan_in ** -0.5)
    gain = lambda k, shape: 1.0 + 0.02 * jax.random.normal(k, shape, jnp.float32)
    c_cols = (C_Q_HEADS + 2 * C_KV_HEADS) * Dh
    return {
        "x": jax.random.normal(ks[0], (BATCH, SEQ, D), jnp.float32),
        "norm1": gain(ks[1], (DEPTH, D)),
        "norm2": gain(ks[2], (DEPTH, D)),
        "a_wqkv": nrm(ks[3], (N_A, D, A_GROUPS * 3 * D), D),
        "a_q_gain": gain(ks[4], (N_A, A_GROUPS, Dh)),
        "a_k_gain": gain(ks[5], (N_A, A_GROUPS, Dh)),
        "a_wo": nrm(ks[6], (N_A, D, D), D),
        "b_win": nrm(ks[7], (N_B, D, 3 * D), D),
        "b_conv": nrm(ks[8], (N_B, B_CONV_WIDTH, D), B_CONV_WIDTH),
        "b_wout": nrm(ks[9], (N_B, D, D), D),
        "c_wqkv": nrm(ks[10], (N_C, D, c_cols), D),
        "c_q_gain": gain(ks[11], (N_C, Dh)),
        "c_k_gain": gain(ks[12], (N_C, Dh)),
        "c_wo": nrm(ks[13], (N_C, D, D), D),
        "mlp_w1": nrm(ks[14], (DEPTH, D, D_FF), D),
        "mlp_w2": nrm(ks[15], (DEPTH, D_FF, D), D_FF),
    }


def reference(x, norm1, norm2, a_wqkv, a_q_gain, a_k_gain, a_wo, b_win, b_conv, b_wout,
              c_wqkv, c_q_gain, c_k_gain, c_wo, mlp_w1, mlp_w2):
    S = x.shape[1]
    cos, sin = rope_angles(jnp.arange(S), ROPE_DIMS, ROPE_THETA)
    h = x
    for i in range(DEPTH):
        kind, j = i % N_MIXERS, i // N_MIXERS
        y = rms_norm(h, norm1[i])
        if kind == 0:
            y = mixer_a(y, a_wqkv[j], a_q_gain[j], a_k_gain[j], a_wo[j], cos, sin)
        elif kind == 1:
            y = mixer_b(y, b_win[j], b_conv[j], b_wout[j])
        else:
            y = mixer_c(y, c_wqkv[j], c_q_gain[j], c_k_gain[j], c_wo[j])
        h = h + y
        h = h + mlp_sq_relu(rms_norm(h, norm2[i]), mlp_w1[i], mlp_w2[i])
    return h
```

```python
import functools
import math

import jax
import jax.numpy as jnp
import numpy as np
from jax import lax
from jax.experimental import pallas as pl
from jax.experimental.pallas import tpu as pltpu

HEAD_DIM = 64
EPS = 1e-6
NEG_INF = -1e30
N_MIXERS = 3
A_WINDOWS = (128, 512, 2048)
A_DILATIONS = (1, 4, 16)
A_RADIUS = 64
ROPE_THETA = 500000.0
ROPE_DIMS = HEAD_DIM // 4
C_KV_HEADS = 4
C_THETA = 10000.0
GRID_W = 64
SCORE_SCALE = HEAD_DIM ** -0.5

LANES = 128
BF16_SUBLANES = 16
VMEM_LIMIT = 56 * 1024 * 1024

F32 = jnp.float32
BF16 = jnp.bfloat16

NT_DIMS = (((1,), (1,)), ((), ()))
TN_DIMS = (((0,), (0,)), ((), ()))


def _cparams(sem):
    return pltpu.CompilerParams(dimension_semantics=sem, vmem_limit_bytes=VMEM_LIMIT)


def _resident(shape):
    nd = len(shape)
    return pl.BlockSpec(shape, lambda *_: (0,) * nd, pipeline_mode=pl.Buffered(1))


def _rms_rows(x, g):
    ms = jnp.mean(x * x, axis=-1, keepdims=True)
    return (x * lax.rsqrt(ms + EPS)) * g


def _head_norm_T(t, gain):
    n = t.shape[-1]
    t3 = t.reshape(t.shape[0] // HEAD_DIM, HEAD_DIM, n)
    ms = jnp.mean(t3 * t3, axis=1, keepdims=True)
    return (t3 * lax.rsqrt(ms + EPS)) * gain[None]


def _rot(a, b, c, s):
    return a * c - b * s, b * c + a * s


def _a_proj_kernel(x_ref, g_ref, wt_ref, gq_ref, gk_ref, cos_ref, sin_ref, qT_ref, k_ref, vT_ref):
    d = x_ref.shape[-1]
    y = _rms_rows(x_ref[0], g_ref[...]).astype(BF16)
    c = cos_ref[0]
    s = sin_ref[0]
    half = ROPE_DIMS // 2

    def qk(rows, gain, scale):
        t = lax.dot_general(wt_ref[rows, :], y, NT_DIMS, preferred_element_type=F32)
        tn = _head_norm_T(t, gain)
        ra, rb = _rot(tn[:, :half], tn[:, half:ROPE_DIMS], c[None], s[None])
        out = jnp.concatenate([ra, rb, tn[:, ROPE_DIMS:]], axis=1)
        if scale != 1.0:
            out = out * scale
        return out.reshape(t.shape)

    qT_ref[0, 0] = qk(slice(0, d), gq_ref[...], SCORE_SCALE).astype(BF16)
    k_ref[0, 0] = qk(slice(d, 2 * d), gk_ref[...], 1.0).T.astype(BF16)
    vT_ref[0, 0] = lax.dot_general(wt_ref[2 * d:3 * d, :], y, NT_DIMS,
                                   preferred_element_type=F32).astype(BF16)


def _a_proj(h, g1, wt, gq, gk, cosT, sinT, dil, tl):
    b, s, d = h.shape
    l = s // dil
    hv = h.reshape(b, l, dil * d)
    grid = (b, dil, l // tl)
    feat = pl.BlockSpec((1, 1, d, tl), lambda bi, r, i: (bi, r, 0, i))
    tok = pl.BlockSpec((1, 1, tl, d), lambda bi, r, i: (bi, r, i, 0))
    tab = pl.BlockSpec((1, ROPE_DIMS // 2, tl), lambda bi, r, i: (r, 0, i))
    return pl.pallas_call(
        _a_proj_kernel,
        grid=grid,
        in_specs=[
            pl.BlockSpec((1, tl, d), lambda bi, r, i: (bi, i, r)),
            _resident((1, d)),
            _resident((3 * d, d)),
            _resident((HEAD_DIM, tl)),
            _resident((HEAD_DIM, tl)),
            tab, tab,
        ],
        out_specs=[feat, tok, feat],
        out_shape=[
            jax.ShapeDtypeStruct((b, dil, d, l), BF16),
            jax.ShapeDtypeStruct((b, dil, l, d), BF16),
            jax.ShapeDtypeStruct((b, dil, d, l), BF16),
        ],
        compiler_params=_cparams(("parallel", "parallel", "parallel")),
        name="a_proj",
    )(hv, g1, wt, gq, gk, cosT, sinT)


def _a_attn_kernel(qT_ref, klo_ref, kmid_ref, khi_ref, vlo_ref, vmid_ref, vhi_ref,
                   blo_ref, bmid_ref, bhi_ref, o_ref, m_ref, l_ref, oT_scr, st_scr):
    i = pl.program_id(1)
    nt = pl.num_programs(1)
    tq = qT_ref.shape[-1]
    n_heads = qT_ref.shape[1] // HEAD_DIM
    edge_lo = jnp.where(i == 0, NEG_INF, 0.0).astype(F32)
    edge_hi = jnp.where(i == nt - 1, NEG_INF, 0.0).astype(F32)
    b_lo = blo_ref[...] + edge_lo
    b_mid = bmid_ref[...]
    b_hi = bhi_ref[...] + edge_hi
    upper = lax.broadcasted_iota(jnp.int32, (LANES, tq), 0) >= HEAD_DIM

    st_scr[...] = jnp.zeros_like(st_scr)
    st_scr[1, n_heads:, :] = jnp.ones((LANES - n_heads, tq), F32)

    for p in range(n_heads // 2):
        cols = slice(p * LANES, (p + 1) * LANES)
        q2 = qT_ref[0, cols, :]
        k_lo, k_mid, k_hi = klo_ref[0, :, cols], kmid_ref[0, :, cols], khi_ref[0, :, cols]
        for hh in range(2):
            h = 2 * p + hh
            qz = jnp.where(upper == (hh == 1), q2, jnp.zeros_like(q2))
            s_lo = jnp.dot(k_lo, qz, preferred_element_type=F32) + b_lo
            s_mid = jnp.dot(k_mid, qz, preferred_element_type=F32) + b_mid
            s_hi = jnp.dot(k_hi, qz, preferred_element_type=F32) + b_hi
            m = jnp.maximum(jnp.maximum(jnp.max(s_lo, axis=0, keepdims=True),
                                        jnp.max(s_mid, axis=0, keepdims=True)),
                            jnp.max(s_hi, axis=0, keepdims=True))
            p_lo, p_mid, p_hi = jnp.exp(s_lo - m), jnp.exp(s_mid - m), jnp.exp(s_hi - m)
            l = (jnp.sum(p_lo, axis=0, keepdims=True) + jnp.sum(p_mid, axis=0, keepdims=True)
                 + jnp.sum(p_hi, axis=0, keepdims=True))
            rows = slice(h * HEAD_DIM, (h + 1) * HEAD_DIM)
            num = (jnp.dot(vlo_ref[0, rows, :], p_lo.astype(BF16), preferred_element_type=F32)
                   + jnp.dot(vmid_ref[0, rows, :], p_mid.astype(BF16), preferred_element_type=F32)
                   + jnp.dot(vhi_ref[0, rows, :], p_hi.astype(BF16), preferred_element_type=F32))
            oT_scr[rows, :] = num / l
            st_scr[0, h:h + 1, :] = m
            st_scr[1, h:h + 1, :] = l

    o_ref[0] = oT_scr[...].T.astype(BF16)
    m_ref[0] = st_scr[0].T
    l_ref[0] = st_scr[1].T


def _band_bias(tq):
    j = np.arange(tq)[None, :]
    kk = np.arange(LANES)[:, None]
    lo = np.where(kk >= j + (LANES - A_RADIUS), 0.0, NEG_INF)
    hi = np.where(kk <= j + A_RADIUS - tq, 0.0, NEG_INF)
    km = np.arange(tq)[:, None]
    mid = np.where(np.abs(km - j) <= A_RADIUS, 0.0, NEG_INF)
    return [jnp.asarray(a, F32) for a in (lo, mid, hi)]


def _a_attn(qT, k, vT, dil, tq):
    b, _, d, l = qT.shape
    n = b * dil
    qT, k, vT = qT.reshape(n, d, l), k.reshape(n, l, d), vT.reshape(n, d, l)
    per = tq // LANES
    nh = l // LANES
    lo_idx = lambda i: jnp.maximum(i * per - 1, 0)
    hi_idx = lambda i: jnp.minimum((i + 1) * per, nh - 1)
    blo, bmid, bhi = _band_bias(tq)
    tok_out = lambda width: pl.BlockSpec((1, tq, width), lambda ni, i: (ni // dil, i, ni % dil))
    o, m, lsum = pl.pallas_call(
        _a_attn_kernel,
        grid=(n, l // tq),
        in_specs=[
            pl.BlockSpec((1, d, tq), lambda ni, i: (ni, 0, i)),
            pl.BlockSpec((1, LANES, d), lambda ni, i: (ni, lo_idx(i), 0)),
            pl.BlockSpec((1, tq, d), lambda ni, i: (ni, i, 0)),
            pl.BlockSpec((1, LANES, d), lambda ni, i: (ni, hi_idx(i), 0)),
            pl.BlockSpec((1, d, LANES), lambda ni, i: (ni, 0, lo_idx(i))),
            pl.BlockSpec((1, d, tq), lambda ni, i: (ni, 0, i)),
            pl.BlockSpec((1, d, LANES), lambda ni, i: (ni, 0, hi_idx(i))),
            _resident((LANES, tq)),
            _resident((tq, tq)),
            _resident((LANES, tq)),
        ],
        out_specs=[tok_out(d), tok_out(LANES), tok_out(LANES)],
        out_shape=[
            jax.ShapeDtypeStruct((b, l, dil * d), BF16),
            jax.ShapeDtypeStruct((b, l, dil * LANES), F32),
            jax.ShapeDtypeStruct((b, l, dil * LANES), F32),
        ],
        scratch_shapes=[pltpu.VMEM((d, tq), F32), pltpu.VMEM((2, LANES, tq), F32)],
        compiler_params=_cparams(("parallel", "parallel")),
        name="a_attn",
    )(qT, k, k, k, vT, vT, vT, blo, bmid, bhi)
    s = l * dil
    return o.reshape(b * s, d), m.reshape(b * s, LANES), lsum.reshape(b * s, LANES)


def _expand_heads(a, e_ref):
    hi = a.astype(BF16)
    r1 = a - hi.astype(F32)
    mid = r1.astype(BF16)
    lo = (r1 - mid.astype(F32)).astype(BF16)
    e = e_ref[...]
    return (jnp.dot(hi, e, preferred_element_type=F32) + jnp.dot(mid, e, preferred_element_type=F32)
            + jnp.dot(lo, e, preferred_element_type=F32))


def _a_out_kernel(o0, o1, o2, m0, m1, m2, l0, l1, l2, e_ref, wo_ref, h_ref, out_ref):
    ms = [m0[...], m1[...], m2[...]]
    ls = [l0[...], l1[...], l2[...]]
    mx = jnp.maximum(jnp.maximum(ms[0], ms[1]), ms[2])
    ws = [jnp.exp(m - mx) * l for m, l in zip(ms, ls)]
    den = ws[0] + ws[1] + ws[2]
    o = None
    for w, o_ref in zip(ws, (o0, o1, o2)):
        term = _expand_heads(w / den, e_ref) * o_ref[...].astype(F32)
        o = term if o is None else o + term
    out_ref[...] = h_ref[...] + jnp.dot(o.astype(BF16), wo_ref[...], preferred_element_type=F32)


def _head_indicator(d):
    e = np.zeros((LANES, d), np.float32)
    for h in range(d // HEAD_DIM):
        e[h, h * HEAD_DIM:(h + 1) * HEAD_DIM] = 1.0
    return jnp.asarray(e, BF16)


def _a_out(parts, wo, h2, tm):
    t, d = h2.shape
    tok = lambda width: pl.BlockSpec((tm, width), lambda i: (i, 0))
    os_, ms_, ls_ = zip(*parts)
    return pl.pallas_call(
        _a_out_kernel,
        grid=(t // tm,),
        in_specs=[tok(d)] * 3 + [tok(LANES)] * 6 + [_resident((LANES, d)), _resident((d, d)), tok(d)],
        out_specs=tok(d),
        out_shape=jax.ShapeDtypeStruct((t, d), F32),
        compiler_params=_cparams(("parallel",)),
        name="a_out",
    )(*os_, *ms_, *ls_, _head_indicator(d), wo, h2)


def _rope_tables_T(pos, dim, theta):
    inv = theta ** (-jnp.arange(0, dim, 2, dtype=F32) / dim)
    ang = inv[:, None] * pos.astype(F32)[None, :]
    return jnp.cos(ang), jnp.sin(ang)


def _mixer_a(h, g1, w_qkv, q_gain, k_gain, w_o):
    b, s, d = h.shape
    parts = []
    for g, dil in enumerate(A_DILATIONS):
        assert A_WINDOWS[g] // (2 * dil) == A_RADIUS
        l = s // dil
        tl = min(512, l)
        tq = min(256, l // 2)
        pos = (jnp.arange(l)[None, :] * dil + jnp.arange(dil)[:, None]).reshape(-1)
        cosT, sinT = _rope_tables_T(pos, ROPE_DIMS, ROPE_THETA)
        cosT = cosT.reshape(-1, dil, l).transpose(1, 0, 2)
        sinT = sinT.reshape(-1, dil, l).transpose(1, 0, 2)
        wt = w_qkv[:, g * 3 * d:(g + 1) * 3 * d].T.astype(BF16)
        gq = jnp.broadcast_to(q_gain[g][:, None], (HEAD_DIM, tl))
        gk = jnp.broadcast_to(k_gain[g][:, None], (HEAD_DIM, tl))
        qT, k, vT = _a_proj(h, g1, wt, gq, gk, cosT, sinT, dil, tl)
        parts.append(_a_attn(qT, k, vT, dil, tq))
    out = _a_out(parts, w_o.astype(BF16), h.reshape(b * s, d), 512)
    return out.reshape(b, s, d)


def _b_in_kernel(x_ref, g_ref, w_ref, bg_ref, u_ref):
    d = x_ref.shape[-1]
    y = _rms_rows(x_ref[...], g_ref[...]).astype(BF16)
    bg_ref[...] = jnp.dot(y, w_ref[:, 0:d], preferred_element_type=F32).astype(BF16)
    cg = jnp.dot(y, w_ref[:, d:2 * d], preferred_element_type=F32)
    xt = jnp.dot(y, w_ref[:, 2 * d:3 * d], preferred_element_type=F32)
    u_ref[...] = (cg * xt).astype(BF16)


def _b_out_kernel(tiles_per_seq, u_ref, up_ref, un_ref, bg_ref, cw_ref, wo_ref, h_ref, out_ref, scr):
    i = pl.program_id(0)
    tm = u_ref.shape[0]
    first = (i % tiles_per_seq) == 0
    last = (i % tiles_per_seq) == tiles_per_seq - 1
    hb = up_ref.shape[0]
    scr[0:hb, :] = jnp.where(first, 0.0, up_ref[...].astype(F32))
    scr[hb:hb + tm, :] = u_ref[...].astype(F32)
    scr[hb + tm:2 * hb + tm, :] = jnp.where(last, 0.0, un_ref[...].astype(F32))
    cw = cw_ref[...]
    y = (scr[hb - 1:hb - 1 + tm, :] * cw[0:1] + scr[hb:hb + tm, :] * cw[1:2]
         + scr[hb + 1:hb + 1 + tm, :] * cw[2:3])
    z = (bg_ref[...].astype(F32) * y).astype(BF16)
    out_ref[...] = h_ref[...] + jnp.dot(z, wo_ref[...], preferred_element_type=F32)


def _mixer_b(h, g1, w_in, conv_w, w_out):
    b, s, d = h.shape
    t = b * s
    tm = 512
    h2 = h.reshape(t, d)
    tok = pl.BlockSpec((tm, d), lambda i: (i, 0))
    bg, u = pl.pallas_call(
        _b_in_kernel,
        grid=(t // tm,),
        in_specs=[tok, _resident((1, d)), _resident((d, 3 * d))],
        out_specs=[tok, tok],
        out_shape=[jax.ShapeDtypeStruct((t, d), BF16)] * 2,
        compiler_params=_cparams(("parallel",)),
        name="b_in",
    )(h2, g1, w_in.astype(BF16))
    hb = BF16_SUBLANES
    per = tm // hb
    out = pl.pallas_call(
        functools.partial(_b_out_kernel, s // tm),
        grid=(t // tm,),
        in_specs=[
            tok,
            pl.BlockSpec((hb, d), lambda i: (jnp.maximum(i * per - 1, 0), 0)),
            pl.BlockSpec((hb, d), lambda i: (jnp.minimum((i + 1) * per, t // hb - 1), 0)),
            tok, _resident((3, d)), _resident((d, d)), tok,
        ],
        out_specs=tok,
        out_shape=jax.ShapeDtypeStruct((t, d), F32),
        scratch_shapes=[pltpu.VMEM((tm + 2 * hb, d), F32)],
        compiler_params=_cparams(("parallel",)),
        name="b_out",
    )(u, u, u, bg, conv_w, w_out.astype(BF16), h2)
    return out.reshape(b, s, d)


def _axial_rope_T(tn, cr, sr, cc, sc):
    q = HEAD_DIM // 4
    a0, b0 = _rot(tn[:, 0:q], tn[:, q:2 * q], cr[None], sr[None])
    a1, b1 = _rot(tn[:, 2 * q:3 * q], tn[:, 3 * q:], cc[None], sc[None])
    return jnp.concatenate([a0, b0, a1, b1], axis=1)


def _c_proj_kernel(x_ref, g_ref, wq_ref, wk_ref, wv_ref, gq_ref, gk_ref, cr_ref, sr_ref, cc_ref, sc_ref,
                   qT_ref, k_ref, vT_ref):
    y = _rms_rows(x_ref[0], g_ref[...]).astype(BF16)
    tabs = (cr_ref[...], sr_ref[...], cc_ref[...], sc_ref[...])
    qT = lax.dot_general(wq_ref[...], y, NT_DIMS, preferred_element_type=F32)
    qn = _axial_rope_T(_head_norm_T(qT, gq_ref[...]), *tabs) * SCORE_SCALE
    qT_ref[0] = qn.reshape(qT.shape).astype(BF16)
    kT = lax.dot_general(wk_ref[...], y, NT_DIMS, preferred_element_type=F32)
    kn = _axial_rope_T(_head_norm_T(kT, gk_ref[...]), *tabs)
    k_ref[0] = kn.reshape(kT.shape).T.astype(BF16)
    vT_ref[0] = lax.dot_general(wv_ref[...], y, NT_DIMS, preferred_element_type=F32).astype(BF16)


def _c_attn_kernel(qT_ref, k_ref, vT_ref, oT_ref, qz_scr, m_scr, l_scr, acc_scr):
    j = pl.program_id(3)
    grp = qT_ref.shape[1] // HEAD_DIM
    tq = qT_ref.shape[-1]

    @pl.when(j == 0)
    def _():
        m_scr[...] = jnp.full_like(m_scr, -jnp.inf)
        l_scr[...] = jnp.zeros_like(l_scr)
        acc_scr[...] = jnp.zeros_like(acc_scr)
        half = lax.broadcasted_iota(jnp.int32, (LANES, tq), 0) // HEAD_DIM
        keep = half == pl.program_id(1) % 2
        for g in range(grp):
            q = qT_ref[0, g * HEAD_DIM:(g + 1) * HEAD_DIM, :]
            q2 = jnp.concatenate([q, q], axis=0)
            qz_scr[g] = jnp.where(keep, q2, jnp.zeros_like(q2))

    k = k_ref[0]
    vT = vT_ref[0, 0]
    for g in range(grp):
        sT = jnp.dot(k, qz_scr[g], preferred_element_type=F32)
        m_old = m_scr[g]
        m_new = jnp.maximum(m_old, jnp.max(sT, axis=0, keepdims=True))
        p = jnp.exp(sT - m_new)
        alpha = jnp.exp(m_old - m_new)
        l_scr[g] = alpha * l_scr[g] + jnp.sum(p, axis=0, keepdims=True)
        acc_scr[g] = alpha * acc_scr[g] + jnp.dot(vT, p.astype(BF16), preferred_element_type=F32)
        m_scr[g] = m_new

    @pl.when(j == pl.num_programs(3) - 1)
    def _():
        for g in range(grp):
            oT_ref[0, g * HEAD_DIM:(g + 1) * HEAD_DIM, :] = (acc_scr[g] / l_scr[g]).astype(BF16)


def _c_out_kernel(oT_ref, wo_ref, h_ref, out_ref):
    o = oT_ref[0].astype(F32).T.astype(BF16)
    out_ref[0] = h_ref[0] + jnp.dot(o, wo_ref[...], preferred_element_type=F32)


def _mixer_c(h, g1, w_qkv, q_gain, k_gain, w_o):
    b, s, d = h.shape
    nq = d
    nk = C_KV_HEADS * HEAD_DIM
    tm = 512
    pos = jnp.arange(s)
    cr, sr = _rope_tables_T(pos // GRID_W, HEAD_DIM // 2, C_THETA)
    cc, sc = _rope_tables_T(pos % GRID_W, HEAD_DIM // 2, C_THETA)
    wq = w_qkv[:, :nq].T.astype(BF16)
    wk = w_qkv[:, nq:nq + nk].T.astype(BF16)
    wv = w_qkv[:, nq + nk:].T.astype(BF16)
    gq = jnp.broadcast_to(q_gain[:, None], (HEAD_DIM, tm))
    gk = jnp.broadcast_to(k_gain[:, None], (HEAD_DIM, tm))
    tab = pl.BlockSpec((HEAD_DIM // 4, tm), lambda bi, i: (0, i))
    qT, k, vT = pl.pallas_call(
        _c_proj_kernel,
        grid=(b, s // tm),
        in_specs=[
            pl.BlockSpec((1, tm, d), lambda bi, i: (bi, i, 0)),
            _resident((1, d)), _resident((nq, d)), _resident((nk, d)), _resident((nk, d)),
            _resident((HEAD_DIM, tm)), _resident((HEAD_DIM, tm)),
            tab, tab, tab, tab,
        ],
        out_specs=[
            pl.BlockSpec((1, nq, tm), lambda bi, i: (bi, 0, i)),
            pl.BlockSpec((1, tm, nk), lambda bi, i: (bi, i, 0)),
            pl.BlockSpec((1, nk, tm), lambda bi, i: (bi, 0, i)),
        ],
        out_shape=[
            jax.ShapeDtypeStruct((b, nq, s), BF16),
            jax.ShapeDtypeStruct((b, s, nk), BF16),
            jax.ShapeDtypeStruct((b, nk, s), BF16),
        ],
        compiler_params=_cparams(("parallel", "parallel")),
        name="c_proj",
    )(h, g1, wq, wk, wv, gq, gk, cr, sr, cc, sc)

    grp = nq // nk
    tq, tk = 256, 512
    gw = grp * HEAD_DIM
    oT = pl.pallas_call(
        _c_attn_kernel,
        grid=(b, C_KV_HEADS, s // tq, s // tk),
        in_specs=[
            pl.BlockSpec((1, gw, tq), lambda bi, hk, i, j: (bi, hk, i)),
            pl.BlockSpec((1, tk, LANES), lambda bi, hk, i, j: (bi, j, hk // 2)),
            pl.BlockSpec((1, 1, HEAD_DIM, tk), lambda bi, hk, i, j: (bi, hk, 0, j)),
        ],
        out_specs=pl.BlockSpec((1, gw, tq), lambda bi, hk, i, j: (bi, hk, i)),
        out_shape=jax.ShapeDtypeStruct((b, nq, s), BF16),
        scratch_shapes=[pltpu.VMEM((grp, LANES, tq), BF16),
                        pltpu.VMEM((grp, 1, tq), F32), pltpu.VMEM((grp, 1, tq), F32),
                        pltpu.VMEM((grp, HEAD_DIM, tq), F32)],
        compiler_params=_cparams(("parallel", "parallel", "parallel", "arbitrary")),
        name="c_attn",
    )(qT, k, vT.reshape(b, C_KV_HEADS, HEAD_DIM, s))

    return pl.pallas_call(
        _c_out_kernel,
        grid=(b, s // tm),
        in_specs=[
            pl.BlockSpec((1, d, tm), lambda bi, i: (bi, 0, i)),
            _resident((d, d)),
            pl.BlockSpec((1, tm, d), lambda bi, i: (bi, i, 0)),
        ],
        out_specs=pl.BlockSpec((1, tm, d), lambda bi, i: (bi, i, 0)),
        out_shape=jax.ShapeDtypeStruct((b, s, d), F32),
        compiler_params=_cparams(("parallel", "parallel")),
        name="c_out",
    )(oT, w_o.astype(BF16), h)


def _mlp_kernel(x_ref, g_ref, w1_ref, w2_ref, out_ref):
    x = x_ref[...]
    d = x.shape[-1]
    y = _rms_rows(x, g_ref[...]).astype(BF16)
    acc = x
    for c in range(w1_ref.shape[1] // d):
        cols = slice(c * d, (c + 1) * d)
        a = jnp.maximum(jnp.dot(y, w1_ref[:, cols], preferred_element_type=F32), 0.0)
        acc = acc + jnp.dot((a * a).astype(BF16), w2_ref[cols, :], preferred_element_type=F32)
    out_ref[...] = acc


def _mlp(h, g2, w1, w2):
    b, s, d = h.shape
    t = b * s
    tm = 512
    tok = pl.BlockSpec((tm, d), lambda i: (i, 0))
    out = pl.pallas_call(
        _mlp_kernel,
        grid=(t // tm,),
        in_specs=[tok, _resident((1, d)), _resident(w1.shape), _resident(w2.shape)],
        out_specs=tok,
        out_shape=jax.ShapeDtypeStruct((t, d), F32),
        compiler_params=_cparams(("parallel",)),
        name="mlp",
    )(h.reshape(t, d), g2, w1.astype(BF16), w2.astype(BF16))
    return out.reshape(b, s, d)


def kernel(x, norm1, norm2, a_wqkv, a_q_gain, a_k_gain, a_wo, b_win, b_conv, b_wout,
           c_wqkv, c_q_gain, c_k_gain, c_wo, mlp_w1, mlp_w2):
    h = x
    for i in range(norm1.shape[0]):
        kind, j = i % N_MIXERS, i // N_MIXERS
        g1 = norm1[i][None, :]
        if kind == 0:
            h = _mixer_a(h, g1, a_wqkv[j], a_q_gain[j], a_k_gain[j], a_wo[j])
        elif kind == 1:
            h = _mixer_b(h, g1, b_win[j], b_conv[j], b_wout[j])
        else:
            h = _mixer_c(h, g1, c_wqkv[j], c_q_gain[j], c_k_gain[j], c_wo[j])
        h = _mlp(h, norm2[i][None, :], mlp_w1[i], mlp_w2[i])
    return h
```

```python
import functools
import math

import jax
import jax.numpy as jnp
import numpy as np
from jax import lax
from jax.experimental import pallas as pl
from jax.experimental.pallas import tpu as pltpu

HEAD_DIM = 64
EPS = 1e-6
NEG_INF = -1e30
N_MIXERS = 3
A_WINDOWS = (128, 512, 2048)
A_DILATIONS = (1, 4, 16)
A_RADIUS = 64
ROPE_THETA = 500000.0
ROPE_DIMS = HEAD_DIM // 4
C_KV_HEADS = 4
C_THETA = 10000.0
GRID_W = 64
SCORE_SCALE = HEAD_DIM ** -0.5
LOG2E = math.log2(math.e)

LANES = 128
BF16_SUBLANES = 16
VMEM_LIMIT = 56 * 1024 * 1024

F32 = jnp.float32
BF16 = jnp.bfloat16

NT_DIMS = (((1,), (1,)), ((), ()))


def _cparams(sem):
    return pltpu.CompilerParams(dimension_semantics=sem, vmem_limit_bytes=VMEM_LIMIT)


def _resident(shape):
    nd = len(shape)
    return pl.BlockSpec(shape, lambda *_: (0,) * nd, pipeline_mode=pl.Buffered(1))


def _rms_rows(x, g):
    ms = jnp.mean(x * x, axis=-1, keepdims=True)
    return (x * lax.rsqrt(ms + EPS)) * g


def _head_norm_T(t, gain):
    n = t.shape[-1]
    t3 = t.reshape(t.shape[0] // HEAD_DIM, HEAD_DIM, n)
    ms = jnp.mean(t3 * t3, axis=1, keepdims=True)
    return (t3 * lax.rsqrt(ms + EPS)) * gain[None]


def _rot(a, b, c, s):
    return a * c - b * s, b * c + a * s


def _rope_tables_T(pos, dim, theta):
    inv = theta ** (-jnp.arange(0, dim, 2, dtype=F32) / dim)
    ang = inv[:, None] * pos.astype(F32)[None, :]
    return jnp.cos(ang), jnp.sin(ang)


def _a_proj_kernel(dil, *refs):
    n_slab = len(refs) - 9
    x_slabs = refs[:n_slab]
    g_ref, wt_ref, gq_ref, gk_ref, cos_ref, sin_ref, qT_ref, k_ref, vT_ref = refs[n_slab:]
    d = wt_ref.shape[1]
    tl = cos_ref.shape[-1]
    nsp = gq_ref.shape[-1] // tl
    half = ROPE_DIMS // 2

    for r0 in range(0, dil, nsp):
        streams = range(r0, r0 + nsp)
        x = jnp.concatenate(
            [jnp.concatenate([xs[0, pl.ds(r, tl, stride=dil), :] for xs in x_slabs], axis=1)
             for r in streams], axis=0)
        y = _rms_rows(x, g_ref[...]).astype(BF16)
        c = jnp.concatenate([cos_ref[r] for r in streams], axis=1)
        s = jnp.concatenate([sin_ref[r] for r in streams], axis=1)

        def qk(rows, gain, scale):
            t = lax.dot_general(wt_ref[rows, :], y, NT_DIMS, preferred_element_type=F32)
            tn = _head_norm_T(t, gain)
            ra, rb = _rot(tn[:, :half], tn[:, half:ROPE_DIMS], c[None], s[None])
            out = jnp.concatenate([ra, rb, tn[:, ROPE_DIMS:]], axis=1)
            if scale != 1.0:
                out = out * scale
            return out.reshape(t.shape)

        qT = qk(slice(0, d), gq_ref[...], SCORE_SCALE * LOG2E).astype(BF16)
        k = qk(slice(d, 2 * d), gk_ref[...], 1.0).T.astype(BF16)
        vT = lax.dot_general(wt_ref[2 * d:3 * d, :], y, NT_DIMS,
                             preferred_element_type=F32).astype(BF16)
        for a, r in enumerate(streams):
            k_ref[0, r] = k[a * tl:(a + 1) * tl]
            for t in range(tl // LANES):
                cols = slice(a * tl + t * LANES, a * tl + (t + 1) * LANES)
                qT_ref[0, r, t] = qT[:, cols]
                vT_ref[0, r, t] = vT[:, cols]


def _a_proj(h, g1, wt, q_gain, k_gain, dil):
    b, s, d = h.shape
    l = s // dil
    if dil == 1:
        tl, nsp = 4 * LANES, 1
    else:
        tl, nsp = LANES, 2
    tt = tl * dil
    pos = (jnp.arange(l)[None, :] * dil + jnp.arange(dil)[:, None]).reshape(-1)
    cosT, sinT = _rope_tables_T(pos, ROPE_DIMS, ROPE_THETA)
    cosT = cosT.reshape(-1, dil, l).transpose(1, 0, 2)
    sinT = sinT.reshape(-1, dil, l).transpose(1, 0, 2)
    gq = jnp.broadcast_to(q_gain[:, None], (HEAD_DIM, nsp * tl))
    gk = jnp.broadcast_to(k_gain[:, None], (HEAD_DIM, nsp * tl))
    n_slab = d // LANES
    slab = lambda c: pl.BlockSpec((1, tt, LANES), lambda bi, i: (bi, i, c))
    feat = pl.BlockSpec((1, dil, tl // LANES, d, LANES), lambda bi, i: (bi, 0, i, 0, 0))
    tab = pl.BlockSpec((dil, ROPE_DIMS // 2, tl), lambda bi, i: (0, 0, i))
    feat_shape = jax.ShapeDtypeStruct((b, dil, l // LANES, d, LANES), BF16)
    return pl.pallas_call(
        functools.partial(_a_proj_kernel, dil),
        grid=(b, s // tt),
        in_specs=[slab(c) for c in range(n_slab)] + [
            _resident((1, d)), _resident((3 * d, d)),
            _resident((HEAD_DIM, nsp * tl)), _resident((HEAD_DIM, nsp * tl)),
            tab, tab,
        ],
        out_specs=[feat, pl.BlockSpec((1, dil, tl, d), lambda bi, i: (bi, 0, i, 0)), feat],
        out_shape=[feat_shape, jax.ShapeDtypeStruct((b, dil, l, d), BF16), feat_shape],
        compiler_params=_cparams(("parallel", "parallel")),
        name="a_proj",
    )(*([h] * n_slab), g1, wt, gq, gk, cosT, sinT)


def _a_attn_kernel(qT_ref, kprev_ref, kmain_ref, knext_ref, vprev_ref, vmain_ref, vnext_ref,
                   bias_ref, o_ref, m_ref, l_ref, oT_scr, st_scr):
    i = pl.program_id(1)
    tb = qT_ref.shape[1]
    d = qT_ref.shape[2]
    n_heads = d // HEAD_DIM
    quad = 4 * HEAD_DIM
    win = 2 * LANES
    first_sel = jnp.where(i == 0, 0, 1)
    last_sel = jnp.where(i == pl.num_programs(1) - 1, 2, 1)
    own_head = (lax.broadcasted_iota(jnp.int32, (quad, 4 * LANES), 0) // HEAD_DIM
                == lax.broadcasted_iota(jnp.int32, (quad, 4 * LANES), 1) // LANES)
    zrows = jnp.zeros((A_RADIUS, 4 * LANES), BF16)

    st_scr[...] = jnp.zeros_like(st_scr)
    st_scr[1, n_heads:, :] = jnp.ones((LANES - n_heads, LANES), F32)

    for s in range(tb):
        lo, hi = s * LANES - A_RADIUS, s * LANES - A_RADIUS + win
        if s == 0:
            kw = jnp.concatenate([kprev_ref[0], kmain_ref[0, 0:hi]], axis=0)
            bias = bias_ref[first_sel]
        elif s == tb - 1:
            kw = jnp.concatenate([kmain_ref[0, lo:tb * LANES], knext_ref[0]], axis=0)
            bias = bias_ref[last_sel]
        else:
            kw = kmain_ref[0, lo:hi]
            bias = bias_ref[1]
        va = vprev_ref[0, 0] if s == 0 else vmain_ref[0, s - 1]
        vc = vnext_ref[0, 0] if s == tb - 1 else vmain_ref[0, s + 1]
        vcat = jnp.concatenate([va, vmain_ref[0, s], vc], axis=1)

        for u in range(n_heads // 4):
            q4 = qT_ref[0, s, u * quad:(u + 1) * quad, :]
            q4 = jnp.concatenate([q4] * 4, axis=1)
            qz = jnp.where(own_head, q4, jnp.zeros_like(q4))
            sT = jnp.dot(kw[:, u * quad:(u + 1) * quad], qz, preferred_element_type=F32) + bias
            m = jnp.max(sT, axis=0, keepdims=True)
            p = jnp.exp2(sT - m)
            l = jnp.sum(p, axis=0, keepdims=True)
            ppad = jnp.concatenate([zrows, p.astype(BF16), zrows], axis=0)
            for pr in range(2):
                h0 = 4 * u + 2 * pr
                c0 = 2 * pr * LANES
                out = jnp.dot(vcat[h0 * HEAD_DIM:(h0 + 2) * HEAD_DIM, :], ppad[:, c0:c0 + 2 * LANES],
                              preferred_element_type=F32)
                oT_scr[h0 * HEAD_DIM:(h0 + 1) * HEAD_DIM, :] = (
                    out[:HEAD_DIM, :LANES] / l[:, c0:c0 + LANES])
                oT_scr[(h0 + 1) * HEAD_DIM:(h0 + 2) * HEAD_DIM, :] = (
                    out[HEAD_DIM:, LANES:] / l[:, c0 + LANES:c0 + 2 * LANES])
            for g in range(4):
                st_scr[0, 4 * u + g:4 * u + g + 1, :] = m[:, g * LANES:(g + 1) * LANES]
                st_scr[1, 4 * u + g:4 * u + g + 1, :] = l[:, g * LANES:(g + 1) * LANES]

        rows = slice(s * LANES, (s + 1) * LANES)
        o_ref[0, rows, :] = oT_scr[...].T.astype(BF16)
        m_ref[0, rows, :] = st_scr[0].T
        l_ref[0, rows, :] = st_scr[1].T


def _band_bias():
    kk = np.arange(2 * LANES)[:, None]
    j = np.arange(LANES)[None, :]
    band = (kk >= j) & (kk <= j + 2 * A_RADIUS)
    variants = [band & (kk >= A_RADIUS), band, band & (kk < 2 * LANES - A_RADIUS)]
    bias = np.stack([np.where(v, 0.0, NEG_INF) for v in variants])
    return jnp.asarray(np.tile(bias, (1, 1, 4)), F32)


def _a_attn(qT, k, vT):
    b, dil, nt, d, _ = qT.shape
    n, l = b * dil, nt * LANES
    qT, k, vT = qT.reshape(n, nt, d, LANES), k.reshape(n, l, d), vT.reshape(n, nt, d, LANES)
    tb = min(4, nt // 2)
    assert tb >= 2 and nt % tb == 0
    hb = A_RADIUS
    per = tb * LANES // hb
    tok = lambda width: pl.BlockSpec((1, tb * LANES, width), lambda ni, i: (ni, i, 0))
    feat = pl.BlockSpec((1, tb, d, LANES), lambda ni, i: (ni, i, 0, 0))
    o, m, lsum = pl.pallas_call(
        _a_attn_kernel,
        grid=(n, nt // tb),
        in_specs=[
            feat,
            pl.BlockSpec((1, hb, d), lambda ni, i: (ni, jnp.maximum(i * per - 1, 0), 0)),
            tok(d),
            pl.BlockSpec((1, hb, d), lambda ni, i: (ni, jnp.minimum((i + 1) * per, l // hb - 1), 0)),
            pl.BlockSpec((1, 1, d, LANES), lambda ni, i: (ni, jnp.maximum(i * tb - 1, 0), 0, 0)),
            feat,
            pl.BlockSpec((1, 1, d, LANES), lambda ni, i: (ni, jnp.minimum((i + 1) * tb, nt - 1), 0, 0)),
            _resident((3, 2 * LANES, 4 * LANES)),
        ],
        out_specs=[tok(d), tok(LANES), tok(LANES)],
        out_shape=[
            jax.ShapeDtypeStruct((n, l, d), BF16),
            jax.ShapeDtypeStruct((n, l, LANES), F32),
            jax.ShapeDtypeStruct((n, l, LANES), F32),
        ],
        scratch_shapes=[pltpu.VMEM((d, LANES), F32), pltpu.VMEM((2, LANES, LANES), F32)],
        compiler_params=_cparams(("parallel", "parallel")),
        name="a_attn",
    )(qT, k, k, k, vT, vT, vT, _band_bias())
    return (o.reshape(b, dil, l, d), m.reshape(b, dil, l, LANES), lsum.reshape(b, dil, l, LANES))


def _a_out_kernel(n_heads, o0, o1, o2, m0, m1, m2, l0, l1, l2, e_ref, wo_ref, h_ref, out_ref,
                  o_scr, st_scr):
    tm = h_ref.shape[1]

    def stat_tokens(ref, slot):
        dil = ref.shape[1]
        if dil == 1:
            return ref[0, 0]
        for r in range(dil):
            st_scr[slot, pl.ds(r, tm // dil, stride=dil), :] = ref[0, r]
        return st_scr[slot]

    def out_tokens(ref):
        dil = ref.shape[1]
        if dil == 1:
            return ref[0, 0].astype(F32)
        for r in range(dil):
            blk = ref[0, r].astype(F32)
            for c in range(o_scr.shape[0]):
                o_scr[c, pl.ds(r, tm // dil, stride=dil), :] = blk[:, c * LANES:(c + 1) * LANES]
        return jnp.concatenate([o_scr[c] for c in range(o_scr.shape[0])], axis=1)

    ms = [stat_tokens(r, j) for j, r in enumerate((m0, m1, m2))]
    ls = [stat_tokens(r, 3 + j) for j, r in enumerate((l0, l1, l2))]
    mx = jnp.maximum(jnp.maximum(ms[0], ms[1]), ms[2])
    ws = [jnp.exp2(m - mx) * l for m, l in zip(ms, ls)]
    den = ws[0] + ws[1] + ws[2]
    head_lane = lax.broadcasted_iota(jnp.int32, den.shape, 1) < n_heads

    o = None
    for w, o_ref in zip(ws, (o0, o1, o2)):
        a = jnp.where(head_lane, w / den, 0.0)
        hi = a.astype(BF16).astype(F32)
        mid = (a - hi).astype(BF16).astype(F32)
        lo = (a - hi - mid).astype(BF16).astype(F32)
        packed = hi + pltpu.roll(mid, n_heads, axis=1) + pltpu.roll(lo, 2 * n_heads, axis=1)
        coef = jnp.dot(packed.astype(BF16), e_ref[...], preferred_element_type=F32)
        term = coef * out_tokens(o_ref)
        o = term if o is None else o + term
    out_ref[0] = h_ref[0] + jnp.dot(o.astype(BF16), wo_ref[...], preferred_element_type=F32)


def _head_indicator(d):
    n_heads = d // HEAD_DIM
    e = np.zeros((LANES, d), np.float32)
    for part in range(3):
        for h in range(n_heads):
            e[part * n_heads + h, h * HEAD_DIM:(h + 1) * HEAD_DIM] = 1.0
    return jnp.asarray(e, BF16)


def _a_out(parts, wo, h, tm):
    b, s, d = h.shape
    blk = lambda dil, width: pl.BlockSpec((1, dil, tm // dil, width), lambda bi, i: (bi, 0, i, 0))
    os_, ms_, ls_ = zip(*parts)
    dils = [o.shape[1] for o in os_]
    tok = pl.BlockSpec((1, tm, d), lambda bi, i: (bi, i, 0))
    return pl.pallas_call(
        functools.partial(_a_out_kernel, d // HEAD_DIM),
        grid=(b, s // tm),
        in_specs=([blk(dil, d) for dil in dils] + [blk(dil, LANES) for dil in dils] * 2
                  + [_resident((LANES, d)), _resident((d, d)), tok]),
        out_specs=tok,
        out_shape=jax.ShapeDtypeStruct((b, s, d), F32),
        scratch_shapes=[pltpu.VMEM((d // LANES, tm, LANES), F32), pltpu.VMEM((6, tm, LANES), F32)],
        compiler_params=_cparams(("parallel", "parallel")),
        name="a_out",
    )(*os_, *ms_, *ls_, _head_indicator(d), wo, h)


def _mixer_a(h, g1, w_qkv, q_gain, k_gain, w_o):
    d = h.shape[-1]
    parts = []
    for g, dil in enumerate(A_DILATIONS):
        assert A_WINDOWS[g] // (2 * dil) == A_RADIUS
        wt = w_qkv[:, g * 3 * d:(g + 1) * 3 * d].T.astype(BF16)
        parts.append(_a_attn(*_a_proj(h, g1, wt, q_gain[g], k_gain[g], dil)))
    return _a_out(parts, w_o.astype(BF16), h, 512)


def _b_in_kernel(x_ref, g_ref, w_ref, bg_ref, u_ref):
    d = x_ref.shape[-1]
    y = _rms_rows(x_ref[...], g_ref[...]).astype(BF16)
    bg_ref[...] = jnp.dot(y, w_ref[:, 0:d], preferred_element_type=F32).astype(BF16)
    cg = jnp.dot(y, w_ref[:, d:2 * d], preferred_element_type=F32)
    xt = jnp.dot(y, w_ref[:, 2 * d:3 * d], preferred_element_type=F32)
    u_ref[...] = (cg * xt).astype(BF16)


def _b_out_kernel(tiles_per_seq, u_ref, up_ref, un_ref, bg_ref, cw_ref, wo_ref, h_ref, out_ref, scr):
    i = pl.program_id(0)
    tm = u_ref.shape[0]
    first = (i % tiles_per_seq) == 0
    last = (i % tiles_per_seq) == tiles_per_seq - 1
    hb = up_ref.shape[0]
    scr[0:hb, :] = jnp.where(first, 0.0, up_ref[...].astype(F32))
    scr[hb:hb + tm, :] = u_ref[...].astype(F32)
    scr[hb + tm:2 * hb + tm, :] = jnp.where(last, 0.0, un_ref[...].astype(F32))
    cw = cw_ref[...]
    y = (scr[hb - 1:hb - 1 + tm, :] * cw[0:1] + scr[hb:hb + tm, :] * cw[1:2]
         + scr[hb + 1:hb + 1 + tm, :] * cw[2:3])
    z = (bg_ref[...].astype(F32) * y).astype(BF16)
    out_ref[...] = h_ref[...] + jnp.dot(z, wo_ref[...], preferred_element_type=F32)


def _mixer_b(h, g1, w_in, conv_w, w_out):
    b, s, d = h.shape
    t = b * s
    tm = 512
    h2 = h.reshape(t, d)
    tok = pl.BlockSpec((tm, d), lambda i: (i, 0))
    bg, u = pl.pallas_call(
        _b_in_kernel,
        grid=(t // tm,),
        in_specs=[tok, _resident((1, d)), _resident((d, 3 * d))],
        out_specs=[tok, tok],
        out_shape=[jax.ShapeDtypeStruct((t, d), BF16)] * 2,
        compiler_params=_cparams(("parallel",)),
        name="b_in",
    )(h2, g1, w_in.astype(BF16))
    hb = BF16_SUBLANES
    per = tm // hb
    out = pl.pallas_call(
        functools.partial(_b_out_kernel, s // tm),
        grid=(t // tm,),
        in_specs=[
            tok,
            pl.BlockSpec((hb, d), lambda i: (jnp.maximum(i * per - 1, 0), 0)),
            pl.BlockSpec((hb, d), lambda i: (jnp.minimum((i + 1) * per, t // hb - 1), 0)),
            tok, _resident((3, d)), _resident((d, d)), tok,
        ],
        out_specs=tok,
        out_shape=jax.ShapeDtypeStruct((t, d), F32),
        scratch_shapes=[pltpu.VMEM((tm + 2 * hb, d), F32)],
        compiler_params=_cparams(("parallel",)),
        name="b_out",
    )(u, u, u, bg, conv_w, w_out.astype(BF16), h2)
    return out.reshape(b, s, d)


def _axial_rope_T(tn, cr, sr, cc, sc):
    q = HEAD_DIM // 4
    a0, b0 = _rot(tn[:, 0:q], tn[:, q:2 * q], cr[None], sr[None])
    a1, b1 = _rot(tn[:, 2 * q:3 * q], tn[:, 3 * q:], cc[None], sc[None])
    return jnp.concatenate([a0, b0, a1, b1], axis=1)


def _c_proj_kernel(x_ref, g_ref, wq_ref, wk_ref, wv_ref, gq_ref, gk_ref, cr_ref, sr_ref, cc_ref, sc_ref,
                   qT_ref, k_ref, vT_ref):
    y = _rms_rows(x_ref[0], g_ref[...]).astype(BF16)
    tabs = (cr_ref[...], sr_ref[...], cc_ref[...], sc_ref[...])
    qT = lax.dot_general(wq_ref[...], y, NT_DIMS, preferred_element_type=F32)
    qn = _axial_rope_T(_head_norm_T(qT, gq_ref[...]), *tabs) * (SCORE_SCALE * LOG2E)
    qT_ref[0] = qn.reshape(qT.shape).astype(BF16)
    kT = lax.dot_general(wk_ref[...], y, NT_DIMS, preferred_element_type=F32)
    kn = _axial_rope_T(_head_norm_T(kT, gk_ref[...]), *tabs)
    k_ref[0] = kn.reshape(kT.shape).T.astype(BF16)
    vT = lax.dot_general(wv_ref[...], y, NT_DIMS, preferred_element_type=F32).astype(BF16)
    vT_ref[0, :, 0] = vT.reshape(vT_ref.shape[1], HEAD_DIM, vT.shape[-1])


def _c_attn_kernel(qT_ref, k_ref, vT_ref, oT_ref, qz_scr, s_scr, cm_scr, m_scr, l_scr, acc_scr):
    grp = qT_ref.shape[1] // HEAD_DIM
    tq = qT_ref.shape[-1]
    nc, tk = vT_ref.shape[2], vT_ref.shape[-1]

    m_scr[...] = jnp.full_like(m_scr, -jnp.inf)
    l_scr[...] = jnp.zeros_like(l_scr)
    acc_scr[...] = jnp.zeros_like(acc_scr)
    half = lax.broadcasted_iota(jnp.int32, (LANES, tq), 0) // HEAD_DIM
    keep = half == pl.program_id(1) % 2
    for g in range(grp):
        q = qT_ref[0, g * HEAD_DIM:(g + 1) * HEAD_DIM, :]
        q2 = jnp.concatenate([q, q], axis=0)
        qz_scr[:, g * tq:(g + 1) * tq] = jnp.where(keep, q2, jnp.zeros_like(q2))

    def scores(c, slot):
        k = k_ref[0, pl.ds(pl.multiple_of(c * tk, tk), tk), :]
        sT = jnp.dot(k, qz_scr[...], preferred_element_type=F32)
        s_scr[slot] = sT
        cm_scr[slot] = jnp.max(sT, axis=0, keepdims=True)

    def absorb(c, slot):
        m_old = m_scr[...]
        m_new = jnp.maximum(m_old, cm_scr[slot])
        p = jnp.exp2(s_scr[slot] - m_new)
        alpha = jnp.exp2(m_old - m_new)
        l_scr[...] = alpha * l_scr[...] + jnp.sum(p, axis=0, keepdims=True)
        acc_scr[...] = alpha * acc_scr[...] + jnp.dot(vT_ref[0, 0, c], p.astype(BF16),
                                                      preferred_element_type=F32)
        m_scr[...] = m_new

    scores(0, 0)

    def pair(c2, carry):
        c = 2 * c2
        scores(c + 1, 1)
        absorb(c, 0)
        scores(c + 2, 0)
        absorb(c + 1, 1)
        return carry

    lax.fori_loop(0, nc // 2 - 1, pair, 0)
    scores(nc - 1, 1)
    absorb(nc - 2, 0)
    absorb(nc - 1, 1)

    o = acc_scr[...] / l_scr[...]
    for g in range(grp):
        oT_ref[0, g * HEAD_DIM:(g + 1) * HEAD_DIM, :] = o[:, g * tq:(g + 1) * tq].astype(BF16)


def _c_out_kernel(oT_ref, wo_ref, h_ref, out_ref):
    o = oT_ref[0].astype(F32).T.astype(BF16)
    out_ref[0] = h_ref[0] + jnp.dot(o, wo_ref[...], preferred_element_type=F32)


def _mixer_c(h, g1, w_qkv, q_gain, k_gain, w_o):
    b, s, d = h.shape
    nq = d
    nk = C_KV_HEADS * HEAD_DIM
    tm = 512
    nc = s // tm
    assert nc % 2 == 0 and nc >= 4
    pos = jnp.arange(s)
    cr, sr = _rope_tables_T(pos // GRID_W, HEAD_DIM // 2, C_THETA)
    cc, sc = _rope_tables_T(pos % GRID_W, HEAD_DIM // 2, C_THETA)
    wq = w_qkv[:, :nq].T.astype(BF16)
    wk = w_qkv[:, nq:nq + nk].T.astype(BF16)
    wv = w_qkv[:, nq + nk:].T.astype(BF16)
    gq = jnp.broadcast_to(q_gain[:, None], (HEAD_DIM, tm))
    gk = jnp.broadcast_to(k_gain[:, None], (HEAD_DIM, tm))
    tab = pl.BlockSpec((HEAD_DIM // 4, tm), lambda bi, i: (0, i))
    qT, k, vT = pl.pallas_call(
        _c_proj_kernel,
        grid=(b, nc),
        in_specs=[
            pl.BlockSpec((1, tm, d), lambda bi, i: (bi, i, 0)),
            _resident((1, d)), _resident((nq, d)), _resident((nk, d)), _resident((nk, d)),
            _resident((HEAD_DIM, tm)), _resident((HEAD_DIM, tm)),
            tab, tab, tab, tab,
        ],
        out_specs=[
            pl.BlockSpec((1, nq, tm), lambda bi, i: (bi, 0, i)),
            pl.BlockSpec((1, tm, nk), lambda bi, i: (bi, i, 0)),
            pl.BlockSpec((1, C_KV_HEADS, 1, HEAD_DIM, tm), lambda bi, i: (bi, 0, i, 0, 0)),
        ],
        out_shape=[
            jax.ShapeDtypeStruct((b, nq, s), BF16),
            jax.ShapeDtypeStruct((b, s, nk), BF16),
            jax.ShapeDtypeStruct((b, C_KV_HEADS, nc, HEAD_DIM, tm), BF16),
        ],
        compiler_params=_cparams(("parallel", "parallel")),
        name="c_proj",
    )(h, g1, wq, wk, wv, gq, gk, cr, sr, cc, sc)

    grp = nq // nk
    tq = 256
    gw = grp * HEAD_DIM
    oT = pl.pallas_call(
        _c_attn_kernel,
        grid=(b, C_KV_HEADS, s // tq),
        in_specs=[
            pl.BlockSpec((1, gw, tq), lambda bi, hk, i: (bi, hk, i)),
            pl.BlockSpec((1, s, LANES), lambda bi, hk, i: (bi, 0, hk // 2)),
            pl.BlockSpec((1, 1, nc, HEAD_DIM, tm), lambda bi, hk, i: (bi, hk, 0, 0, 0)),
        ],
        out_specs=pl.BlockSpec((1, gw, tq), lambda bi, hk, i: (bi, hk, i)),
        out_shape=jax.ShapeDtypeStruct((b, nq, s), BF16),
        scratch_shapes=[pltpu.VMEM((LANES, grp * tq), BF16),
                        pltpu.VMEM((2, tm, grp * tq), F32), pltpu.VMEM((2, 1, grp * tq), F32),
                        pltpu.VMEM((1, grp * tq), F32), pltpu.VMEM((1, grp * tq), F32),
                        pltpu.VMEM((HEAD_DIM, grp * tq), F32)],
        compiler_params=_cparams(("parallel", "parallel", "parallel")),
        name="c_attn",
    )(qT, k, vT)

    return pl.pallas_call(
        _c_out_kernel,
        grid=(b, nc),
        in_specs=[
            pl.BlockSpec((1, d, tm), lambda bi, i: (bi, 0, i)),
            _resident((d, d)),
            pl.BlockSpec((1, tm, d), lambda bi, i: (bi, i, 0)),
        ],
        out_specs=pl.BlockSpec((1, tm, d), lambda bi, i: (bi, i, 0)),
        out_shape=jax.ShapeDtypeStruct((b, s, d), F32),
        compiler_params=_cparams(("parallel", "parallel")),
        name="c_out",
    )(oT, w_o.astype(BF16), h)


def _mlp_kernel(x_ref, g_ref, w1_ref, w2_ref, out_ref):
    x = x_ref[...]
    d = x.shape[-1]
    y = _rms_rows(x, g_ref[...]).astype(BF16)
    acc = x
    for c in range(w1_ref.shape[1] // d):
        cols = slice(c * d, (c + 1) * d)
        a = jnp.maximum(jnp.dot(y, w1_ref[:, cols], preferred_element_type=F32), 0.0)
        acc = acc + jnp.dot((a * a).astype(BF16), w2_ref[cols, :], preferred_element_type=F32)
    out_ref[...] = acc


def _mlp(h, g2, w1, w2):
    b, s, d = h.shape
    t = b * s
    tm = 512
    tok = pl.BlockSpec((tm, d), lambda i: (i, 0))
    out = pl.pallas_call(
        _mlp_kernel,
        grid=(t // tm,),
        in_specs=[tok, _resident((1, d)), _resident(w1.shape), _resident(w2.shape)],
        out_specs=tok,
        out_shape=jax.ShapeDtypeStruct((t, d), F32),
        compiler_params=_cparams(("parallel",)),
        name="mlp",
    )(h.reshape(t, d), g2, w1.astype(BF16), w2.astype(BF16))
    return out.reshape(b, s, d)


def kernel(x, norm1, norm2, a_wqkv, a_q_gain, a_k_gain, a_wo, b_win, b_conv, b_wout,
           c_wqkv, c_q_gain, c_k_gain, c_wo, mlp_w1, mlp_w2):
    h = x
    for i in range(norm1.shape[0]):
        kind, j = i % N_MIXERS, i // N_MIXERS
        g1 = norm1[i][None, :]
        if kind == 0:
            h = _mixer_a(h, g1, a_wqkv[j], a_q_gain[j], a_k_gain[j], a_wo[j])
        elif kind == 1:
            h = _mixer_b(h, g1, b_win[j], b_conv[j], b_wout[j])
        else:
            h = _mixer_c(h, g1, c_wqkv[j], c_q_gain[j], c_k_gain[j], c_wo[j])
        h = _mlp(h, norm2[i][None, :], mlp_w1[i], mlp_w2[i])
    return h
```

```python
import functools
import math

import jax
import jax.numpy as jnp
import numpy as np
from jax import lax
from jax.experimental import pallas as pl
from jax.experimental.pallas import tpu as pltpu

HEAD_DIM = 64
EPS = 1e-6
NEG_INF = -1e30
N_MIXERS = 3
A_WINDOWS = (128, 512, 2048)
A_DILATIONS = (1, 4, 16)
A_RADIUS = 64
ROPE_THETA = 500000.0
ROPE_DIMS = HEAD_DIM // 4
C_KV_HEADS = 4
C_THETA = 10000.0
GRID_W = 64
SCORE_SCALE = HEAD_DIM ** -0.5
LOG2E = math.log2(math.e)

LANES = 128
BF16_SUBLANES = 16
PAIR_ROWS = 2 * HEAD_DIM + BF16_SUBLANES
SAFE_LOG2_SCORE = 64.0
PASS1_UNROLL = 4
VMEM_LIMIT = 56 * 1024 * 1024

F32 = jnp.float32
BF16 = jnp.bfloat16

NT_DIMS = (((1,), (1,)), ((), ()))


def _cparams(sem):
    return pltpu.CompilerParams(dimension_semantics=sem, vmem_limit_bytes=VMEM_LIMIT)


def _resident(shape):
    nd = len(shape)
    return pl.BlockSpec(shape, lambda *_: (0,) * nd, pipeline_mode=pl.Buffered(1))


def _rms_rows(x, g):
    ms = jnp.mean(x * x, axis=-1, keepdims=True)
    return (x * lax.rsqrt(ms + EPS)) * g


def _head_norm_T(t, gain):
    n = t.shape[-1]
    t3 = t.reshape(t.shape[0] // HEAD_DIM, HEAD_DIM, n)
    ms = jnp.mean(t3 * t3, axis=1, keepdims=True)
    return (t3 * lax.rsqrt(ms + EPS)) * gain[None]


def _rot(a, b, c, s):
    return a * c - b * s, b * c + a * s


def _rope_tables_T(pos, dim, theta):
    inv = theta ** (-jnp.arange(0, dim, 2, dtype=F32) / dim)
    ang = inv[:, None] * pos.astype(F32)[None, :]
    return jnp.cos(ang), jnp.sin(ang)


def _a_proj_kernel(dil, *refs):
    n_slab = len(refs) - 9
    x_slabs = refs[:n_slab]
    g_ref, wt_ref, gq_ref, gk_ref, cos_ref, sin_ref, qT_ref, k_ref, vT_ref = refs[n_slab:]
    d = wt_ref.shape[1]
    tl = cos_ref.shape[-1]
    nsp = gq_ref.shape[-1] // tl
    half = ROPE_DIMS // 2

    for r0 in range(0, dil, nsp):
        streams = range(r0, r0 + nsp)
        x = jnp.concatenate(
            [jnp.concatenate([xs[0, pl.ds(r, tl, stride=dil), :] for xs in x_slabs], axis=1)
             for r in streams], axis=0)
        y = _rms_rows(x, g_ref[...]).astype(BF16)
        c = jnp.concatenate([cos_ref[r] for r in streams], axis=1)
        s = jnp.concatenate([sin_ref[r] for r in streams], axis=1)

        def qk(rows, gain, scale):
            t = lax.dot_general(wt_ref[rows, :], y, NT_DIMS, preferred_element_type=F32)
            tn = _head_norm_T(t, gain)
            ra, rb = _rot(tn[:, :half], tn[:, half:ROPE_DIMS], c[None], s[None])
            out = jnp.concatenate([ra, rb, tn[:, ROPE_DIMS:]], axis=1)
            if scale != 1.0:
                out = out * scale
            return out.reshape(t.shape)

        qT = qk(slice(0, d), gq_ref[...], SCORE_SCALE * LOG2E).astype(BF16)
        k = qk(slice(d, 2 * d), gk_ref[...], 1.0).T.astype(BF16)
        vT = lax.dot_general(wt_ref[2 * d:3 * d, :], y, NT_DIMS,
                             preferred_element_type=F32).astype(BF16)
        npair, cols = d // (2 * HEAD_DIM), vT.shape[-1]
        ones_row = (lax.broadcasted_iota(jnp.int32, (npair, BF16_SUBLANES, cols), 1) == 0).astype(BF16)
        vT = jnp.concatenate([vT.reshape(npair, 2 * HEAD_DIM, cols), ones_row], axis=1)
        vT = vT.reshape(npair * PAIR_ROWS, cols)
        for a, r in enumerate(streams):
            k_ref[0, r] = k[a * tl:(a + 1) * tl]
            for t in range(tl // LANES):
                cols = slice(a * tl + t * LANES, a * tl + (t + 1) * LANES)
                qT_ref[0, r, t] = qT[:, cols]
                vT_ref[0, r, t] = vT[:, cols]


def _a_proj(h, g1, wt, q_gain, k_gain, dil):
    b, s, d = h.shape
    l = s // dil
    if dil == 1:
        tl, nsp = 4 * LANES, 1
    else:
        tl, nsp = LANES, 2
    tt = tl * dil
    pos = (jnp.arange(l)[None, :] * dil + jnp.arange(dil)[:, None]).reshape(-1)
    cosT, sinT = _rope_tables_T(pos, ROPE_DIMS, ROPE_THETA)
    cosT = cosT.reshape(-1, dil, l).transpose(1, 0, 2)
    sinT = sinT.reshape(-1, dil, l).transpose(1, 0, 2)
    gq = jnp.broadcast_to(q_gain[:, None], (HEAD_DIM, nsp * tl))
    gk = jnp.broadcast_to(k_gain[:, None], (HEAD_DIM, nsp * tl))
    n_slab = d // LANES
    slab = lambda c: pl.BlockSpec((1, tt, LANES), lambda bi, i: (bi, i, c))
    vd = d // (2 * HEAD_DIM) * PAIR_ROWS
    feat = lambda rows: pl.BlockSpec((1, dil, tl // LANES, rows, LANES), lambda bi, i: (bi, 0, i, 0, 0))
    tab = pl.BlockSpec((dil, ROPE_DIMS // 2, tl), lambda bi, i: (0, 0, i))
    feat_shape = lambda rows: jax.ShapeDtypeStruct((b, dil, l // LANES, rows, LANES), BF16)
    return pl.pallas_call(
        functools.partial(_a_proj_kernel, dil),
        grid=(b, s // tt),
        in_specs=[slab(c) for c in range(n_slab)] + [
            _resident((1, d)), _resident((3 * d, d)),
            _resident((HEAD_DIM, nsp * tl)), _resident((HEAD_DIM, nsp * tl)),
            tab, tab,
        ],
        out_specs=[feat(d), pl.BlockSpec((1, dil, tl, d), lambda bi, i: (bi, 0, i, 0)), feat(vd)],
        out_shape=[feat_shape(d), jax.ShapeDtypeStruct((b, dil, l, d), BF16), feat_shape(vd)],
        compiler_params=_cparams(("parallel", "parallel")),
        name="a_proj",
    )(*([h] * n_slab), g1, wt, gq, gk, cosT, sinT)


def _a_attn_kernel(qT_ref, kprev_ref, kmain_ref, knext_ref, vprev_ref, vmain_ref, vnext_ref,
                   bias_ref, cap_ref, o_ref, m_ref, l_ref, oT_scr, st_scr):
    i = pl.program_id(1)
    tb = qT_ref.shape[1]
    d = qT_ref.shape[2]
    n_heads = d // HEAD_DIM
    quad = 4 * HEAD_DIM
    win = 2 * LANES
    first_sel = jnp.where(i == 0, 0, 1)
    last_sel = jnp.where(i == pl.num_programs(1) - 1, 2, 1)
    own_head = (lax.broadcasted_iota(jnp.int32, (quad, 4 * LANES), 0) // HEAD_DIM
                == lax.broadcasted_iota(jnp.int32, (quad, 4 * LANES), 1) // LANES)
    zrows = jnp.zeros((A_RADIUS, 4 * LANES), BF16)

    def window(s):
        lo, hi = s * LANES - A_RADIUS, s * LANES - A_RADIUS + win
        if s == 0:
            kw = jnp.concatenate([kprev_ref[0], kmain_ref[0, 0:hi]], axis=0)
            sel = first_sel
        elif s == tb - 1:
            kw = jnp.concatenate([kmain_ref[0, lo:tb * LANES], knext_ref[0]], axis=0)
            sel = last_sel
        else:
            kw = kmain_ref[0, lo:hi]
            sel = 1
        va = vprev_ref[0, 0] if s == 0 else vmain_ref[0, s - 1]
        vc = vnext_ref[0, 0] if s == tb - 1 else vmain_ref[0, s + 1]
        return kw, sel, jnp.concatenate([va, vmain_ref[0, s], vc], axis=1)

    def quad_scores(s, u, kw):
        q4 = qT_ref[0, s, u * quad:(u + 1) * quad, :]
        q4 = jnp.concatenate([q4] * 4, axis=1)
        qz = jnp.where(own_head, q4, jnp.zeros_like(q4))
        return jnp.dot(kw[:, u * quad:(u + 1) * quad], qz, preferred_element_type=F32)

    def pair_values(vcat, pair, p, pr):
        ppad = jnp.concatenate([zrows, p, zrows], axis=0)
        c0 = 2 * pr * LANES
        return jnp.dot(vcat[pair * PAIR_ROWS:(pair + 1) * PAIR_ROWS, :], ppad[:, c0:c0 + 2 * LANES],
                       preferred_element_type=F32)

    def finish(s):
        rows = slice(s * LANES, (s + 1) * LANES)
        o_ref[0, rows, :] = oT_scr[...].T.astype(BF16)
        m_ref[0, rows, :] = st_scr[0].T
        l_ref[0, rows, :] = st_scr[1].T

    def init_stats():
        st_scr[...] = jnp.zeros_like(st_scr)
        st_scr[1, n_heads:, :] = jnp.ones((LANES - n_heads, LANES), F32)

    init_stats()
    lmin = lmax = None
    for s in range(tb):
        kw, sel, vcat = window(s)
        cap = cap_ref[sel]
        for u in range(n_heads // 4):
            p = jnp.minimum(jnp.exp2(quad_scores(s, u, kw)).astype(BF16), cap)
            for pr in range(2):
                h0 = 4 * u + 2 * pr
                out = pair_values(vcat, h0 // 2, p, pr)
                for hh in range(2):
                    l = out[2 * HEAD_DIM:2 * HEAD_DIM + 1, hh * LANES:(hh + 1) * LANES]
                    oT_scr[(h0 + hh) * HEAD_DIM:(h0 + hh + 1) * HEAD_DIM, :] = (
                        out[hh * HEAD_DIM:(hh + 1) * HEAD_DIM, hh * LANES:(hh + 1) * LANES] / l)
                    st_scr[1, h0 + hh:h0 + hh + 1, :] = l
                    lmin = l if lmin is None else jnp.minimum(lmin, l)
                    lmax = l if lmax is None else jnp.maximum(lmax, l)
        finish(s)
    in_range = jnp.logical_and(jnp.max(lmax) <= 2.0 ** SAFE_LOG2_SCORE,
                               jnp.min(lmin) >= 2.0 ** -SAFE_LOG2_SCORE)

    @pl.when(jnp.logical_not(in_range))
    def _():
        init_stats()
        for s in range(tb):
            kw, sel, vcat = window(s)
            bias = bias_ref[sel]
            for u in range(n_heads // 4):
                sT = quad_scores(s, u, kw) + bias
                m = jnp.max(sT, axis=0, keepdims=True)
                p = jnp.exp2(sT - m)
                l = jnp.sum(p, axis=0, keepdims=True)
                for pr in range(2):
                    h0 = 4 * u + 2 * pr
                    out = pair_values(vcat, h0 // 2, p.astype(BF16), pr)
                    for hh in range(2):
                        cols = slice((2 * pr + hh) * LANES, (2 * pr + hh + 1) * LANES)
                        oT_scr[(h0 + hh) * HEAD_DIM:(h0 + hh + 1) * HEAD_DIM, :] = (
                            out[hh * HEAD_DIM:(hh + 1) * HEAD_DIM, hh * LANES:(hh + 1) * LANES] / l[:, cols])
                for g in range(4):
                    st_scr[0, 4 * u + g:4 * u + g + 1, :] = m[:, g * LANES:(g + 1) * LANES]
                    st_scr[1, 4 * u + g:4 * u + g + 1, :] = l[:, g * LANES:(g + 1) * LANES]
            finish(s)


def _band_masks():
    kk = np.arange(2 * LANES)[:, None]
    j = np.arange(LANES)[None, :]
    band = (kk >= j) & (kk <= j + 2 * A_RADIUS)
    variants = np.stack([band & (kk >= A_RADIUS), band, band & (kk < 2 * LANES - A_RADIUS)])
    variants = np.tile(variants, (1, 1, 4))
    bias = jnp.asarray(np.where(variants, 0.0, NEG_INF), F32)
    cap = jnp.asarray(np.where(variants, float(jnp.finfo(BF16).max), 0.0), BF16)
    return bias, cap


def _a_attn(qT, k, vT):
    b, dil, nt, d, _ = qT.shape
    vd = vT.shape[3]
    n, l = b * dil, nt * LANES
    qT, k, vT = qT.reshape(n, nt, d, LANES), k.reshape(n, l, d), vT.reshape(n, nt, vd, LANES)
    tb = min(4, nt // 2)
    assert tb >= 2 and nt % tb == 0
    hb = A_RADIUS
    per = tb * LANES // hb
    tok = lambda width: pl.BlockSpec((1, tb * LANES, width), lambda ni, i: (ni, i, 0))
    feat = lambda rows: pl.BlockSpec((1, tb, rows, LANES), lambda ni, i: (ni, i, 0, 0))
    o, m, lsum = pl.pallas_call(
        _a_attn_kernel,
        grid=(n, nt // tb),
        in_specs=[
            feat(d),
            pl.BlockSpec((1, hb, d), lambda ni, i: (ni, jnp.maximum(i * per - 1, 0), 0)),
            tok(d),
            pl.BlockSpec((1, hb, d), lambda ni, i: (ni, jnp.minimum((i + 1) * per, l // hb - 1), 0)),
            pl.BlockSpec((1, 1, vd, LANES), lambda ni, i: (ni, jnp.maximum(i * tb - 1, 0), 0, 0)),
            feat(vd),
            pl.BlockSpec((1, 1, vd, LANES), lambda ni, i: (ni, jnp.minimum((i + 1) * tb, nt - 1), 0, 0)),
            _resident((3, 2 * LANES, 4 * LANES)),
            _resident((3, 2 * LANES, 4 * LANES)),
        ],
        out_specs=[tok(d), tok(LANES), tok(LANES)],
        out_shape=[
            jax.ShapeDtypeStruct((n, l, d), BF16),
            jax.ShapeDtypeStruct((n, l, LANES), F32),
            jax.ShapeDtypeStruct((n, l, LANES), F32),
        ],
        scratch_shapes=[pltpu.VMEM((d, LANES), F32), pltpu.VMEM((2, LANES, LANES), F32)],
        compiler_params=_cparams(("parallel", "parallel")),
        name="a_attn",
    )(qT, k, k, k, vT, vT, vT, *_band_masks())
    return (o.reshape(b, dil, l, d), m.reshape(b, dil, l, LANES), lsum.reshape(b, dil, l, LANES))


def _a_out_kernel(n_heads, o0, o1, o2, m0, m1, m2, l0, l1, l2, e_ref, wo_ref, h_ref, out_ref,
                  o_scr, st_scr):
    tm = h_ref.shape[1]

    def stat_tokens(ref, slot):
        dil = ref.shape[1]
        if dil == 1:
            return ref[0, 0]
        for r in range(dil):
            st_scr[slot, pl.ds(r, tm // dil, stride=dil), :] = ref[0, r]
        return st_scr[slot]

    def out_tokens(ref):
        dil = ref.shape[1]
        if dil == 1:
            return ref[0, 0].astype(F32)
        for r in range(dil):
            blk = ref[0, r].astype(F32)
            for c in range(o_scr.shape[0]):
                o_scr[c, pl.ds(r, tm // dil, stride=dil), :] = blk[:, c * LANES:(c + 1) * LANES]
        return jnp.concatenate([o_scr[c] for c in range(o_scr.shape[0])], axis=1)

    ms = [stat_tokens(r, j) for j, r in enumerate((m0, m1, m2))]
    ls = [stat_tokens(r, 3 + j) for j, r in enumerate((l0, l1, l2))]
    mx = jnp.maximum(jnp.maximum(ms[0], ms[1]), ms[2])
    ws = [jnp.exp2(m - mx) * l for m, l in zip(ms, ls)]
    den = ws[0] + ws[1] + ws[2]
    head_lane = lax.broadcasted_iota(jnp.int32, den.shape, 1) < n_heads

    o = None
    for w, o_ref in zip(ws, (o0, o1, o2)):
        a = jnp.where(head_lane, w / den, 0.0)
        hi = a.astype(BF16).astype(F32)
        mid = (a - hi).astype(BF16).astype(F32)
        lo = (a - hi - mid).astype(BF16).astype(F32)
        packed = hi + pltpu.roll(mid, n_heads, axis=1) + pltpu.roll(lo, 2 * n_heads, axis=1)
        coef = jnp.dot(packed.astype(BF16), e_ref[...], preferred_element_type=F32)
        term = coef * out_tokens(o_ref)
        o = term if o is None else o + term
    out_ref[0] = h_ref[0] + jnp.dot(o.astype(BF16), wo_ref[...], preferred_element_type=F32)


def _head_indicator(d):
    n_heads = d // HEAD_DIM
    e = np.zeros((LANES, d), np.float32)
    for part in range(3):
        for h in range(n_heads):
            e[part * n_heads + h, h * HEAD_DIM:(h + 1) * HEAD_DIM] = 1.0
    return jnp.asarray(e, BF16)


def _a_out(parts, wo, h, tm):
    b, s, d = h.shape
    blk = lambda dil, width: pl.BlockSpec((1, dil, tm // dil, width), lambda bi, i: (bi, 0, i, 0))
    os_, ms_, ls_ = zip(*parts)
    dils = [o.shape[1] for o in os_]
    tok = pl.BlockSpec((1, tm, d), lambda bi, i: (bi, i, 0))
    return pl.pallas_call(
        functools.partial(_a_out_kernel, d // HEAD_DIM),
        grid=(b, s // tm),
        in_specs=([blk(dil, d) for dil in dils] + [blk(dil, LANES) for dil in dils] * 2
                  + [_resident((LANES, d)), _resident((d, d)), tok]),
        out_specs=tok,
        out_shape=jax.ShapeDtypeStruct((b, s, d), F32),
        scratch_shapes=[pltpu.VMEM((d // LANES, tm, LANES), F32), pltpu.VMEM((6, tm, LANES), F32)],
        compiler_params=_cparams(("parallel", "parallel")),
        name="a_out",
    )(*os_, *ms_, *ls_, _head_indicator(d), wo, h)


def _mixer_a(h, g1, w_qkv, q_gain, k_gain, w_o):
    d = h.shape[-1]
    parts = []
    for g, dil in enumerate(A_DILATIONS):
        assert A_WINDOWS[g] // (2 * dil) == A_RADIUS
        wt = w_qkv[:, g * 3 * d:(g + 1) * 3 * d].T.astype(BF16)
        parts.append(_a_attn(*_a_proj(h, g1, wt, q_gain[g], k_gain[g], dil)))
    return _a_out(parts, w_o.astype(BF16), h, 512)


def _b_in_kernel(x_ref, g_ref, w_ref, bg_ref, u_ref):
    d = x_ref.shape[-1]
    y = _rms_rows(x_ref[...], g_ref[...]).astype(BF16)
    bg_ref[...] = jnp.dot(y, w_ref[:, 0:d], preferred_element_type=F32).astype(BF16)
    cg = jnp.dot(y, w_ref[:, d:2 * d], preferred_element_type=F32)
    xt = jnp.dot(y, w_ref[:, 2 * d:3 * d], preferred_element_type=F32)
    u_ref[...] = (cg * xt).astype(BF16)


def _b_out_kernel(tiles_per_seq, u_ref, up_ref, un_ref, bg_ref, cw_ref, wo_ref, h_ref, out_ref, scr):
    i = pl.program_id(0)
    tm = u_ref.shape[0]
    first = (i % tiles_per_seq) == 0
    last = (i % tiles_per_seq) == tiles_per_seq - 1
    hb = up_ref.shape[0]
    scr[0:hb, :] = jnp.where(first, 0.0, up_ref[...].astype(F32))
    scr[hb:hb + tm, :] = u_ref[...].astype(F32)
    scr[hb + tm:2 * hb + tm, :] = jnp.where(last, 0.0, un_ref[...].astype(F32))
    cw = cw_ref[...]
    y = (scr[hb - 1:hb - 1 + tm, :] * cw[0:1] + scr[hb:hb + tm, :] * cw[1:2]
         + scr[hb + 1:hb + 1 + tm, :] * cw[2:3])
    z = (bg_ref[...].astype(F32) * y).astype(BF16)
    out_ref[...] = h_ref[...] + jnp.dot(z, wo_ref[...], preferred_element_type=F32)


def _mixer_b(h, g1, w_in, conv_w, w_out):
    b, s, d = h.shape
    t = b * s
    tm = 512
    h2 = h.reshape(t, d)
    tok = pl.BlockSpec((tm, d), lambda i: (i, 0))
    bg, u = pl.pallas_call(
        _b_in_kernel,
        grid=(t // tm,),
        in_specs=[tok, _resident((1, d)), _resident((d, 3 * d))],
        out_specs=[tok, tok],
        out_shape=[jax.ShapeDtypeStruct((t, d), BF16)] * 2,
        compiler_params=_cparams(("parallel",)),
        name="b_in",
    )(h2, g1, w_in.astype(BF16))
    hb = BF16_SUBLANES
    per = tm // hb
    out = pl.pallas_call(
        functools.partial(_b_out_kernel, s // tm),
        grid=(t // tm,),
        in_specs=[
            tok,
            pl.BlockSpec((hb, d), lambda i: (jnp.maximum(i * per - 1, 0), 0)),
            pl.BlockSpec((hb, d), lambda i: (jnp.minimum((i + 1) * per, t // hb - 1), 0)),
            tok, _resident((3, d)), _resident((d, d)), tok,
        ],
        out_specs=tok,
        out_shape=jax.ShapeDtypeStruct((t, d), F32),
        scratch_shapes=[pltpu.VMEM((tm + 2 * hb, d), F32)],
        compiler_params=_cparams(("parallel",)),
        name="b_out",
    )(u, u, u, bg, conv_w, w_out.astype(BF16), h2)
    return out.reshape(b, s, d)


def _axial_rope_T(tn, cr, sr, cc, sc):
    q = HEAD_DIM // 4
    a0, b0 = _rot(tn[:, 0:q], tn[:, q:2 * q], cr[None], sr[None])
    a1, b1 = _rot(tn[:, 2 * q:3 * q], tn[:, 3 * q:], cc[None], sc[None])
    return jnp.concatenate([a0, b0, a1, b1], axis=1)


def _c_proj_kernel(x_ref, g_ref, wq_ref, wk_ref, wv_ref, gq_ref, gk_ref, cr_ref, sr_ref, cc_ref, sc_ref,
                   qT_ref, k_ref, vT_ref):
    y = _rms_rows(x_ref[0], g_ref[...]).astype(BF16)
    tabs = (cr_ref[...], sr_ref[...], cc_ref[...], sc_ref[...])
    qT = lax.dot_general(wq_ref[...], y, NT_DIMS, preferred_element_type=F32)
    qn = _axial_rope_T(_head_norm_T(qT, gq_ref[...]), *tabs) * (SCORE_SCALE * LOG2E)
    qT_ref[0] = qn.reshape(qT.shape).astype(BF16)
    kT = lax.dot_general(wk_ref[...], y, NT_DIMS, preferred_element_type=F32)
    kn = _axial_rope_T(_head_norm_T(kT, gk_ref[...]), *tabs)
    k_ref[0] = kn.reshape(kT.shape).T.astype(BF16)
    vT = lax.dot_general(wv_ref[...], y, NT_DIMS, preferred_element_type=F32).astype(BF16)
    vT_ref[0, :, 0] = vT.reshape(vT_ref.shape[1], HEAD_DIM, vT.shape[-1])


def _c_attn_kernel(qT_ref, k_ref, vT_ref, oT_ref, qz_scr, acc_scr, den_scr, s_scr, cm_scr,
                   m_scr, l_scr, sacc_scr):
    grp = qT_ref.shape[1] // HEAD_DIM
    tq = qT_ref.shape[-1]
    nc, tk = vT_ref.shape[2], vT_ref.shape[-1]
    n = grp * tq

    half = lax.broadcasted_iota(jnp.int32, (LANES, tq), 0) // HEAD_DIM
    keep = half == pl.program_id(1) % 2
    for g in range(grp):
        q = qT_ref[0, g * HEAD_DIM:(g + 1) * HEAD_DIM, :]
        q2 = jnp.concatenate([q, q], axis=0)
        qz_scr[:, g * tq:(g + 1) * tq] = jnp.where(keep, q2, jnp.zeros_like(q2))

    def key_chunk(c):
        return k_ref[0, pl.ds(pl.multiple_of(c * tk, tk), tk), :]

    def store_out(o):
        for g in range(grp):
            oT_ref[0, g * HEAD_DIM:(g + 1) * HEAD_DIM, :] = o[:, g * tq:(g + 1) * tq].astype(BF16)

    acc_scr[...] = jnp.zeros_like(acc_scr)
    den_scr[...] = jnp.zeros_like(den_scr)

    def plain_chunk(c):
        sT = jnp.dot(key_chunk(c), qz_scr[...], preferred_element_type=F32)
        p = jnp.exp2(sT)
        ps = jnp.sum(p.reshape(tk // 8, 8, n), axis=0)
        pv = jnp.dot(vT_ref[0, 0, c], p.astype(BF16), preferred_element_type=F32)
        return pv, ps

    def plain_group(cg, carry):
        pv, ps = plain_chunk(PASS1_UNROLL * cg)
        for u in range(1, PASS1_UNROLL):
            pv_u, ps_u = plain_chunk(PASS1_UNROLL * cg + u)
            pv, ps = pv + pv_u, ps + ps_u
        acc_scr[...] += pv
        den_scr[...] += ps
        return carry

    lax.fori_loop(0, nc // PASS1_UNROLL, plain_group, 0)
    den = jnp.sum(den_scr[...], axis=0, keepdims=True)
    in_range = jnp.logical_and(jnp.max(den) <= 2.0 ** SAFE_LOG2_SCORE,
                               jnp.min(den) >= 2.0 ** -SAFE_LOG2_SCORE)

    @pl.when(in_range)
    def _():
        store_out(acc_scr[...] / den)

    @pl.when(jnp.logical_not(in_range))
    def _():
        m_scr[...] = jnp.full_like(m_scr, -jnp.inf)
        l_scr[...] = jnp.zeros_like(l_scr)
        sacc_scr[...] = jnp.zeros_like(sacc_scr)

        def scores(c, slot):
            sT = jnp.dot(key_chunk(c), qz_scr[...], preferred_element_type=F32)
            s_scr[slot] = sT
            cm_scr[slot] = jnp.max(sT, axis=0, keepdims=True)

        def absorb(c, slot):
            m_old = m_scr[...]
            m_new = jnp.maximum(m_old, cm_scr[slot])
            p = jnp.exp2(s_scr[slot] - m_new)
            alpha = jnp.exp2(m_old - m_new)
            l_scr[...] = alpha * l_scr[...] + jnp.sum(p, axis=0, keepdims=True)
            sacc_scr[...] = alpha * sacc_scr[...] + jnp.dot(
                vT_ref[0, 0, c], p.astype(BF16), preferred_element_type=F32)
            m_scr[...] = m_new

        scores(0, 0)

        def pair(c2, carry):
            c = 2 * c2
            scores(c + 1, 1)
            absorb(c, 0)
            scores(c + 2, 0)
            absorb(c + 1, 1)
            return carry

        lax.fori_loop(0, nc // 2 - 1, pair, 0)
        scores(nc - 1, 1)
        absorb(nc - 2, 0)
        absorb(nc - 1, 1)
        store_out(sacc_scr[...] / l_scr[...])


def _c_out_kernel(oT_ref, wo_ref, h_ref, out_ref):
    o = oT_ref[0].astype(F32).T.astype(BF16)
    out_ref[0] = h_ref[0] + jnp.dot(o, wo_ref[...], preferred_element_type=F32)


def _mixer_c(h, g1, w_qkv, q_gain, k_gain, w_o):
    b, s, d = h.shape
    nq = d
    nk = C_KV_HEADS * HEAD_DIM
    tm = 512
    nc = s // tm
    assert nc % 2 == 0 and nc >= 4
    pos = jnp.arange(s)
    cr, sr = _rope_tables_T(pos // GRID_W, HEAD_DIM // 2, C_THETA)
    cc, sc = _rope_tables_T(pos % GRID_W, HEAD_DIM // 2, C_THETA)
    wq = w_qkv[:, :nq].T.astype(BF16)
    wk = w_qkv[:, nq:nq + nk].T.astype(BF16)
    wv = w_qkv[:, nq + nk:].T.astype(BF16)
    gq = jnp.broadcast_to(q_gain[:, None], (HEAD_DIM, tm))
    gk = jnp.broadcast_to(k_gain[:, None], (HEAD_DIM, tm))
    tab = pl.BlockSpec((HEAD_DIM // 4, tm), lambda bi, i: (0, i))
    qT, k, vT = pl.pallas_call(
        _c_proj_kernel,
        grid=(b, nc),
        in_specs=[
            pl.BlockSpec((1, tm, d), lambda bi, i: (bi, i, 0)),
            _resident((1, d)), _resident((nq, d)), _resident((nk, d)), _resident((nk, d)),
            _resident((HEAD_DIM, tm)), _resident((HEAD_DIM, tm)),
            tab, tab, tab, tab,
        ],
        out_specs=[
            pl.BlockSpec((1, nq, tm), lambda bi, i: (bi, 0, i)),
            pl.BlockSpec((1, tm, nk), lambda bi, i: (bi, i, 0)),
            pl.BlockSpec((1, C_KV_HEADS, 1, HEAD_DIM, tm), lambda bi, i: (bi, 0, i, 0, 0)),
        ],
        out_shape=[
            jax.ShapeDtypeStruct((b, nq, s), BF16),
            jax.ShapeDtypeStruct((b, s, nk), BF16),
            jax.ShapeDtypeStruct((b, C_KV_HEADS, nc, HEAD_DIM, tm), BF16),
        ],
        compiler_params=_cparams(("parallel", "parallel")),
        name="c_proj",
    )(h, g1, wq, wk, wv, gq, gk, cr, sr, cc, sc)

    grp = nq // nk
    tq = 256
    gw = grp * HEAD_DIM
    oT = pl.pallas_call(
        _c_attn_kernel,
        grid=(b, C_KV_HEADS, s // tq),
        in_specs=[
            pl.BlockSpec((1, gw, tq), lambda bi, hk, i: (bi, hk, i)),
            pl.BlockSpec((1, s, LANES), lambda bi, hk, i: (bi, 0, hk // 2)),
            pl.BlockSpec((1, 1, nc, HEAD_DIM, tm), lambda bi, hk, i: (bi, hk, 0, 0, 0)),
        ],
        out_specs=pl.BlockSpec((1, gw, tq), lambda bi, hk, i: (bi, hk, i)),
        out_shape=jax.ShapeDtypeStruct((b, nq, s), BF16),
        scratch_shapes=[pltpu.VMEM((LANES, grp * tq), BF16),
                        pltpu.VMEM((HEAD_DIM, grp * tq), F32), pltpu.VMEM((8, grp * tq), F32),
                        pltpu.VMEM((2, tm, grp * tq), F32), pltpu.VMEM((2, 1, grp * tq), F32),
                        pltpu.VMEM((1, grp * tq), F32), pltpu.VMEM((1, grp * tq), F32),
                        pltpu.VMEM((HEAD_DIM, grp * tq), F32)],
        compiler_params=_cparams(("parallel", "parallel", "parallel")),
        name="c_attn",
    )(qT, k, vT)

    return pl.pallas_call(
        _c_out_kernel,
        grid=(b, nc),
        in_specs=[
            pl.BlockSpec((1, d, tm), lambda bi, i: (bi, 0, i)),
            _resident((d, d)),
            pl.BlockSpec((1, tm, d), lambda bi, i: (bi, i, 0)),
        ],
        out_specs=pl.BlockSpec((1, tm, d), lambda bi, i: (bi, i, 0)),
        out_shape=jax.ShapeDtypeStruct((b, s, d), F32),
        compiler_params=_cparams(("parallel", "parallel")),
        name="c_out",
    )(oT, w_o.astype(BF16), h)


def _mlp_kernel(x_ref, g_ref, w1_ref, w2_ref, out_ref):
    x = x_ref[...]
    d = x.shape[-1]
    y = _rms_rows(x, g_ref[...]).astype(BF16)
    acc = x
    for c in range(w1_ref.shape[1] // d):
        cols = slice(c * d, (c + 1) * d)
        a = jnp.maximum(jnp.dot(y, w1_ref[:, cols], preferred_element_type=F32), 0.0)
        acc = acc + jnp.dot((a * a).astype(BF16), w2_ref[cols, :], preferred_element_type=F32)
    out_ref[...] = acc


def _mlp(h, g2, w1, w2):
    b, s, d = h.shape
    t = b * s
    tm = 512
    tok = pl.BlockSpec((tm, d), lambda i: (i, 0))
    out = pl.pallas_call(
        _mlp_kernel,
        grid=(t // tm,),
        in_specs=[tok, _resident((1, d)), _resident(w1.shape), _resident(w2.shape)],
        out_specs=tok,
        out_shape=jax.ShapeDtypeStruct((t, d), F32),
        compiler_params=_cparams(("parallel",)),
        name="mlp",
    )(h.reshape(t, d), g2, w1.astype(BF16), w2.astype(BF16))
    return out.reshape(b, s, d)


def kernel(x, norm1, norm2, a_wqkv, a_q_gain, a_k_gain, a_wo, b_win, b_conv, b_wout,
           c_wqkv, c_q_gain, c_k_gain, c_wo, mlp_w1, mlp_w2):
    h = x
    for i in range(norm1.shape[0]):
        kind, j = i % N_MIXERS, i // N_MIXERS
        g1 = norm1[i][None, :]
        if kind == 0:
            h = _mixer_a(h, g1, a_wqkv[j], a_q_gain[j], a_k_gain[j], a_wo[j])
        elif kind == 1:
            h = _mixer_b(h, g1, b_win[j], b_conv[j], b_wout[j])
        else:
            h = _mixer_c(h, g1, c_wqkv[j], c_q_gain[j], c_k_gain[j], c_wo[j])
        h = _mlp(h, norm2[i][None, :], mlp_w1[i], mlp_w2[i])
    return h
```

```python
import functools
import math

import jax
import jax.numpy as jnp
import numpy as np
from jax import lax
from jax.experimental import pallas as pl
from jax.experimental.pallas import tpu as pltpu

HEAD_DIM = 64
EPS = 1e-6
NEG_INF = -1e30
N_MIXERS = 3
A_WINDOWS = (128, 512, 2048)
A_DILATIONS = (1, 4, 16)
A_RADIUS = 64
ROPE_THETA = 500000.0
ROPE_DIMS = HEAD_DIM // 4
C_KV_HEADS = 4
C_THETA = 10000.0
GRID_W = 64
SCORE_SCALE = HEAD_DIM ** -0.5
LOG2E = math.log2(math.e)

LANES = 128
BF16_SUBLANES = 16
PAIR_ROWS = 2 * HEAD_DIM + BF16_SUBLANES
SAFE_LOG2_SCORE = 64.0
PASS1_KEYS = 512
PASS1_UNROLL = 8
VMEM_LIMIT = 56 * 1024 * 1024

F32 = jnp.float32
BF16 = jnp.bfloat16

NT_DIMS = (((1,), (1,)), ((), ()))


def _cparams(sem):
    return pltpu.CompilerParams(dimension_semantics=sem, vmem_limit_bytes=VMEM_LIMIT)


def _resident(shape):
    nd = len(shape)
    return pl.BlockSpec(shape, lambda *_: (0,) * nd, pipeline_mode=pl.Buffered(1))


def _rms_rows(x, g):
    ms = jnp.mean(x * x, axis=-1, keepdims=True)
    return (x * lax.rsqrt(ms + EPS)) * g


def _head_norm_T(t, gain):
    n = t.shape[-1]
    t3 = t.reshape(t.shape[0] // HEAD_DIM, HEAD_DIM, n)
    ms = jnp.mean(t3 * t3, axis=1, keepdims=True)
    return (t3 * lax.rsqrt(ms + EPS)) * gain[None]


def _rot(a, b, c, s):
    return a * c - b * s, b * c + a * s


def _rope_tables_T(pos, dim, theta):
    inv = theta ** (-jnp.arange(0, dim, 2, dtype=F32) / dim)
    ang = inv[:, None] * pos.astype(F32)[None, :]
    return jnp.cos(ang), jnp.sin(ang)


def _a_proj_kernel(dil, *refs):
    n_slab = len(refs) - 9
    x_slabs = refs[:n_slab]
    g_ref, wt_ref, gq_ref, gk_ref, cos_ref, sin_ref, qT_ref, k_ref, vT_ref = refs[n_slab:]
    d = wt_ref.shape[1]
    tl = cos_ref.shape[-1]
    width = gq_ref.shape[-1]
    half = ROPE_DIMS // 2
    if dil == 1:
        units = [[(0, p0, width)] for p0 in range(0, tl, width)]
    else:
        units = [[(r, 0, tl) for r in range(r0, r0 + width // tl)] for r0 in range(0, dil, width // tl)]

    def prep(unit):
        x = jnp.concatenate(
            [jnp.concatenate([xs[0, pl.ds(r + p0 * dil, cnt, stride=dil), :] for xs in x_slabs], axis=1)
             for r, p0, cnt in unit], axis=0)
        y = _rms_rows(x, g_ref[...]).astype(BF16)
        c = jnp.concatenate([cos_ref[r, :, p0:p0 + cnt] for r, p0, cnt in unit], axis=1)
        s = jnp.concatenate([sin_ref[r, :, p0:p0 + cnt] for r, p0, cnt in unit], axis=1)
        return y, c, s

    def qk_finish(t, gain, scale, c, s):
        tn = _head_norm_T(t, gain)
        ra, rb = _rot(tn[:, :half], tn[:, half:ROPE_DIMS], c[None], s[None])
        out = jnp.concatenate([ra, rb, tn[:, ROPE_DIMS:]], axis=1)
        if scale != 1.0:
            out = out * scale
        return out.reshape(t.shape)

    prepped = prep(units[0])
    for i, unit in enumerate(units):
        y, c, s = prepped
        tq, tk, tv = (lax.dot_general(wt_ref[j * d:(j + 1) * d, :], y, NT_DIMS, preferred_element_type=F32)
                      for j in range(3))
        if i + 1 < len(units):
            prepped = prep(units[i + 1])
        qT = qk_finish(tq, gq_ref[...], SCORE_SCALE * LOG2E, c, s).astype(BF16)
        k = qk_finish(tk, gk_ref[...], 1.0, c, s).T.astype(BF16)
        vT = tv.astype(BF16)
        npair = d // (2 * HEAD_DIM)
        ones_row = (lax.broadcasted_iota(jnp.int32, (npair, BF16_SUBLANES, width), 1) == 0).astype(BF16)
        vT = jnp.concatenate([vT.reshape(npair, 2 * HEAD_DIM, width), ones_row], axis=1)
        vT = vT.reshape(npair * PAIR_ROWS, width)
        col = 0
        for r, p0, cnt in unit:
            k_ref[0, r, p0:p0 + cnt] = k[col:col + cnt]
            for t in range(cnt // LANES):
                cols = slice(col + t * LANES, col + (t + 1) * LANES)
                qT_ref[0, r, p0 // LANES + t] = qT[:, cols]
                vT_ref[0, r, p0 // LANES + t] = vT[:, cols]
            col += cnt


def _a_proj(h, g1, wt, q_gain, k_gain, dil):
    b, s, d = h.shape
    l = s // dil
    width = 2 * LANES
    tl = 4 * LANES if dil == 1 else LANES
    tt = tl * dil
    pos = (jnp.arange(l)[None, :] * dil + jnp.arange(dil)[:, None]).reshape(-1)
    cosT, sinT = _rope_tables_T(pos, ROPE_DIMS, ROPE_THETA)
    cosT = cosT.reshape(-1, dil, l).transpose(1, 0, 2)
    sinT = sinT.reshape(-1, dil, l).transpose(1, 0, 2)
    gq = jnp.broadcast_to(q_gain[:, None], (HEAD_DIM, width))
    gk = jnp.broadcast_to(k_gain[:, None], (HEAD_DIM, width))
    n_slab = d // LANES
    slab = lambda c: pl.BlockSpec((1, tt, LANES), lambda bi, i: (bi, i, c))
    vd = d // (2 * HEAD_DIM) * PAIR_ROWS
    feat = lambda rows: pl.BlockSpec((1, dil, tl // LANES, rows, LANES), lambda bi, i: (bi, 0, i, 0, 0))
    tab = pl.BlockSpec((dil, ROPE_DIMS // 2, tl), lambda bi, i: (0, 0, i))
    feat_shape = lambda rows: jax.ShapeDtypeStruct((b, dil, l // LANES, rows, LANES), BF16)
    return pl.pallas_call(
        functools.partial(_a_proj_kernel, dil),
        grid=(b, s // tt),
        in_specs=[slab(c) for c in range(n_slab)] + [
            _resident((1, d)), _resident((3 * d, d)),
            _resident((HEAD_DIM, width)), _resident((HEAD_DIM, width)),
            tab, tab,
        ],
        out_specs=[feat(d), pl.BlockSpec((1, dil, tl, d), lambda bi, i: (bi, 0, i, 0)), feat(vd)],
        out_shape=[feat_shape(d), jax.ShapeDtypeStruct((b, dil, l, d), BF16), feat_shape(vd)],
        compiler_params=_cparams(("parallel", "parallel")),
        name="a_proj",
    )(*([h] * n_slab), g1, wt, gq, gk, cosT, sinT)


def _a_attn_kernel(qT_ref, kprev_ref, kmain_ref, knext_ref, vprev_ref, vmain_ref, vnext_ref,
                   bias_ref, cap_ref, o_ref, m_ref, l_ref, oT_scr, st_scr):
    i = pl.program_id(1)
    tb = qT_ref.shape[1]
    d = qT_ref.shape[2]
    n_heads = d // HEAD_DIM
    quad = 4 * HEAD_DIM
    win = 2 * LANES
    first_sel = jnp.where(i == 0, 0, 1)
    last_sel = jnp.where(i == pl.num_programs(1) - 1, 2, 1)
    own_head = (lax.broadcasted_iota(jnp.int32, (quad, 4 * LANES), 0) // HEAD_DIM
                == lax.broadcasted_iota(jnp.int32, (quad, 4 * LANES), 1) // LANES)
    zrows = jnp.zeros((A_RADIUS, 4 * LANES), BF16)

    def window(s):
        lo, hi = s * LANES - A_RADIUS, s * LANES - A_RADIUS + win
        if s == 0:
            kw = jnp.concatenate([kprev_ref[0], kmain_ref[0, 0:hi]], axis=0)
            sel = first_sel
        elif s == tb - 1:
            kw = jnp.concatenate([kmain_ref[0, lo:tb * LANES], knext_ref[0]], axis=0)
            sel = last_sel
        else:
            kw = kmain_ref[0, lo:hi]
            sel = 1
        va = vprev_ref[0, 0] if s == 0 else vmain_ref[0, s - 1]
        vc = vnext_ref[0, 0] if s == tb - 1 else vmain_ref[0, s + 1]
        return kw, sel, jnp.concatenate([va, vmain_ref[0, s], vc], axis=1)

    def quad_scores(s, u, kw):
        q4 = qT_ref[0, s, u * quad:(u + 1) * quad, :]
        q4 = jnp.concatenate([q4] * 4, axis=1)
        qz = jnp.where(own_head, q4, jnp.zeros_like(q4))
        return jnp.dot(kw[:, u * quad:(u + 1) * quad], qz, preferred_element_type=F32)

    def pair_values(vcat, pair, p, pr):
        ppad = jnp.concatenate([zrows, p, zrows], axis=0)
        c0 = 2 * pr * LANES
        return jnp.dot(vcat[pair * PAIR_ROWS:(pair + 1) * PAIR_ROWS, :], ppad[:, c0:c0 + 2 * LANES],
                       preferred_element_type=F32)

    def finish(s):
        rows = slice(s * LANES, (s + 1) * LANES)
        o_ref[0, rows, :] = oT_scr[...].T.astype(BF16)
        m_ref[0, rows, :] = st_scr[0].T
        l_ref[0, rows, :] = st_scr[1].T

    def init_stats():
        st_scr[...] = jnp.zeros_like(st_scr)
        st_scr[1, n_heads:, :] = jnp.ones((LANES - n_heads, LANES), F32)

    init_stats()
    dens = []

    def quad_values(s, u, vcat, p):
        for pr in range(2):
            h0 = 4 * u + 2 * pr
            out = pair_values(vcat, h0 // 2, p, pr)
            for hh in range(2):
                l = out[2 * HEAD_DIM:2 * HEAD_DIM + 1, hh * LANES:(hh + 1) * LANES]
                oT_scr[(h0 + hh) * HEAD_DIM:(h0 + hh + 1) * HEAD_DIM, :] = (
                    out[hh * HEAD_DIM:(hh + 1) * HEAD_DIM, hh * LANES:(hh + 1) * LANES] / l)
                st_scr[1, h0 + hh:h0 + hh + 1, :] = l
                dens.append(l)
        if u == n_heads // 4 - 1:
            finish(s)

    pending = None
    for s in range(tb):
        kw, sel, vcat = window(s)
        cap = cap_ref[sel]
        for u in range(n_heads // 4):
            sT = quad_scores(s, u, kw)
            if pending is not None:
                quad_values(*pending)
            pending = (s, u, vcat, jnp.minimum(jnp.exp2(sT).astype(BF16), cap))
    quad_values(*pending)
    lmin, lmax = dens[0], dens[0]
    for l in dens[1:]:
        lmin, lmax = jnp.minimum(lmin, l), jnp.maximum(lmax, l)
    in_range = jnp.logical_and(jnp.max(lmax) <= 2.0 ** SAFE_LOG2_SCORE,
                               jnp.min(lmin) >= 2.0 ** -SAFE_LOG2_SCORE)

    @pl.when(jnp.logical_not(in_range))
    def _():
        init_stats()
        for s in range(tb):
            kw, sel, vcat = window(s)
            bias = bias_ref[sel]
            for u in range(n_heads // 4):
                sT = quad_scores(s, u, kw) + bias
                m = jnp.max(sT, axis=0, keepdims=True)
                p = jnp.exp2(sT - m)
                l = jnp.sum(p, axis=0, keepdims=True)
                for pr in range(2):
                    h0 = 4 * u + 2 * pr
                    out = pair_values(vcat, h0 // 2, p.astype(BF16), pr)
                    for hh in range(2):
                        cols = slice((2 * pr + hh) * LANES, (2 * pr + hh + 1) * LANES)
                        oT_scr[(h0 + hh) * HEAD_DIM:(h0 + hh + 1) * HEAD_DIM, :] = (
                            out[hh * HEAD_DIM:(hh + 1) * HEAD_DIM, hh * LANES:(hh + 1) * LANES] / l[:, cols])
                for g in range(4):
                    st_scr[0, 4 * u + g:4 * u + g + 1, :] = m[:, g * LANES:(g + 1) * LANES]
                    st_scr[1, 4 * u + g:4 * u + g + 1, :] = l[:, g * LANES:(g + 1) * LANES]
            finish(s)


def _band_masks():
    kk = np.arange(2 * LANES)[:, None]
    j = np.arange(LANES)[None, :]
    band = (kk >= j) & (kk <= j + 2 * A_RADIUS)
    variants = np.stack([band & (kk >= A_RADIUS), band, band & (kk < 2 * LANES - A_RADIUS)])
    variants = np.tile(variants, (1, 1, 4))
    bias = jnp.asarray(np.where(variants, 0.0, NEG_INF), F32)
    cap = jnp.asarray(np.where(variants, float(jnp.finfo(BF16).max), 0.0), BF16)
    return bias, cap


def _a_attn(qT, k, vT):
    b, dil, nt, d, _ = qT.shape
    vd = vT.shape[3]
    n, l = b * dil, nt * LANES
    qT, k, vT = qT.reshape(n, nt, d, LANES), k.reshape(n, l, d), vT.reshape(n, nt, vd, LANES)
    tb = min(4, nt // 2)
    assert tb >= 2 and nt % tb == 0
    hb = A_RADIUS
    per = tb * LANES // hb
    tok = lambda width: pl.BlockSpec((1, tb * LANES, width), lambda ni, i: (ni, i, 0))
    feat = lambda rows: pl.BlockSpec((1, tb, rows, LANES), lambda ni, i: (ni, i, 0, 0))
    o, m, lsum = pl.pallas_call(
        _a_attn_kernel,
        grid=(n, nt // tb),
        in_specs=[
            feat(d),
            pl.BlockSpec((1, hb, d), lambda ni, i: (ni, jnp.maximum(i * per - 1, 0), 0)),
            tok(d),
            pl.BlockSpec((1, hb, d), lambda ni, i: (ni, jnp.minimum((i + 1) * per, l // hb - 1), 0)),
            pl.BlockSpec((1, 1, vd, LANES), lambda ni, i: (ni, jnp.maximum(i * tb - 1, 0), 0, 0)),
            feat(vd),
            pl.BlockSpec((1, 1, vd, LANES), lambda ni, i: (ni, jnp.minimum((i + 1) * tb, nt - 1), 0, 0)),
            _resident((3, 2 * LANES, 4 * LANES)),
            _resident((3, 2 * LANES, 4 * LANES)),
        ],
        out_specs=[tok(d), tok(LANES), tok(LANES)],
        out_shape=[
            jax.ShapeDtypeStruct((n, l, d), BF16),
            jax.ShapeDtypeStruct((n, l, LANES), F32),
            jax.ShapeDtypeStruct((n, l, LANES), F32),
        ],
        scratch_shapes=[pltpu.VMEM((d, LANES), F32), pltpu.VMEM((2, LANES, LANES), F32)],
        compiler_params=_cparams(("parallel", "parallel")),
        name="a_attn",
    )(qT, k, k, k, vT, vT, vT, *_band_masks())
    return (o.reshape(b, dil, l, d), m.reshape(b, dil, l, LANES), lsum.reshape(b, dil, l, LANES))


def _a_out_kernel(n_heads, o0, o1, o2, m0, m1, m2, l0, l1, l2, e_ref, wo_ref, h_ref, out_ref,
                  o_scr, st_scr):
    tm = h_ref.shape[1]

    def stat_tokens(ref, slot):
        dil = ref.shape[1]
        if dil == 1:
            return ref[0, 0]
        for r in range(dil):
            st_scr[slot, pl.ds(r, tm // dil, stride=dil), :] = ref[0, r]
        return st_scr[slot]

    def out_tokens(ref):
        dil = ref.shape[1]
        if dil == 1:
            return ref[0, 0].astype(F32)
        for r in range(dil):
            blk = ref[0, r].astype(F32)
            for c in range(o_scr.shape[0]):
                o_scr[c, pl.ds(r, tm // dil, stride=dil), :] = blk[:, c * LANES:(c + 1) * LANES]
        return jnp.concatenate([o_scr[c] for c in range(o_scr.shape[0])], axis=1)

    ms = [stat_tokens(r, j) for j, r in enumerate((m0, m1, m2))]
    ls = [stat_tokens(r, 3 + j) for j, r in enumerate((l0, l1, l2))]
    mx = jnp.maximum(jnp.maximum(ms[0], ms[1]), ms[2])
    ws = [jnp.exp2(m - mx) * l for m, l in zip(ms, ls)]
    den = ws[0] + ws[1] + ws[2]
    head_lane = lax.broadcasted_iota(jnp.int32, den.shape, 1) < n_heads

    o = None
    for w, o_ref in zip(ws, (o0, o1, o2)):
        a = jnp.where(head_lane, w / den, 0.0)
        hi = a.astype(BF16).astype(F32)
        mid = (a - hi).astype(BF16).astype(F32)
        lo = (a - hi - mid).astype(BF16).astype(F32)
        packed = hi + pltpu.roll(mid, n_heads, axis=1) + pltpu.roll(lo, 2 * n_heads, axis=1)
        coef = jnp.dot(packed.astype(BF16), e_ref[...], preferred_element_type=F32)
        term = coef * out_tokens(o_ref)
        o = term if o is None else o + term
    out_ref[0] = h_ref[0] + jnp.dot(o.astype(BF16), wo_ref[...], preferred_element_type=F32)


def _head_indicator(d):
    n_heads = d // HEAD_DIM
    e = np.zeros((LANES, d), np.float32)
    for part in range(3):
        for h in range(n_heads):
            e[part * n_heads + h, h * HEAD_DIM:(h + 1) * HEAD_DIM] = 1.0
    return jnp.asarray(e, BF16)


def _a_out(parts, wo, h, tm):
    b, s, d = h.shape
    blk = lambda dil, width: pl.BlockSpec((1, dil, tm // dil, width), lambda bi, i: (bi, 0, i, 0))
    os_, ms_, ls_ = zip(*parts)
    dils = [o.shape[1] for o in os_]
    tok = pl.BlockSpec((1, tm, d), lambda bi, i: (bi, i, 0))
    return pl.pallas_call(
        functools.partial(_a_out_kernel, d // HEAD_DIM),
        grid=(b, s // tm),
        in_specs=([blk(dil, d) for dil in dils] + [blk(dil, LANES) for dil in dils] * 2
                  + [_resident((LANES, d)), _resident((d, d)), tok]),
        out_specs=tok,
        out_shape=jax.ShapeDtypeStruct((b, s, d), F32),
        scratch_shapes=[pltpu.VMEM((d // LANES, tm, LANES), F32), pltpu.VMEM((6, tm, LANES), F32)],
        compiler_params=_cparams(("parallel", "parallel")),
        name="a_out",
    )(*os_, *ms_, *ls_, _head_indicator(d), wo, h)


def _mixer_a(h, g1, w_qkv, q_gain, k_gain, w_o):
    d = h.shape[-1]
    parts = []
    for g, dil in enumerate(A_DILATIONS):
        assert A_WINDOWS[g] // (2 * dil) == A_RADIUS
        wt = w_qkv[:, g * 3 * d:(g + 1) * 3 * d].T.astype(BF16)
        parts.append(_a_attn(*_a_proj(h, g1, wt, q_gain[g], k_gain[g], dil)))
    return _a_out(parts, w_o.astype(BF16), h, 512)


def _b_in_kernel(x_ref, g_ref, w_ref, bg_ref, u_ref):
    d = x_ref.shape[-1]
    y = _rms_rows(x_ref[...], g_ref[...]).astype(BF16)
    bg_ref[...] = jnp.dot(y, w_ref[:, 0:d], preferred_element_type=F32).astype(BF16)
    cg = jnp.dot(y, w_ref[:, d:2 * d], preferred_element_type=F32)
    xt = jnp.dot(y, w_ref[:, 2 * d:3 * d], preferred_element_type=F32)
    u_ref[...] = (cg * xt).astype(BF16)


def _b_out_kernel(tiles_per_seq, u_ref, up_ref, un_ref, bg_ref, cw_ref, wo_ref, h_ref, out_ref, scr):
    i = pl.program_id(0)
    tm = u_ref.shape[0]
    first = (i % tiles_per_seq) == 0
    last = (i % tiles_per_seq) == tiles_per_seq - 1
    hb = up_ref.shape[0]
    scr[0:hb, :] = jnp.where(first, 0.0, up_ref[...].astype(F32))
    scr[hb:hb + tm, :] = u_ref[...].astype(F32)
    scr[hb + tm:2 * hb + tm, :] = jnp.where(last, 0.0, un_ref[...].astype(F32))
    cw = cw_ref[...]
    y = (scr[hb - 1:hb - 1 + tm, :] * cw[0:1] + scr[hb:hb + tm, :] * cw[1:2]
         + scr[hb + 1:hb + 1 + tm, :] * cw[2:3])
    z = (bg_ref[...].astype(F32) * y).astype(BF16)
    out_ref[...] = h_ref[...] + jnp.dot(z, wo_ref[...], preferred_element_type=F32)


def _mixer_b(h, g1, w_in, conv_w, w_out):
    b, s, d = h.shape
    t = b * s
    tm = 512
    h2 = h.reshape(t, d)
    tok = pl.BlockSpec((tm, d), lambda i: (i, 0))
    bg, u = pl.pallas_call(
        _b_in_kernel,
        grid=(t // tm,),
        in_specs=[tok, _resident((1, d)), _resident((d, 3 * d))],
        out_specs=[tok, tok],
        out_shape=[jax.ShapeDtypeStruct((t, d), BF16)] * 2,
        compiler_params=_cparams(("parallel",)),
        name="b_in",
    )(h2, g1, w_in.astype(BF16))
    hb = BF16_SUBLANES
    per = tm // hb
    out = pl.pallas_call(
        functools.partial(_b_out_kernel, s // tm),
        grid=(t // tm,),
        in_specs=[
            tok,
            pl.BlockSpec((hb, d), lambda i: (jnp.maximum(i * per - 1, 0), 0)),
            pl.BlockSpec((hb, d), lambda i: (jnp.minimum((i + 1) * per, t // hb - 1), 0)),
            tok, _resident((3, d)), _resident((d, d)), tok,
        ],
        out_specs=tok,
        out_shape=jax.ShapeDtypeStruct((t, d), F32),
        scratch_shapes=[pltpu.VMEM((tm + 2 * hb, d), F32)],
        compiler_params=_cparams(("parallel",)),
        name="b_out",
    )(u, u, u, bg, conv_w, w_out.astype(BF16), h2)
    return out.reshape(b, s, d)


def _axial_rope_T(tn, cr, sr, cc, sc):
    q = HEAD_DIM // 4
    a0, b0 = _rot(tn[:, 0:q], tn[:, q:2 * q], cr[None], sr[None])
    a1, b1 = _rot(tn[:, 2 * q:3 * q], tn[:, 3 * q:], cc[None], sc[None])
    return jnp.concatenate([a0, b0, a1, b1], axis=1)


def _c_proj_kernel(x_ref, g_ref, wq_ref, wk_ref, wv_ref, gq_ref, gk_ref, cr_ref, sr_ref, cc_ref, sc_ref,
                   qT_ref, k_ref, vT_ref):
    y = _rms_rows(x_ref[0], g_ref[...]).astype(BF16)
    tabs = (cr_ref[...], sr_ref[...], cc_ref[...], sc_ref[...])
    qT = lax.dot_general(wq_ref[...], y, NT_DIMS, preferred_element_type=F32)
    qn = _axial_rope_T(_head_norm_T(qT, gq_ref[...]), *tabs) * (SCORE_SCALE * LOG2E)
    qT_ref[0] = qn.reshape(qT.shape).astype(BF16)
    kT = lax.dot_general(wk_ref[...], y, NT_DIMS, preferred_element_type=F32)
    kn = _axial_rope_T(_head_norm_T(kT, gk_ref[...]), *tabs)
    k_ref[0] = kn.reshape(kT.shape).T.astype(BF16)
    vT = lax.dot_general(wv_ref[...], y, NT_DIMS, preferred_element_type=F32).astype(BF16)
    vT_ref[0, :, 0] = vT.reshape(vT_ref.shape[1], HEAD_DIM, vT.shape[-1])


def _c_attn_kernel(qT_ref, k_ref, vT_ref, oT_ref, qz_scr, acc_scr, den_scr, s_scr, cm_scr,
                   m_scr, l_scr, sacc_scr):
    grp = qT_ref.shape[1] // HEAD_DIM
    tq = qT_ref.shape[-1]
    nc, tk = vT_ref.shape[2], vT_ref.shape[-1]
    n = grp * tq

    half = lax.broadcasted_iota(jnp.int32, (LANES, tq), 0) // HEAD_DIM
    keep = half == pl.program_id(1) % 2
    for g in range(grp):
        q = qT_ref[0, g * HEAD_DIM:(g + 1) * HEAD_DIM, :]
        q2 = jnp.concatenate([q, q], axis=0)
        qz_scr[:, g * tq:(g + 1) * tq] = jnp.where(keep, q2, jnp.zeros_like(q2))

    def key_chunk(c):
        return k_ref[0, pl.ds(pl.multiple_of(c * tk, tk), tk), :]

    def store_out(o):
        for g in range(grp):
            oT_ref[0, g * HEAD_DIM:(g + 1) * HEAD_DIM, :] = o[:, g * tq:(g + 1) * tq].astype(BF16)

    acc_scr[...] = jnp.zeros_like(acc_scr)
    den_scr[...] = jnp.zeros_like(den_scr)

    span = PASS1_KEYS // tk

    def span_scores(c0):
        k = k_ref[0, pl.ds(pl.multiple_of(c0 * tk, PASS1_KEYS), PASS1_KEYS), :]
        return jnp.dot(k, qz_scr[...], preferred_element_type=F32)

    def span_values(c0, pb, pv):
        for j in range(span):
            pv_j = jnp.dot(vT_ref[0, 0, c0 + j], pb[j * tk:(j + 1) * tk], preferred_element_type=F32)
            pv = pv_j if pv is None else pv + pv_j
        return pv

    def plain_group(cg, carry):
        pv = ps = pending = None
        for u in range(PASS1_UNROLL):
            c0 = (PASS1_UNROLL * cg + u) * span
            sT = span_scores(c0)
            if pending is not None:
                pv = span_values(*pending, pv)
            p = jnp.exp2(sT)
            ps_u = jnp.sum(p.reshape(PASS1_KEYS // 8, 8, n), axis=0)
            ps = ps_u if ps is None else ps + ps_u
            pending = (c0, p.astype(BF16))
        acc_scr[...] += span_values(*pending, pv)
        den_scr[...] += ps
        return carry

    lax.fori_loop(0, nc // (PASS1_UNROLL * span), plain_group, 0)
    den = jnp.sum(den_scr[...], axis=0, keepdims=True)
    in_range = jnp.logical_and(jnp.max(den) <= 2.0 ** SAFE_LOG2_SCORE,
                               jnp.min(den) >= 2.0 ** -SAFE_LOG2_SCORE)

    @pl.when(in_range)
    def _():
        store_out(acc_scr[...] / den)

    @pl.when(jnp.logical_not(in_range))
    def _():
        m_scr[...] = jnp.full_like(m_scr, -jnp.inf)
        l_scr[...] = jnp.zeros_like(l_scr)
        sacc_scr[...] = jnp.zeros_like(sacc_scr)

        def scores(c, slot):
            sT = jnp.dot(key_chunk(c), qz_scr[...], preferred_element_type=F32)
            s_scr[slot] = sT
            cm_scr[slot] = jnp.max(sT, axis=0, keepdims=True)

        def absorb(c, slot):
            m_old = m_scr[...]
            m_new = jnp.maximum(m_old, cm_scr[slot])
            p = jnp.exp2(s_scr[slot] - m_new)
            alpha = jnp.exp2(m_old - m_new)
            l_scr[...] = alpha * l_scr[...] + jnp.sum(p, axis=0, keepdims=True)
            sacc_scr[...] = alpha * sacc_scr[...] + jnp.dot(
                vT_ref[0, 0, c], p.astype(BF16), preferred_element_type=F32)
            m_scr[...] = m_new

        scores(0, 0)

        def pair(c2, carry):
            c = 2 * c2
            scores(c + 1, 1)
            absorb(c, 0)
            scores(c + 2, 0)
            absorb(c + 1, 1)
            return carry

        lax.fori_loop(0, nc // 2 - 1, pair, 0)
        scores(nc - 1, 1)
        absorb(nc - 2, 0)
        absorb(nc - 1, 1)
        store_out(sacc_scr[...] / l_scr[...])


def _c_out_kernel(oT_ref, wo_ref, h_ref, out_ref):
    o = oT_ref[0].astype(F32).T.astype(BF16)
    out_ref[0] = h_ref[0] + jnp.dot(o, wo_ref[...], preferred_element_type=F32)


def _mixer_c(h, g1, w_qkv, q_gain, k_gain, w_o):
    b, s, d = h.shape
    nq = d
    nk = C_KV_HEADS * HEAD_DIM
    tm = 512
    nc = s // tm
    assert nc % 2 == 0 and nc >= 4 and (nc * tm) % (PASS1_UNROLL * PASS1_KEYS) == 0
    pos = jnp.arange(s)
    cr, sr = _rope_tables_T(pos // GRID_W, HEAD_DIM // 2, C_THETA)
    cc, sc = _rope_tables_T(pos % GRID_W, HEAD_DIM // 2, C_THETA)
    wq = w_qkv[:, :nq].T.astype(BF16)
    wk = w_qkv[:, nq:nq + nk].T.astype(BF16)
    wv = w_qkv[:, nq + nk:].T.astype(BF16)
    gq = jnp.broadcast_to(q_gain[:, None], (HEAD_DIM, tm))
    gk = jnp.broadcast_to(k_gain[:, None], (HEAD_DIM, tm))
    tab = pl.BlockSpec((HEAD_DIM // 4, tm), lambda bi, i: (0, i))
    qT, k, vT = pl.pallas_call(
        _c_proj_kernel,
        grid=(b, nc),
        in_specs=[
            pl.BlockSpec((1, tm, d), lambda bi, i: (bi, i, 0)),
            _resident((1, d)), _resident((nq, d)), _resident((nk, d)), _resident((nk, d)),
            _resident((HEAD_DIM, tm)), _resident((HEAD_DIM, tm)),
            tab, tab, tab, tab,
        ],
        out_specs=[
            pl.BlockSpec((1, nq, tm), lambda bi, i: (bi, 0, i)),
            pl.BlockSpec((1, tm, nk), lambda bi, i: (bi, i, 0)),
            pl.BlockSpec((1, C_KV_HEADS, 1, HEAD_DIM, tm), lambda bi, i: (bi, 0, i, 0, 0)),
        ],
        out_shape=[
            jax.ShapeDtypeStruct((b, nq, s), BF16),
            jax.ShapeDtypeStruct((b, s, nk), BF16),
            jax.ShapeDtypeStruct((b, C_KV_HEADS, nc, HEAD_DIM, tm), BF16),
        ],
        compiler_params=_cparams(("parallel", "parallel")),
        name="c_proj",
    )(h, g1, wq, wk, wv, gq, gk, cr, sr, cc, sc)

    grp = nq // nk
    tq = 256
    gw = grp * HEAD_DIM
    oT = pl.pallas_call(
        _c_attn_kernel,
        grid=(b, C_KV_HEADS, s // tq),
        in_specs=[
            pl.BlockSpec((1, gw, tq), lambda bi, hk, i: (bi, hk, i)),
            pl.BlockSpec((1, s, LANES), lambda bi, hk, i: (bi, 0, hk // 2)),
            pl.BlockSpec((1, 1, nc, HEAD_DIM, tm), lambda bi, hk, i: (bi, hk, 0, 0, 0)),
        ],
        out_specs=pl.BlockSpec((1, gw, tq), lambda bi, hk, i: (bi, hk, i)),
        out_shape=jax.ShapeDtypeStruct((b, nq, s), BF16),
        scratch_shapes=[pltpu.VMEM((LANES, grp * tq), BF16),
                        pltpu.VMEM((HEAD_DIM, grp * tq), F32), pltpu.VMEM((8, grp * tq), F32),
                        pltpu.VMEM((2, tm, grp * tq), F32), pltpu.VMEM((2, 1, grp * tq), F32),
                        pltpu.VMEM((1, grp * tq), F32), pltpu.VMEM((1, grp * tq), F32),
                        pltpu.VMEM((HEAD_DIM, grp * tq), F32)],
        compiler_params=_cparams(("parallel", "parallel", "parallel")),
        name="c_attn",
    )(qT, k, vT)

    return pl.pallas_call(
        _c_out_kernel,
        grid=(b, nc),
        in_specs=[
            pl.BlockSpec((1, d, tm), lambda bi, i: (bi, 0, i)),
            _resident((d, d)),
            pl.BlockSpec((1, tm, d), lambda bi, i: (bi, i, 0)),
        ],
        out_specs=pl.BlockSpec((1, tm, d), lambda bi, i: (bi, i, 0)),
        out_shape=jax.ShapeDtypeStruct((b, s, d), F32),
        compiler_params=_cparams(("parallel", "parallel")),
        name="c_out",
    )(oT, w_o.astype(BF16), h)


def _mlp_kernel(x_ref, g_ref, w1_ref, w2_ref, out_ref):
    x = x_ref[...]
    d = x.shape[-1]
    y = _rms_rows(x, g_ref[...]).astype(BF16)
    acc = x
    for c in range(w1_ref.shape[1] // d):
        cols = slice(c * d, (c + 1) * d)
        a = jnp.maximum(jnp.dot(y, w1_ref[:, cols], preferred_element_type=F32), 0.0)
        acc = acc + jnp.dot((a * a).astype(BF16), w2_ref[cols, :], preferred_element_type=F32)
    out_ref[...] = acc


def _mlp(h, g2, w1, w2):
    b, s, d = h.shape
    t = b * s
    tm = 512
    tok = pl.BlockSpec((tm, d), lambda i: (i, 0))
    out = pl.pallas_call(
        _mlp_kernel,
        grid=(t // tm,),
        in_specs=[tok, _resident((1, d)), _resident(w1.shape), _resident(w2.shape)],
        out_specs=tok,
        out_shape=jax.ShapeDtypeStruct((t, d), F32),
        compiler_params=_cparams(("parallel",)),
        name="mlp",
    )(h.reshape(t, d), g2, w1.astype(BF16), w2.astype(BF16))
    return out.reshape(b, s, d)


def kernel(x, norm1, norm2, a_wqkv, a_q_gain, a_k_gain, a_wo, b_win, b_conv, b_wout,
           c_wqkv, c_q_gain, c_k_gain, c_wo, mlp_w1, mlp_w2):
    h = x
    for i in range(norm1.shape[0]):
        kind, j = i % N_MIXERS, i // N_MIXERS
        g1 = norm1[i][None, :]
        if kind == 0:
            h = _mixer_a(h, g1, a_wqkv[j], a_q_gain[j], a_k_gain[j], a_wo[j])
        elif kind == 1:
            h = _mixer_b(h, g1, b_win[j], b_conv[j], b_wout[j])
        else:
            h = _mixer_c(h, g1, c_wqkv[j], c_q_gain[j], c_k_gain[j], c_wo[j])
        h = _mlp(h, norm2[i][None, :], mlp_w1[i], mlp_w2[i])
    return h
```

```python
import functools
import math

import jax
import jax.numpy as jnp
import numpy as np
from jax import lax
from jax.experimental import pallas as pl
from jax.experimental.pallas import tpu as pltpu

HEAD_DIM = 64
EPS = 1e-6
NEG_INF = -1e30
N_MIXERS = 3
A_WINDOWS = (128, 512, 2048)
A_DILATIONS = (1, 4, 16)
A_RADIUS = 64
ROPE_THETA = 500000.0
ROPE_DIMS = HEAD_DIM // 4
C_KV_HEADS = 4
C_THETA = 10000.0
GRID_W = 64
SCORE_SCALE = HEAD_DIM ** -0.5
LOG2E = math.log2(math.e)

LANES = 128
BF16_SUBLANES = 16
PAIR_ROWS = 2 * HEAD_DIM + BF16_SUBLANES
SAFE_LOG2_SCORE = 64.0
PASS1_KEYS = 512
PASS1_UNROLL = 8
PASS1_QUERIES = 512
VMEM_LIMIT = 56 * 1024 * 1024

F32 = jnp.float32
BF16 = jnp.bfloat16

NT_DIMS = (((1,), (1,)), ((), ()))


def _cparams(sem):
    return pltpu.CompilerParams(dimension_semantics=sem, vmem_limit_bytes=VMEM_LIMIT)


def _resident(shape):
    nd = len(shape)
    return pl.BlockSpec(shape, lambda *_: (0,) * nd, pipeline_mode=pl.Buffered(1))


def _rms_rows(x, g):
    ms = jnp.mean(x * x, axis=-1, keepdims=True)
    return (x * lax.rsqrt(ms + EPS)) * g


def _head_norm_T(t, gain):
    n = t.shape[-1]
    t3 = t.reshape(t.shape[0] // HEAD_DIM, HEAD_DIM, n)
    ms = jnp.mean(t3 * t3, axis=1, keepdims=True)
    return (t3 * lax.rsqrt(ms + EPS)) * gain[None]


def _rot(a, b, c, s):
    return a * c - b * s, b * c + a * s


def _rope_tables_T(pos, dim, theta):
    inv = theta ** (-jnp.arange(0, dim, 2, dtype=F32) / dim)
    ang = inv[:, None] * pos.astype(F32)[None, :]
    return jnp.cos(ang), jnp.sin(ang)


def _a_proj_kernel(dil, *refs):
    n_slab = len(refs) - 9
    x_slabs = refs[:n_slab]
    g_ref, wt_ref, gq_ref, gk_ref, cos_ref, sin_ref, qT_ref, k_ref, vT_ref = refs[n_slab:]
    d = wt_ref.shape[1]
    tl = cos_ref.shape[-1]
    width = gq_ref.shape[-1]
    half = ROPE_DIMS // 2
    if dil == 1:
        units = [[(0, p0, width)] for p0 in range(0, tl, width)]
    else:
        units = [[(r, 0, tl) for r in range(r0, r0 + width // tl)] for r0 in range(0, dil, width // tl)]

    def prep(unit):
        x = jnp.concatenate(
            [jnp.concatenate([xs[0, pl.ds(r + p0 * dil, cnt, stride=dil), :] for xs in x_slabs], axis=1)
             for r, p0, cnt in unit], axis=0)
        y = _rms_rows(x, g_ref[...]).astype(BF16)
        c = jnp.concatenate([cos_ref[r, :, p0:p0 + cnt] for r, p0, cnt in unit], axis=1)
        s = jnp.concatenate([sin_ref[r, :, p0:p0 + cnt] for r, p0, cnt in unit], axis=1)
        return y, c, s

    def qk_finish(t, gain, scale, c, s):
        tn = _head_norm_T(t, gain)
        ra, rb = _rot(tn[:, :half], tn[:, half:ROPE_DIMS], c[None], s[None])
        out = jnp.concatenate([ra, rb, tn[:, ROPE_DIMS:]], axis=1)
        if scale != 1.0:
            out = out * scale
        return out.reshape(t.shape)

    prepped = prep(units[0])
    for i, unit in enumerate(units):
        y, c, s = prepped
        tq, tk, tv = (lax.dot_general(wt_ref[j * d:(j + 1) * d, :], y, NT_DIMS, preferred_element_type=F32)
                      for j in range(3))
        if i + 1 < len(units):
            prepped = prep(units[i + 1])
        qT = qk_finish(tq, gq_ref[...], SCORE_SCALE * LOG2E, c, s).astype(BF16)
        k = qk_finish(tk, gk_ref[...], 1.0, c, s).T.astype(BF16)
        vT = tv.astype(BF16)
        npair = d // (2 * HEAD_DIM)
        ones_row = (lax.broadcasted_iota(jnp.int32, (npair, BF16_SUBLANES, width), 1) == 0).astype(BF16)
        vT = jnp.concatenate([vT.reshape(npair, 2 * HEAD_DIM, width), ones_row], axis=1)
        vT = vT.reshape(npair * PAIR_ROWS, width)
        col = 0
        for r, p0, cnt in unit:
            k_ref[0, r, p0:p0 + cnt] = k[col:col + cnt]
            for t in range(cnt // LANES):
                cols = slice(col + t * LANES, col + (t + 1) * LANES)
                qT_ref[0, r, p0 // LANES + t] = qT[:, cols]
                vT_ref[0, r, p0 // LANES + t] = vT[:, cols]
            col += cnt


def _a_proj(h, g1, wt, q_gain, k_gain, dil):
    b, s, d = h.shape
    l = s // dil
    width = 2 * LANES
    tl = max(LANES, 8 * LANES // dil)
    tt = tl * dil
    pos = (jnp.arange(l)[None, :] * dil + jnp.arange(dil)[:, None]).reshape(-1)
    cosT, sinT = _rope_tables_T(pos, ROPE_DIMS, ROPE_THETA)
    cosT = cosT.reshape(-1, dil, l).transpose(1, 0, 2)
    sinT = sinT.reshape(-1, dil, l).transpose(1, 0, 2)
    gq = jnp.broadcast_to(q_gain[:, None], (HEAD_DIM, width))
    gk = jnp.broadcast_to(k_gain[:, None], (HEAD_DIM, width))
    n_slab = d // LANES
    slab = lambda c: pl.BlockSpec((1, tt, LANES), lambda bi, i: (bi, i, c))
    vd = d // (2 * HEAD_DIM) * PAIR_ROWS
    feat = lambda rows: pl.BlockSpec((1, dil, tl // LANES, rows, LANES), lambda bi, i: (bi, 0, i, 0, 0))
    tab = pl.BlockSpec((dil, ROPE_DIMS // 2, tl), lambda bi, i: (0, 0, i))
    feat_shape = lambda rows: jax.ShapeDtypeStruct((b, dil, l // LANES, rows, LANES), BF16)
    return pl.pallas_call(
        functools.partial(_a_proj_kernel, dil),
        grid=(b, s // tt),
        in_specs=[slab(c) for c in range(n_slab)] + [
            _resident((1, d)), _resident((3 * d, d)),
            _resident((HEAD_DIM, width)), _resident((HEAD_DIM, width)),
            tab, tab,
        ],
        out_specs=[feat(d), pl.BlockSpec((1, dil, tl, d), lambda bi, i: (bi, 0, i, 0)), feat(vd)],
        out_shape=[feat_shape(d), jax.ShapeDtypeStruct((b, dil, l, d), BF16), feat_shape(vd)],
        compiler_params=_cparams(("parallel", "parallel")),
        name="a_proj",
    )(*([h] * n_slab), g1, wt, gq, gk, cosT, sinT)


def _a_attn_kernel(qT_ref, kprev_ref, kmain_ref, knext_ref, vprev_ref, vmain_ref, vnext_ref,
                   bias_ref, cap_ref, o_ref, m_ref, l_ref, oT_scr, st_scr):
    i = pl.program_id(1)
    tb = qT_ref.shape[1]
    d = qT_ref.shape[2]
    n_heads = d // HEAD_DIM
    quad = 4 * HEAD_DIM
    win = 2 * LANES
    first_sel = jnp.where(i == 0, 0, 1)
    last_sel = jnp.where(i == pl.num_programs(1) - 1, 2, 1)
    own_head = (lax.broadcasted_iota(jnp.int32, (quad, 4 * LANES), 0) // HEAD_DIM
                == lax.broadcasted_iota(jnp.int32, (quad, 4 * LANES), 1) // LANES)
    zrows = jnp.zeros((A_RADIUS, 4 * LANES), BF16)

    def window(s):
        lo, hi = s * LANES - A_RADIUS, s * LANES - A_RADIUS + win
        if s == 0:
            kw = jnp.concatenate([kprev_ref[0], kmain_ref[0, 0:hi]], axis=0)
            sel = first_sel
        elif s == tb - 1:
            kw = jnp.concatenate([kmain_ref[0, lo:tb * LANES], knext_ref[0]], axis=0)
            sel = last_sel
        else:
            kw = kmain_ref[0, lo:hi]
            sel = 1
        va = vprev_ref[0, 0] if s == 0 else vmain_ref[0, s - 1]
        vc = vnext_ref[0, 0] if s == tb - 1 else vmain_ref[0, s + 1]
        return kw, sel, jnp.concatenate([va, vmain_ref[0, s], vc], axis=1)

    def quad_scores(s, u, kw):
        q4 = qT_ref[0, s, u * quad:(u + 1) * quad, :]
        q4 = jnp.concatenate([q4] * 4, axis=1)
        qz = jnp.where(own_head, q4, jnp.zeros_like(q4))
        return jnp.dot(kw[:, u * quad:(u + 1) * quad], qz, preferred_element_type=F32)

    def pair_values(vcat, pair, p, pr):
        ppad = jnp.concatenate([zrows, p, zrows], axis=0)
        c0 = 2 * pr * LANES
        return jnp.dot(vcat[pair * PAIR_ROWS:(pair + 1) * PAIR_ROWS, :], ppad[:, c0:c0 + 2 * LANES],
                       preferred_element_type=F32)

    def finish(s):
        rows = slice(s * LANES, (s + 1) * LANES)
        o_ref[0, rows, :] = oT_scr[...].T.astype(BF16)
        m_ref[0, rows, :] = st_scr[0].T
        l_ref[0, rows, :] = st_scr[1].T

    def init_stats():
        st_scr[...] = jnp.zeros_like(st_scr)
        st_scr[1, n_heads:, :] = jnp.ones((LANES - n_heads, LANES), F32)

    init_stats()
    dens = []

    def quad_values(s, u, vcat, p):
        for pr in range(2):
            h0 = 4 * u + 2 * pr
            out = pair_values(vcat, h0 // 2, p, pr)
            for hh in range(2):
                l = out[2 * HEAD_DIM:2 * HEAD_DIM + 1, hh * LANES:(hh + 1) * LANES]
                oT_scr[(h0 + hh) * HEAD_DIM:(h0 + hh + 1) * HEAD_DIM, :] = (
                    out[hh * HEAD_DIM:(hh + 1) * HEAD_DIM, hh * LANES:(hh + 1) * LANES] / l)
                st_scr[1, h0 + hh:h0 + hh + 1, :] = l
                dens.append(l)
        if u == n_heads // 4 - 1:
            finish(s)

    pending = None
    for s in range(tb):
        kw, sel, vcat = window(s)
        cap = cap_ref[sel]
        for u in range(n_heads // 4):
            sT = quad_scores(s, u, kw)
            if pending is not None:
                quad_values(*pending)
            pending = (s, u, vcat, jnp.minimum(jnp.exp2(sT).astype(BF16), cap))
    quad_values(*pending)
    lmin, lmax = dens[0], dens[0]
    for l in dens[1:]:
        lmin, lmax = jnp.minimum(lmin, l), jnp.maximum(lmax, l)
    in_range = jnp.logical_and(jnp.max(lmax) <= 2.0 ** SAFE_LOG2_SCORE,
                               jnp.min(lmin) >= 2.0 ** -SAFE_LOG2_SCORE)

    @pl.when(jnp.logical_not(in_range))
    def _():
        init_stats()
        for s in range(tb):
            kw, sel, vcat = window(s)
            bias = bias_ref[sel]
            for u in range(n_heads // 4):
                sT = quad_scores(s, u, kw) + bias
                m = jnp.max(sT, axis=0, keepdims=True)
                p = jnp.exp2(sT - m)
                l = jnp.sum(p, axis=0, keepdims=True)
                for pr in range(2):
                    h0 = 4 * u + 2 * pr
                    out = pair_values(vcat, h0 // 2, p.astype(BF16), pr)
                    for hh in range(2):
                        cols = slice((2 * pr + hh) * LANES, (2 * pr + hh + 1) * LANES)
                        oT_scr[(h0 + hh) * HEAD_DIM:(h0 + hh + 1) * HEAD_DIM, :] = (
                            out[hh * HEAD_DIM:(hh + 1) * HEAD_DIM, hh * LANES:(hh + 1) * LANES] / l[:, cols])
                for g in range(4):
                    st_scr[0, 4 * u + g:4 * u + g + 1, :] = m[:, g * LANES:(g + 1) * LANES]
                    st_scr[1, 4 * u + g:4 * u + g + 1, :] = l[:, g * LANES:(g + 1) * LANES]
            finish(s)


def _band_masks():
    kk = np.arange(2 * LANES)[:, None]
    j = np.arange(LANES)[None, :]
    band = (kk >= j) & (kk <= j + 2 * A_RADIUS)
    variants = np.stack([band & (kk >= A_RADIUS), band, band & (kk < 2 * LANES - A_RADIUS)])
    variants = np.tile(variants, (1, 1, 4))
    bias = jnp.asarray(np.where(variants, 0.0, NEG_INF), F32)
    cap = jnp.asarray(np.where(variants, float(jnp.finfo(BF16).max), 0.0), BF16)
    return bias, cap


def _a_attn(qT, k, vT):
    b, dil, nt, d, _ = qT.shape
    vd = vT.shape[3]
    n, l = b * dil, nt * LANES
    qT, k, vT = qT.reshape(n, nt, d, LANES), k.reshape(n, l, d), vT.reshape(n, nt, vd, LANES)
    tb = min(4, nt)
    assert tb >= 2 and nt % tb == 0
    hb = A_RADIUS
    per = tb * LANES // hb
    tok = lambda width: pl.BlockSpec((1, tb * LANES, width), lambda ni, i: (ni, i, 0))
    feat = lambda rows: pl.BlockSpec((1, tb, rows, LANES), lambda ni, i: (ni, i, 0, 0))
    o, m, lsum = pl.pallas_call(
        _a_attn_kernel,
        grid=(n, nt // tb),
        in_specs=[
            feat(d),
            pl.BlockSpec((1, hb, d), lambda ni, i: (ni, jnp.maximum(i * per - 1, 0), 0)),
            tok(d),
            pl.BlockSpec((1, hb, d), lambda ni, i: (ni, jnp.minimum((i + 1) * per, l // hb - 1), 0)),
            pl.BlockSpec((1, 1, vd, LANES), lambda ni, i: (ni, jnp.maximum(i * tb - 1, 0), 0, 0)),
            feat(vd),
            pl.BlockSpec((1, 1, vd, LANES), lambda ni, i: (ni, jnp.minimum((i + 1) * tb, nt - 1), 0, 0)),
            _resident((3, 2 * LANES, 4 * LANES)),
            _resident((3, 2 * LANES, 4 * LANES)),
        ],
        out_specs=[tok(d), tok(LANES), tok(LANES)],
        out_shape=[
            jax.ShapeDtypeStruct((n, l, d), BF16),
            jax.ShapeDtypeStruct((n, l, LANES), F32),
            jax.ShapeDtypeStruct((n, l, LANES), F32),
        ],
        scratch_shapes=[pltpu.VMEM((d, LANES), F32), pltpu.VMEM((2, LANES, LANES), F32)],
        compiler_params=_cparams(("parallel", "parallel")),
        name="a_attn",
    )(qT, k, k, k, vT, vT, vT, *_band_masks())
    return (o.reshape(b, dil, l, d), m.reshape(b, dil, l, LANES), lsum.reshape(b, dil, l, LANES))


def _a_out_kernel(n_heads, o0, o1, o2, m0, m1, m2, l0, l1, l2, e_ref, wo_ref, h_ref, out_ref,
                  o_scr, st_scr):
    tm = h_ref.shape[1]

    def stat_tokens(ref, slot):
        dil = ref.shape[1]
        if dil == 1:
            return ref[0, 0]
        for r in range(dil):
            st_scr[slot, pl.ds(r, tm // dil, stride=dil), :] = ref[0, r]
        return st_scr[slot]

    def out_tokens(ref):
        dil = ref.shape[1]
        if dil == 1:
            return ref[0, 0].astype(F32)
        for r in range(dil):
            blk = ref[0, r].astype(F32)
            for c in range(o_scr.shape[0]):
                o_scr[c, pl.ds(r, tm // dil, stride=dil), :] = blk[:, c * LANES:(c + 1) * LANES]
        return jnp.concatenate([o_scr[c] for c in range(o_scr.shape[0])], axis=1)

    ms = [stat_tokens(r, j) for j, r in enumerate((m0, m1, m2))]
    ls = [stat_tokens(r, 3 + j) for j, r in enumerate((l0, l1, l2))]
    mx = jnp.maximum(jnp.maximum(ms[0], ms[1]), ms[2])
    ws = [jnp.exp2(m - mx) * l for m, l in zip(ms, ls)]
    den = ws[0] + ws[1] + ws[2]
    head_lane = lax.broadcasted_iota(jnp.int32, den.shape, 1) < n_heads

    o = None
    for w, o_ref in zip(ws, (o0, o1, o2)):
        a = jnp.where(head_lane, w / den, 0.0)
        hi = a.astype(BF16).astype(F32)
        mid = (a - hi).astype(BF16).astype(F32)
        lo = (a - hi - mid).astype(BF16).astype(F32)
        packed = hi + pltpu.roll(mid, n_heads, axis=1) + pltpu.roll(lo, 2 * n_heads, axis=1)
        coef = jnp.dot(packed.astype(BF16), e_ref[...], preferred_element_type=F32)
        term = coef * out_tokens(o_ref)
        o = term if o is None else o + term
    out_ref[0] = h_ref[0] + jnp.dot(o.astype(BF16), wo_ref[...], preferred_element_type=F32)


def _head_indicator(d):
    n_heads = d // HEAD_DIM
    e = np.zeros((LANES, d), np.float32)
    for part in range(3):
        for h in range(n_heads):
            e[part * n_heads + h, h * HEAD_DIM:(h + 1) * HEAD_DIM] = 1.0
    return jnp.asarray(e, BF16)


def _a_out(parts, wo, h, tm):
    b, s, d = h.shape
    blk = lambda dil, width: pl.BlockSpec((1, dil, tm // dil, width), lambda bi, i: (bi, 0, i, 0))
    os_, ms_, ls_ = zip(*parts)
    dils = [o.shape[1] for o in os_]
    tok = pl.BlockSpec((1, tm, d), lambda bi, i: (bi, i, 0))
    return pl.pallas_call(
        functools.partial(_a_out_kernel, d // HEAD_DIM),
        grid=(b, s // tm),
        in_specs=([blk(dil, d) for dil in dils] + [blk(dil, LANES) for dil in dils] * 2
                  + [_resident((LANES, d)), _resident((d, d)), tok]),
        out_specs=tok,
        out_shape=jax.ShapeDtypeStruct((b, s, d), F32),
        scratch_shapes=[pltpu.VMEM((d // LANES, tm, LANES), F32), pltpu.VMEM((6, tm, LANES), F32)],
        compiler_params=_cparams(("parallel", "parallel")),
        name="a_out",
    )(*os_, *ms_, *ls_, _head_indicator(d), wo, h)


def _mixer_a(h, g1, w_qkv, q_gain, k_gain, w_o):
    d = h.shape[-1]
    parts = []
    for g, dil in enumerate(A_DILATIONS):
        assert A_WINDOWS[g] // (2 * dil) == A_RADIUS
        wt = w_qkv[:, g * 3 * d:(g + 1) * 3 * d].T.astype(BF16)
        parts.append(_a_attn(*_a_proj(h, g1, wt, q_gain[g], k_gain[g], dil)))
    return _a_out(parts, w_o.astype(BF16), h, 512)


def _b_in_kernel(x_ref, g_ref, w_ref, bg_ref, u_ref):
    d = x_ref.shape[-1]
    y = _rms_rows(x_ref[...], g_ref[...]).astype(BF16)
    bg_ref[...] = jnp.dot(y, w_ref[:, 0:d], preferred_element_type=F32).astype(BF16)
    cg = jnp.dot(y, w_ref[:, d:2 * d], preferred_element_type=F32)
    xt = jnp.dot(y, w_ref[:, 2 * d:3 * d], preferred_element_type=F32)
    u_ref[...] = (cg * xt).astype(BF16)


def _b_out_kernel(tiles_per_seq, u_ref, up_ref, un_ref, bg_ref, cw_ref, wo_ref, h_ref, out_ref, scr):
    i = pl.program_id(0)
    tm = u_ref.shape[0]
    first = (i % tiles_per_seq) == 0
    last = (i % tiles_per_seq) == tiles_per_seq - 1
    hb = up_ref.shape[0]
    scr[0:hb, :] = jnp.where(first, 0.0, up_ref[...].astype(F32))
    scr[hb:hb + tm, :] = u_ref[...].astype(F32)
    scr[hb + tm:2 * hb + tm, :] = jnp.where(last, 0.0, un_ref[...].astype(F32))
    cw = cw_ref[...]
    y = (scr[hb - 1:hb - 1 + tm, :] * cw[0:1] + scr[hb:hb + tm, :] * cw[1:2]
         + scr[hb + 1:hb + 1 + tm, :] * cw[2:3])
    z = (bg_ref[...].astype(F32) * y).astype(BF16)
    out_ref[...] = h_ref[...] + jnp.dot(z, wo_ref[...], preferred_element_type=F32)


def _mixer_b(h, g1, w_in, conv_w, w_out):
    b, s, d = h.shape
    t = b * s
    tm = 512
    h2 = h.reshape(t, d)
    tok = pl.BlockSpec((tm, d), lambda i: (i, 0))
    bg, u = pl.pallas_call(
        _b_in_kernel,
        grid=(t // tm,),
        in_specs=[tok, _resident((1, d)), _resident((d, 3 * d))],
        out_specs=[tok, tok],
        out_shape=[jax.ShapeDtypeStruct((t, d), BF16)] * 2,
        compiler_params=_cparams(("parallel",)),
        name="b_in",
    )(h2, g1, w_in.astype(BF16))
    hb = BF16_SUBLANES
    per = tm // hb
    out = pl.pallas_call(
        functools.partial(_b_out_kernel, s // tm),
        grid=(t // tm,),
        in_specs=[
            tok,
            pl.BlockSpec((hb, d), lambda i: (jnp.maximum(i * per - 1, 0), 0)),
            pl.BlockSpec((hb, d), lambda i: (jnp.minimum((i + 1) * per, t // hb - 1), 0)),
            tok, _resident((3, d)), _resident((d, d)), tok,
        ],
        out_specs=tok,
        out_shape=jax.ShapeDtypeStruct((t, d), F32),
        scratch_shapes=[pltpu.VMEM((tm + 2 * hb, d), F32)],
        compiler_params=_cparams(("parallel",)),
        name="b_out",
    )(u, u, u, bg, conv_w, w_out.astype(BF16), h2)
    return out.reshape(b, s, d)


def _axial_rope_T(tn, cr, sr, cc, sc):
    q = HEAD_DIM // 4
    a0, b0 = _rot(tn[:, 0:q], tn[:, q:2 * q], cr[None], sr[None])
    a1, b1 = _rot(tn[:, 2 * q:3 * q], tn[:, 3 * q:], cc[None], sc[None])
    return jnp.concatenate([a0, b0, a1, b1], axis=1)


def _c_proj_kernel(x_ref, g_ref, wq_ref, wk_ref, wv_ref, gq_ref, gk_ref, cr_ref, sr_ref, cc_ref, sc_ref,
                   qT_ref, k_ref, vT_ref):
    y = _rms_rows(x_ref[0], g_ref[...]).astype(BF16)
    tabs = (cr_ref[...], sr_ref[...], cc_ref[...], sc_ref[...])
    qT = lax.dot_general(wq_ref[...], y, NT_DIMS, preferred_element_type=F32)
    qn = _axial_rope_T(_head_norm_T(qT, gq_ref[...]), *tabs) * (SCORE_SCALE * LOG2E)
    qT_ref[0] = qn.reshape(qT.shape).astype(BF16)
    kT = lax.dot_general(wk_ref[...], y, NT_DIMS, preferred_element_type=F32)
    kn = _axial_rope_T(_head_norm_T(kT, gk_ref[...]), *tabs)
    k_ref[0] = kn.reshape(kT.shape).T.astype(BF16)
    vT = lax.dot_general(wv_ref[...], y, NT_DIMS, preferred_element_type=F32).astype(BF16)
    vT_ref[0, :, 0] = vT.reshape(vT_ref.shape[1], HEAD_DIM, vT.shape[-1])


def _c_attn_kernel(qT_ref, k_ref, vT_ref, oT_ref, qz_scr, acc_scr, den_scr, s_scr, cm_scr,
                   m_scr, l_scr, sacc_scr):
    grp = qT_ref.shape[1] // HEAD_DIM
    tq = qT_ref.shape[-1]
    nc, tk = vT_ref.shape[2], vT_ref.shape[-1]
    n = grp * tq

    half = lax.broadcasted_iota(jnp.int32, (LANES, tq), 0) // HEAD_DIM
    keep = half == pl.program_id(1) % 2
    for g in range(grp):
        q = qT_ref[0, g * HEAD_DIM:(g + 1) * HEAD_DIM, :]
        q2 = jnp.concatenate([q, q], axis=0)
        qz_scr[:, g * tq:(g + 1) * tq] = jnp.where(keep, q2, jnp.zeros_like(q2))

    def key_chunk(c):
        return k_ref[0, pl.ds(pl.multiple_of(c * tk, tk), tk), :]

    def store_out(o):
        for g in range(grp):
            oT_ref[0, g * HEAD_DIM:(g + 1) * HEAD_DIM, :] = o[:, g * tq:(g + 1) * tq].astype(BF16)

    acc_scr[...] = jnp.zeros_like(acc_scr)
    den_scr[...] = jnp.zeros_like(den_scr)

    span = PASS1_KEYS // tk
    nq = n // PASS1_QUERIES

    def keys_of(c0):
        return k_ref[0, pl.ds(pl.multiple_of(c0 * tk, PASS1_KEYS), PASS1_KEYS), :]

    def group_values(c0, pb, j):
        pv = None
        for jc in range(span):
            pv_c = jnp.dot(vT_ref[0, 0, c0 + jc], pb[jc * tk:(jc + 1) * tk], preferred_element_type=F32)
            pv = pv_c if pv is None else pv + pv_c
        return pv

    def plain_group(cg, carry):
        pvs = [None] * nq
        pss = [None] * nq
        pending = None
        for u in range(PASS1_UNROLL):
            c0 = (PASS1_UNROLL * cg + u) * span
            k = keys_of(c0)
            new_p = []
            for j in range(nq):
                cols = slice(j * PASS1_QUERIES, (j + 1) * PASS1_QUERIES)
                sT = jnp.dot(k, qz_scr[:, cols], preferred_element_type=F32)
                if pending is not None:
                    pv_j = group_values(pending[0], pending[1][j], j)
                    pvs[j] = pv_j if pvs[j] is None else pvs[j] + pv_j
                p = jnp.exp2(sT)
                ps_j = jnp.sum(p.reshape(PASS1_KEYS // 8, 8, PASS1_QUERIES), axis=0)
                pss[j] = ps_j if pss[j] is None else pss[j] + ps_j
                new_p.append(p.astype(BF16))
            pending = (c0, new_p)
        for j in range(nq):
            cols = slice(j * PASS1_QUERIES, (j + 1) * PASS1_QUERIES)
            acc_scr[:, cols] += pvs[j] + group_values(pending[0], pending[1][j], j)
            den_scr[:, cols] += pss[j]
        return carry

    lax.fori_loop(0, nc // (PASS1_UNROLL * span), plain_group, 0)
    den = jnp.sum(den_scr[...], axis=0, keepdims=True)
    in_range = jnp.logical_and(jnp.max(den) <= 2.0 ** SAFE_LOG2_SCORE,
                               jnp.min(den) >= 2.0 ** -SAFE_LOG2_SCORE)

    @pl.when(in_range)
    def _():
        store_out(acc_scr[...] / den)

    @pl.when(jnp.logical_not(in_range))
    def _():
        m_scr[...] = jnp.full_like(m_scr, -jnp.inf)
        l_scr[...] = jnp.zeros_like(l_scr)
        sacc_scr[...] = jnp.zeros_like(sacc_scr)

        def scores(c, slot):
            sT = jnp.dot(key_chunk(c), qz_scr[...], preferred_element_type=F32)
            s_scr[slot] = sT
            cm_scr[slot] = jnp.max(sT, axis=0, keepdims=True)

        def absorb(c, slot):
            m_old = m_scr[...]
            m_new = jnp.maximum(m_old, cm_scr[slot])
            p = jnp.exp2(s_scr[slot] - m_new)
            alpha = jnp.exp2(m_old - m_new)
            l_scr[...] = alpha * l_scr[...] + jnp.sum(p, axis=0, keepdims=True)
            sacc_scr[...] = alpha * sacc_scr[...] + jnp.dot(
                vT_ref[0, 0, c], p.astype(BF16), preferred_element_type=F32)
            m_scr[...] = m_new

        scores(0, 0)

        def pair(c2, carry):
            c = 2 * c2
            scores(c + 1, 1)
            absorb(c, 0)
            scores(c + 2, 0)
            absorb(c + 1, 1)
            return carry

        lax.fori_loop(0, nc // 2 - 1, pair, 0)
        scores(nc - 1, 1)
        absorb(nc - 2, 0)
        absorb(nc - 1, 1)
        store_out(sacc_scr[...] / l_scr[...])


def _c_out_kernel(oT_ref, wo_ref, h_ref, out_ref):
    o = oT_ref[0].astype(F32).T.astype(BF16)
    out_ref[0] = h_ref[0] + jnp.dot(o, wo_ref[...], preferred_element_type=F32)


def _mixer_c(h, g1, w_qkv, q_gain, k_gain, w_o):
    b, s, d = h.shape
    nq = d
    nk = C_KV_HEADS * HEAD_DIM
    tm = 512
    nc = s // tm
    assert nc % 2 == 0 and nc >= 4 and (nc * tm) % (PASS1_UNROLL * PASS1_KEYS) == 0
    pos = jnp.arange(s)
    cr, sr = _rope_tables_T(pos // GRID_W, HEAD_DIM // 2, C_THETA)
    cc, sc = _rope_tables_T(pos % GRID_W, HEAD_DIM // 2, C_THETA)
    wq = w_qkv[:, :nq].T.astype(BF16)
    wk = w_qkv[:, nq:nq + nk].T.astype(BF16)
    wv = w_qkv[:, nq + nk:].T.astype(BF16)
    gq = jnp.broadcast_to(q_gain[:, None], (HEAD_DIM, tm))
    gk = jnp.broadcast_to(k_gain[:, None], (HEAD_DIM, tm))
    tab = pl.BlockSpec((HEAD_DIM // 4, tm), lambda bi, i: (0, i))
    qT, k, vT = pl.pallas_call(
        _c_proj_kernel,
        grid=(b, nc),
        in_specs=[
            pl.BlockSpec((1, tm, d), lambda bi, i: (bi, i, 0)),
            _resident((1, d)), _resident((nq, d)), _resident((nk, d)), _resident((nk, d)),
            _resident((HEAD_DIM, tm)), _resident((HEAD_DIM, tm)),
            tab, tab, tab, tab,
        ],
        out_specs=[
            pl.BlockSpec((1, nq, tm), lambda bi, i: (bi, 0, i)),
            pl.BlockSpec((1, tm, nk), lambda bi, i: (bi, i, 0)),
            pl.BlockSpec((1, C_KV_HEADS, 1, HEAD_DIM, tm), lambda bi, i: (bi, 0, i, 0, 0)),
        ],
        out_shape=[
            jax.ShapeDtypeStruct((b, nq, s), BF16),
            jax.ShapeDtypeStruct((b, s, nk), BF16),
            jax.ShapeDtypeStruct((b, C_KV_HEADS, nc, HEAD_DIM, tm), BF16),
        ],
        compiler_params=_cparams(("parallel", "parallel")),
        name="c_proj",
    )(h, g1, wq, wk, wv, gq, gk, cr, sr, cc, sc)

    grp = nq // nk
    tq = 512
    gw = grp * HEAD_DIM
    oT = pl.pallas_call(
        _c_attn_kernel,
        grid=(b, C_KV_HEADS, s // tq),
        in_specs=[
            pl.BlockSpec((1, gw, tq), lambda bi, hk, i: (bi, hk, i)),
            pl.BlockSpec((1, s, LANES), lambda bi, hk, i: (bi, 0, hk // 2)),
            pl.BlockSpec((1, 1, nc, HEAD_DIM, tm), lambda bi, hk, i: (bi, hk, 0, 0, 0)),
        ],
        out_specs=pl.BlockSpec((1, gw, tq), lambda bi, hk, i: (bi, hk, i)),
        out_shape=jax.ShapeDtypeStruct((b, nq, s), BF16),
        scratch_shapes=[pltpu.VMEM((LANES, grp * tq), BF16),
                        pltpu.VMEM((HEAD_DIM, grp * tq), F32), pltpu.VMEM((8, grp * tq), F32),
                        pltpu.VMEM((2, tm, grp * tq), F32), pltpu.VMEM((2, 1, grp * tq), F32),
                        pltpu.VMEM((1, grp * tq), F32), pltpu.VMEM((1, grp * tq), F32),
                        pltpu.VMEM((HEAD_DIM, grp * tq), F32)],
        compiler_params=_cparams(("parallel", "parallel", "parallel")),
        name="c_attn",
    )(qT, k, vT)

    return pl.pallas_call(
        _c_out_kernel,
        grid=(b, nc),
        in_specs=[
            pl.BlockSpec((1, d, tm), lambda bi, i: (bi, 0, i)),
            _resident((d, d)),
            pl.BlockSpec((1, tm, d), lambda bi, i: (bi, i, 0)),
        ],
        out_specs=pl.BlockSpec((1, tm, d), lambda bi, i: (bi, i, 0)),
        out_shape=jax.ShapeDtypeStruct((b, s, d), F32),
        compiler_params=_cparams(("parallel", "parallel")),
        name="c_out",
    )(oT, w_o.astype(BF16), h)


def _mlp_kernel(x_ref, g_ref, w1_ref, w2_ref, out_ref):
    x = x_ref[...]
    d = x.shape[-1]
    y = _rms_rows(x, g_ref[...]).astype(BF16)
    acc = x
    for c in range(w1_ref.shape[1] // d):
        cols = slice(c * d, (c + 1) * d)
        a = jnp.maximum(jnp.dot(y, w1_ref[:, cols], preferred_element_type=F32), 0.0)
        acc = acc + jnp.dot((a * a).astype(BF16), w2_ref[cols, :], preferred_element_type=F32)
    out_ref[...] = acc


def _mlp(h, g2, w1, w2):
    b, s, d = h.shape
    t = b * s
    tm = 512
    tok = pl.BlockSpec((tm, d), lambda i: (i, 0))
    out = pl.pallas_call(
        _mlp_kernel,
        grid=(t // tm,),
        in_specs=[tok, _resident((1, d)), _resident(w1.shape), _resident(w2.shape)],
        out_specs=tok,
        out_shape=jax.ShapeDtypeStruct((t, d), F32),
        compiler_params=_cparams(("parallel",)),
        name="mlp",
    )(h.reshape(t, d), g2, w1.astype(BF16), w2.astype(BF16))
    return out.reshape(b, s, d)


def kernel(x, norm1, norm2, a_wqkv, a_q_gain, a_k_gain, a_wo, b_win, b_conv, b_wout,
           c_wqkv, c_q_gain, c_k_gain, c_wo, mlp_w1, mlp_w2):
    h = x
    for i in range(norm1.shape[0]):
        kind, j = i % N_MIXERS, i // N_MIXERS
        g1 = norm1[i][None, :]
        if kind == 0:
            h = _mixer_a(h, g1, a_wqkv[j], a_q_gain[j], a_k_gain[j], a_wo[j])
        elif kind == 1:
            h = _mixer_b(h, g1, b_win[j], b_conv[j], b_wout[j])
        else:
            h = _mixer_c(h, g1, c_wqkv[j], c_q_gain[j], c_k_gain[j], c_wo[j])
        h = _mlp(h, norm2[i][None, :], mlp_w1[i], mlp_w2[i])
    return h
```

```python
import functools
import math

import jax
import jax.numpy as jnp
import numpy as np
from jax import lax
from jax.experimental import pallas as pl
from jax.experimental.pallas import tpu as pltpu

HEAD_DIM = 64
EPS = 1e-6
NEG_INF = -1e30
N_MIXERS = 3
A_WINDOWS = (128, 512, 2048)
A_DILATIONS = (1, 4, 16)
A_RADIUS = 64
ROPE_THETA = 500000.0
ROPE_DIMS = HEAD_DIM // 4
C_KV_HEADS = 4
C_THETA = 10000.0
GRID_W = 64
SCORE_SCALE = HEAD_DIM ** -0.5
LOG2E = math.log2(math.e)

LANES = 128
BF16_SUBLANES = 16
V_ROWS = HEAD_DIM + BF16_SUBLANES
PAIR_ROWS = 2 * HEAD_DIM + BF16_SUBLANES
SAFE_LOG2_SCORE = 64.0
PASS1_KEYS = 512
PASS1_UNROLL = 8
PASS1_QUERIES = 512
VMEM_LIMIT = 56 * 1024 * 1024

F32 = jnp.float32
BF16 = jnp.bfloat16

NT_DIMS = (((1,), (1,)), ((), ()))


def _cparams(sem):
    return pltpu.CompilerParams(dimension_semantics=sem, vmem_limit_bytes=VMEM_LIMIT)


def _resident(shape):
    nd = len(shape)
    return pl.BlockSpec(shape, lambda *_: (0,) * nd, pipeline_mode=pl.Buffered(1))


def _rms_rows(x, g):
    ms = jnp.mean(x * x, axis=-1, keepdims=True)
    return (x * lax.rsqrt(ms + EPS)) * g


def _head_norm_T(t, gain):
    n = t.shape[-1]
    t3 = t.reshape(t.shape[0] // HEAD_DIM, HEAD_DIM, n)
    ms = jnp.mean(t3 * t3, axis=1, keepdims=True)
    return (t3 * lax.rsqrt(ms + EPS)) * gain[None]


def _rot(a, b, c, s):
    return a * c - b * s, b * c + a * s


def _rope_tables_T(pos, dim, theta):
    inv = theta ** (-jnp.arange(0, dim, 2, dtype=F32) / dim)
    ang = inv[:, None] * pos.astype(F32)[None, :]
    return jnp.cos(ang), jnp.sin(ang)


def _a_proj_kernel(dil, *refs):
    n_slab = len(refs) - 9
    x_slabs = refs[:n_slab]
    g_ref, wt_ref, gq_ref, gk_ref, cos_ref, sin_ref, qT_ref, k_ref, vT_ref = refs[n_slab:]
    d = wt_ref.shape[1]
    tl = cos_ref.shape[-1]
    width = gq_ref.shape[-1]
    half = ROPE_DIMS // 2
    if dil == 1:
        units = [[(0, p0, width)] for p0 in range(0, tl, width)]
    else:
        units = [[(r, 0, tl) for r in range(r0, r0 + width // tl)] for r0 in range(0, dil, width // tl)]

    def prep(unit):
        x = jnp.concatenate(
            [jnp.concatenate([xs[0, pl.ds(r + p0 * dil, cnt, stride=dil), :] for xs in x_slabs], axis=1)
             for r, p0, cnt in unit], axis=0)
        y = _rms_rows(x, g_ref[...]).astype(BF16)
        c = jnp.concatenate([cos_ref[r, :, p0:p0 + cnt] for r, p0, cnt in unit], axis=1)
        s = jnp.concatenate([sin_ref[r, :, p0:p0 + cnt] for r, p0, cnt in unit], axis=1)
        return y, c, s

    def qk_finish(t, gain, scale, c, s):
        tn = _head_norm_T(t, gain)
        ra, rb = _rot(tn[:, :half], tn[:, half:ROPE_DIMS], c[None], s[None])
        out = jnp.concatenate([ra, rb, tn[:, ROPE_DIMS:]], axis=1)
        if scale != 1.0:
            out = out * scale
        return out.reshape(t.shape)

    prepped = prep(units[0])
    for i, unit in enumerate(units):
        y, c, s = prepped
        tq, tk, tv = (lax.dot_general(wt_ref[j * d:(j + 1) * d, :], y, NT_DIMS, preferred_element_type=F32)
                      for j in range(3))
        if i + 1 < len(units):
            prepped = prep(units[i + 1])
        qT = qk_finish(tq, gq_ref[...], SCORE_SCALE * LOG2E, c, s).astype(BF16)
        k = qk_finish(tk, gk_ref[...], 1.0, c, s).T.astype(BF16)
        vT = tv.astype(BF16)
        npair = d // (2 * HEAD_DIM)
        ones_row = (lax.broadcasted_iota(jnp.int32, (npair, BF16_SUBLANES, width), 1) == 0).astype(BF16)
        vT = jnp.concatenate([vT.reshape(npair, 2 * HEAD_DIM, width), ones_row], axis=1)
        vT = vT.reshape(npair * PAIR_ROWS, width)
        col = 0
        for r, p0, cnt in unit:
            k_ref[0, r, p0:p0 + cnt] = k[col:col + cnt]
            for t in range(cnt // LANES):
                cols = slice(col + t * LANES, col + (t + 1) * LANES)
                qT_ref[0, r, p0 // LANES + t] = qT[:, cols]
                vT_ref[0, r, p0 // LANES + t] = vT[:, cols]
            col += cnt


def _a_proj(h, g1, wt, q_gain, k_gain, dil):
    b, s, d = h.shape
    l = s // dil
    width = 2 * LANES
    tl = max(LANES, 8 * LANES // dil)
    tt = tl * dil
    pos = (jnp.arange(l)[None, :] * dil + jnp.arange(dil)[:, None]).reshape(-1)
    cosT, sinT = _rope_tables_T(pos, ROPE_DIMS, ROPE_THETA)
    cosT = cosT.reshape(-1, dil, l).transpose(1, 0, 2)
    sinT = sinT.reshape(-1, dil, l).transpose(1, 0, 2)
    gq = jnp.broadcast_to(q_gain[:, None], (HEAD_DIM, width))
    gk = jnp.broadcast_to(k_gain[:, None], (HEAD_DIM, width))
    n_slab = d // LANES
    slab = lambda c: pl.BlockSpec((1, tt, LANES), lambda bi, i: (bi, i, c))
    vd = d // (2 * HEAD_DIM) * PAIR_ROWS
    feat = lambda rows: pl.BlockSpec((1, dil, tl // LANES, rows, LANES), lambda bi, i: (bi, 0, i, 0, 0))
    tab = pl.BlockSpec((dil, ROPE_DIMS // 2, tl), lambda bi, i: (0, 0, i))
    feat_shape = lambda rows: jax.ShapeDtypeStruct((b, dil, l // LANES, rows, LANES), BF16)
    return pl.pallas_call(
        functools.partial(_a_proj_kernel, dil),
        grid=(b, s // tt),
        in_specs=[slab(c) for c in range(n_slab)] + [
            _resident((1, d)), _resident((3 * d, d)),
            _resident((HEAD_DIM, width)), _resident((HEAD_DIM, width)),
            tab, tab,
        ],
        out_specs=[feat(d), pl.BlockSpec((1, dil, tl, d), lambda bi, i: (bi, 0, i, 0)), feat(vd)],
        out_shape=[feat_shape(d), jax.ShapeDtypeStruct((b, dil, l, d), BF16), feat_shape(vd)],
        compiler_params=_cparams(("parallel", "parallel")),
        name="a_proj",
    )(*([h] * n_slab), g1, wt, gq, gk, cosT, sinT)


def _a_attn_kernel(qT_ref, kprev_ref, kmain_ref, knext_ref, vprev_ref, vmain_ref, vnext_ref,
                   bias_ref, cap_ref, o_ref, m_ref, l_ref, oT_scr, st_scr):
    i = pl.program_id(1)
    tb = qT_ref.shape[1]
    d = qT_ref.shape[2]
    n_heads = d // HEAD_DIM
    quad = 4 * HEAD_DIM
    win = 2 * LANES
    first_sel = jnp.where(i == 0, 0, 1)
    last_sel = jnp.where(i == pl.num_programs(1) - 1, 2, 1)
    own_head = (lax.broadcasted_iota(jnp.int32, (quad, 4 * LANES), 0) // HEAD_DIM
                == lax.broadcasted_iota(jnp.int32, (quad, 4 * LANES), 1) // LANES)

    def window(s):
        lo, hi = s * LANES - A_RADIUS, s * LANES - A_RADIUS + win
        if s == 0:
            kw = jnp.concatenate([kprev_ref[0], kmain_ref[0, 0:hi]], axis=0)
            sel = first_sel
        elif s == tb - 1:
            kw = jnp.concatenate([kmain_ref[0, lo:tb * LANES], knext_ref[0]], axis=0)
            sel = last_sel
        else:
            kw = kmain_ref[0, lo:hi]
            sel = 1
        va = vprev_ref[0, 0] if s == 0 else vmain_ref[0, s - 1]
        vc = vnext_ref[0, 0] if s == tb - 1 else vmain_ref[0, s + 1]
        vwin = jnp.concatenate([va[:, LANES - A_RADIUS:], vmain_ref[0, s], vc[:, :A_RADIUS]], axis=1)
        return kw, sel, vwin

    def quad_scores(s, u, kw):
        q4 = qT_ref[0, s, u * quad:(u + 1) * quad, :]
        q4 = jnp.concatenate([q4] * 4, axis=1)
        qz = jnp.where(own_head, q4, jnp.zeros_like(q4))
        return jnp.dot(kw[:, u * quad:(u + 1) * quad], qz, preferred_element_type=F32)

    def pair_values(vcat, pair, p, pr):
        c0 = 2 * pr * LANES
        return jnp.dot(vcat[pair * PAIR_ROWS:(pair + 1) * PAIR_ROWS, :], p[:, c0:c0 + 2 * LANES],
                       preferred_element_type=F32)

    def finish(s):
        rows = slice(s * LANES, (s + 1) * LANES)
        o_ref[0, rows, :] = oT_scr[...].T.astype(BF16)
        m_ref[0, rows, :] = st_scr[0].T
        l_ref[0, rows, :] = st_scr[1].T

    def init_stats():
        st_scr[...] = jnp.zeros_like(st_scr)
        st_scr[1, n_heads:, :] = jnp.ones((LANES - n_heads, LANES), F32)

    init_stats()
    dens = []

    def quad_values(s, u, vcat, p):
        for pr in range(2):
            h0 = 4 * u + 2 * pr
            out = pair_values(vcat, h0 // 2, p, pr)
            for hh in range(2):
                l = out[2 * HEAD_DIM:2 * HEAD_DIM + 1, hh * LANES:(hh + 1) * LANES]
                oT_scr[(h0 + hh) * HEAD_DIM:(h0 + hh + 1) * HEAD_DIM, :] = (
                    out[hh * HEAD_DIM:(hh + 1) * HEAD_DIM, hh * LANES:(hh + 1) * LANES] / l)
                st_scr[1, h0 + hh:h0 + hh + 1, :] = l
                dens.append(l)
        if u == n_heads // 4 - 1:
            finish(s)

    pending = None
    for s in range(tb):
        kw, sel, vcat = window(s)
        cap = cap_ref[sel]
        for u in range(n_heads // 4):
            sT = quad_scores(s, u, kw)
            if pending is not None:
                quad_values(*pending)
            pending = (s, u, vcat, jnp.minimum(jnp.exp2(sT).astype(BF16), cap))
    quad_values(*pending)
    lmin, lmax = dens[0], dens[0]
    for l in dens[1:]:
        lmin, lmax = jnp.minimum(lmin, l), jnp.maximum(lmax, l)
    in_range = jnp.logical_and(jnp.max(lmax) <= 2.0 ** SAFE_LOG2_SCORE,
                               jnp.min(lmin) >= 2.0 ** -SAFE_LOG2_SCORE)

    @pl.when(jnp.logical_not(in_range))
    def _():
        init_stats()
        for s in range(tb):
            kw, sel, vcat = window(s)
            bias = bias_ref[sel]
            for u in range(n_heads // 4):
                sT = quad_scores(s, u, kw) + bias
                m = jnp.max(sT, axis=0, keepdims=True)
                p = jnp.exp2(sT - m)
                l = jnp.sum(p, axis=0, keepdims=True)
                for pr in range(2):
                    h0 = 4 * u + 2 * pr
                    out = pair_values(vcat, h0 // 2, p.astype(BF16), pr)
                    for hh in range(2):
                        cols = slice((2 * pr + hh) * LANES, (2 * pr + hh + 1) * LANES)
                        oT_scr[(h0 + hh) * HEAD_DIM:(h0 + hh + 1) * HEAD_DIM, :] = (
                            out[hh * HEAD_DIM:(hh + 1) * HEAD_DIM, hh * LANES:(hh + 1) * LANES] / l[:, cols])
                for g in range(4):
                    st_scr[0, 4 * u + g:4 * u + g + 1, :] = m[:, g * LANES:(g + 1) * LANES]
                    st_scr[1, 4 * u + g:4 * u + g + 1, :] = l[:, g * LANES:(g + 1) * LANES]
            finish(s)


def _band_masks():
    kk = np.arange(2 * LANES)[:, None]
    j = np.arange(LANES)[None, :]
    band = (kk >= j) & (kk <= j + 2 * A_RADIUS)
    variants = np.stack([band & (kk >= A_RADIUS), band, band & (kk < 2 * LANES - A_RADIUS)])
    variants = np.tile(variants, (1, 1, 4))
    bias = jnp.asarray(np.where(variants, 0.0, NEG_INF), F32)
    cap = jnp.asarray(np.where(variants, float(jnp.finfo(BF16).max), 0.0), BF16)
    return bias, cap


def _a_attn(qT, k, vT):
    b, dil, nt, d, _ = qT.shape
    vd = vT.shape[3]
    n, l = b * dil, nt * LANES
    qT, k, vT = qT.reshape(n, nt, d, LANES), k.reshape(n, l, d), vT.reshape(n, nt, vd, LANES)
    tb = min(4, nt)
    assert tb >= 2 and nt % tb == 0
    hb = A_RADIUS
    per = tb * LANES // hb
    tok = lambda width: pl.BlockSpec((1, tb * LANES, width), lambda ni, i: (ni, i, 0))
    feat = lambda rows: pl.BlockSpec((1, tb, rows, LANES), lambda ni, i: (ni, i, 0, 0))
    o, m, lsum = pl.pallas_call(
        _a_attn_kernel,
        grid=(n, nt // tb),
        in_specs=[
            feat(d),
            pl.BlockSpec((1, hb, d), lambda ni, i: (ni, jnp.maximum(i * per - 1, 0), 0)),
            tok(d),
            pl.BlockSpec((1, hb, d), lambda ni, i: (ni, jnp.minimum((i + 1) * per, l // hb - 1), 0)),
            pl.BlockSpec((1, 1, vd, LANES), lambda ni, i: (ni, jnp.maximum(i * tb - 1, 0), 0, 0)),
            feat(vd),
            pl.BlockSpec((1, 1, vd, LANES), lambda ni, i: (ni, jnp.minimum((i + 1) * tb, nt - 1), 0, 0)),
            _resident((3, 2 * LANES, 4 * LANES)),
            _resident((3, 2 * LANES, 4 * LANES)),
        ],
        out_specs=[tok(d), tok(LANES), tok(LANES)],
        out_shape=[
            jax.ShapeDtypeStruct((n, l, d), BF16),
            jax.ShapeDtypeStruct((n, l, LANES), F32),
            jax.ShapeDtypeStruct((n, l, LANES), F32),
        ],
        scratch_shapes=[pltpu.VMEM((d, LANES), F32), pltpu.VMEM((2, LANES, LANES), F32)],
        compiler_params=_cparams(("parallel", "parallel")),
        name="a_attn",
    )(qT, k, k, k, vT, vT, vT, *_band_masks())
    return (o.reshape(b, dil, l, d), m.reshape(b, dil, l, LANES), lsum.reshape(b, dil, l, LANES))


def _a_out_kernel(n_heads, o0, o1, o2, m0, m1, m2, l0, l1, l2, e_ref, wo_ref, h_ref, out_ref,
                  o_scr, st_scr):
    tm = h_ref.shape[1]

    def stat_tokens(ref, slot):
        dil = ref.shape[1]
        if dil == 1:
            return ref[0, 0]
        for r in range(dil):
            st_scr[slot, pl.ds(r, tm // dil, stride=dil), :] = ref[0, r]
        return st_scr[slot]

    def out_tokens(ref):
        dil = ref.shape[1]
        if dil == 1:
            return ref[0, 0].astype(F32)
        for r in range(dil):
            blk = ref[0, r].astype(F32)
            for c in range(o_scr.shape[0]):
                o_scr[c, pl.ds(r, tm // dil, stride=dil), :] = blk[:, c * LANES:(c + 1) * LANES]
        return jnp.concatenate([o_scr[c] for c in range(o_scr.shape[0])], axis=1)

    ms = [stat_tokens(r, j) for j, r in enumerate((m0, m1, m2))]
    ls = [stat_tokens(r, 3 + j) for j, r in enumerate((l0, l1, l2))]
    mx = jnp.maximum(jnp.maximum(ms[0], ms[1]), ms[2])
    ws = [jnp.exp2(m - mx) * l for m, l in zip(ms, ls)]
    den = ws[0] + ws[1] + ws[2]
    head_lane = lax.broadcasted_iota(jnp.int32, den.shape, 1) < n_heads

    o = None
    for w, o_ref in zip(ws, (o0, o1, o2)):
        a = jnp.where(head_lane, w / den, 0.0)
        hi = a.astype(BF16).astype(F32)
        mid = (a - hi).astype(BF16).astype(F32)
        lo = (a - hi - mid).astype(BF16).astype(F32)
        packed = hi + pltpu.roll(mid, n_heads, axis=1) + pltpu.roll(lo, 2 * n_heads, axis=1)
        coef = jnp.dot(packed.astype(BF16), e_ref[...], preferred_element_type=F32)
        term = coef * out_tokens(o_ref)
        o = term if o is None else o + term
    out_ref[0] = h_ref[0] + jnp.dot(o.astype(BF16), wo_ref[...], preferred_element_type=F32)


def _head_indicator(d):
    n_heads = d // HEAD_DIM
    e = np.zeros((LANES, d), np.float32)
    for part in range(3):
        for h in range(n_heads):
            e[part * n_heads + h, h * HEAD_DIM:(h + 1) * HEAD_DIM] = 1.0
    return jnp.asarray(e, BF16)


def _a_out(parts, wo, h, tm):
    b, s, d = h.shape
    blk = lambda dil, width: pl.BlockSpec((1, dil, tm // dil, width), lambda bi, i: (bi, 0, i, 0))
    os_, ms_, ls_ = zip(*parts)
    dils = [o.shape[1] for o in os_]
    tok = pl.BlockSpec((1, tm, d), lambda bi, i: (bi, i, 0))
    return pl.pallas_call(
        functools.partial(_a_out_kernel, d // HEAD_DIM),
        grid=(b, s // tm),
        in_specs=([blk(dil, d) for dil in dils] + [blk(dil, LANES) for dil in dils] * 2
                  + [_resident((LANES, d)), _resident((d, d)), tok]),
        out_specs=tok,
        out_shape=jax.ShapeDtypeStruct((b, s, d), F32),
        scratch_shapes=[pltpu.VMEM((d // LANES, tm, LANES), F32), pltpu.VMEM((6, tm, LANES), F32)],
        compiler_params=_cparams(("parallel", "parallel")),
        name="a_out",
    )(*os_, *ms_, *ls_, _head_indicator(d), wo, h)


def _mixer_a(h, g1, w_qkv, q_gain, k_gain, w_o):
    d = h.shape[-1]
    parts = []
    for g, dil in enumerate(A_DILATIONS):
        assert A_WINDOWS[g] // (2 * dil) == A_RADIUS
        wt = w_qkv[:, g * 3 * d:(g + 1) * 3 * d].T.astype(BF16)
        parts.append(_a_attn(*_a_proj(h, g1, wt, q_gain[g], k_gain[g], dil)))
    return _a_out(parts, w_o.astype(BF16), h, 512)


def _b_in_kernel(x_ref, g_ref, w_ref, bg_ref, u_ref):
    d = x_ref.shape[-1]
    y = _rms_rows(x_ref[...], g_ref[...]).astype(BF16)
    bg_ref[...] = jnp.dot(y, w_ref[:, 0:d], preferred_element_type=F32).astype(BF16)
    cg = jnp.dot(y, w_ref[:, d:2 * d], preferred_element_type=F32)
    xt = jnp.dot(y, w_ref[:, 2 * d:3 * d], preferred_element_type=F32)
    u_ref[...] = (cg * xt).astype(BF16)


def _b_out_kernel(tiles_per_seq, u_ref, up_ref, un_ref, bg_ref, cw_ref, wo_ref, h_ref, out_ref, scr):
    i = pl.program_id(0)
    tm = u_ref.shape[0]
    first = (i % tiles_per_seq) == 0
    last = (i % tiles_per_seq) == tiles_per_seq - 1
    hb = up_ref.shape[0]
    scr[0:hb, :] = jnp.where(first, 0.0, up_ref[...].astype(F32))
    scr[hb:hb + tm, :] = u_ref[...].astype(F32)
    scr[hb + tm:2 * hb + tm, :] = jnp.where(last, 0.0, un_ref[...].astype(F32))
    cw = cw_ref[...]
    y = (scr[hb - 1:hb - 1 + tm, :] * cw[0:1] + scr[hb:hb + tm, :] * cw[1:2]
         + scr[hb + 1:hb + 1 + tm, :] * cw[2:3])
    z = (bg_ref[...].astype(F32) * y).astype(BF16)
    out_ref[...] = h_ref[...] + jnp.dot(z, wo_ref[...], preferred_element_type=F32)


def _mixer_b(h, g1, w_in, conv_w, w_out):
    b, s, d = h.shape
    t = b * s
    tm = 512
    h2 = h.reshape(t, d)
    tok = pl.BlockSpec((tm, d), lambda i: (i, 0))
    bg, u = pl.pallas_call(
        _b_in_kernel,
        grid=(t // tm,),
        in_specs=[tok, _resident((1, d)), _resident((d, 3 * d))],
        out_specs=[tok, tok],
        out_shape=[jax.ShapeDtypeStruct((t, d), BF16)] * 2,
        compiler_params=_cparams(("parallel",)),
        name="b_in",
    )(h2, g1, w_in.astype(BF16))
    hb = BF16_SUBLANES
    per = tm // hb
    out = pl.pallas_call(
        functools.partial(_b_out_kernel, s // tm),
        grid=(t // tm,),
        in_specs=[
            tok,
            pl.BlockSpec((hb, d), lambda i: (jnp.maximum(i * per - 1, 0), 0)),
            pl.BlockSpec((hb, d), lambda i: (jnp.minimum((i + 1) * per, t // hb - 1), 0)),
            tok, _resident((3, d)), _resident((d, d)), tok,
        ],
        out_specs=tok,
        out_shape=jax.ShapeDtypeStruct((t, d), F32),
        scratch_shapes=[pltpu.VMEM((tm + 2 * hb, d), F32)],
        compiler_params=_cparams(("parallel",)),
        name="b_out",
    )(u, u, u, bg, conv_w, w_out.astype(BF16), h2)
    return out.reshape(b, s, d)


def _axial_rope_T(tn, cr, sr, cc, sc):
    q = HEAD_DIM // 4
    a0, b0 = _rot(tn[:, 0:q], tn[:, q:2 * q], cr[None], sr[None])
    a1, b1 = _rot(tn[:, 2 * q:3 * q], tn[:, 3 * q:], cc[None], sc[None])
    return jnp.concatenate([a0, b0, a1, b1], axis=1)


def _c_proj_kernel(x_ref, g_ref, wq_ref, wk_ref, wv_ref, gq_ref, gk_ref, cr_ref, sr_ref, cc_ref, sc_ref,
                   qT_ref, k_ref, vT_ref):
    y = _rms_rows(x_ref[0], g_ref[...]).astype(BF16)
    tabs = (cr_ref[...], sr_ref[...], cc_ref[...], sc_ref[...])
    qT = lax.dot_general(wq_ref[...], y, NT_DIMS, preferred_element_type=F32)
    qn = _axial_rope_T(_head_norm_T(qT, gq_ref[...]), *tabs) * (SCORE_SCALE * LOG2E)
    qT_ref[0] = qn.reshape(qT.shape).astype(BF16)
    kT = lax.dot_general(wk_ref[...], y, NT_DIMS, preferred_element_type=F32)
    kn = _axial_rope_T(_head_norm_T(kT, gk_ref[...]), *tabs)
    k_ref[0] = kn.reshape(kT.shape).T.astype(BF16)
    vT = lax.dot_general(wv_ref[...], y, NT_DIMS, preferred_element_type=F32).astype(BF16)
    hkv, tm = vT_ref.shape[1], vT.shape[-1]
    ones_row = (lax.broadcasted_iota(jnp.int32, (hkv, BF16_SUBLANES, tm), 1) == 0).astype(BF16)
    vT_ref[0, :, 0] = jnp.concatenate([vT.reshape(hkv, HEAD_DIM, tm), ones_row], axis=1)


def _c_attn_kernel(qT_ref, k_ref, vT_ref, oT_ref, qz_scr, acc_scr, p_scr, s_scr, cm_scr,
                   m_scr, l_scr, sacc_scr):
    grp = qT_ref.shape[1] // HEAD_DIM
    tq = qT_ref.shape[-1]
    nc, tk = vT_ref.shape[2], vT_ref.shape[-1]
    n = grp * tq

    half = lax.broadcasted_iota(jnp.int32, (LANES, tq), 0) // HEAD_DIM
    keep = half == pl.program_id(1) % 2
    for g in range(grp):
        q = qT_ref[0, g * HEAD_DIM:(g + 1) * HEAD_DIM, :]
        q2 = jnp.concatenate([q, q], axis=0)
        qz_scr[:, g * tq:(g + 1) * tq] = jnp.where(keep, q2, jnp.zeros_like(q2))

    def key_chunk(c):
        return k_ref[0, pl.ds(pl.multiple_of(c * tk, tk), tk), :]

    def store_out(o):
        for g in range(grp):
            oT_ref[0, g * HEAD_DIM:(g + 1) * HEAD_DIM, :] = o[:, g * tq:(g + 1) * tq].astype(BF16)

    acc_scr[...] = jnp.zeros_like(acc_scr)

    span = PASS1_KEYS // tk
    nq = n // PASS1_QUERIES

    def keys_of(c0):
        return k_ref[0, pl.ds(pl.multiple_of(c0 * tk, PASS1_KEYS), PASS1_KEYS), :]

    def group_values(c0, slot, j):
        pv = None
        for jc in range(span):
            pv_c = jnp.dot(vT_ref[0, 0, c0 + jc], p_scr[slot, j, jc * tk:(jc + 1) * tk, :],
                           preferred_element_type=F32)
            pv = pv_c if pv is None else pv + pv_c
        return pv

    def plain_group(cg, carry):
        pvs = [None] * nq
        pending = None
        for u in range(PASS1_UNROLL):
            c0 = (PASS1_UNROLL * cg + u) * span
            k = keys_of(c0)
            for j in range(nq):
                cols = slice(j * PASS1_QUERIES, (j + 1) * PASS1_QUERIES)
                sT = jnp.dot(k, qz_scr[:, cols], preferred_element_type=F32)
                if pending is not None:
                    pv_j = group_values(*pending, j)
                    pvs[j] = pv_j if pvs[j] is None else pvs[j] + pv_j
                p_scr[u % 2, j] = jnp.exp2(sT).astype(BF16)
            pending = (c0, u % 2)
        for j in range(nq):
            cols = slice(j * PASS1_QUERIES, (j + 1) * PASS1_QUERIES)
            acc_scr[:, cols] += pvs[j] + group_values(*pending, j)
        return carry

    lax.fori_loop(0, nc // (PASS1_UNROLL * span), plain_group, 0)
    den = acc_scr[HEAD_DIM:HEAD_DIM + 1, :]
    in_range = jnp.logical_and(jnp.max(den) <= 2.0 ** SAFE_LOG2_SCORE,
                               jnp.min(den) >= 2.0 ** -SAFE_LOG2_SCORE)

    @pl.when(in_range)
    def _():
        store_out(acc_scr[:HEAD_DIM, :] / den)

    @pl.when(jnp.logical_not(in_range))
    def _():
        m_scr[...] = jnp.full_like(m_scr, -jnp.inf)
        l_scr[...] = jnp.zeros_like(l_scr)
        sacc_scr[...] = jnp.zeros_like(sacc_scr)

        def scores(c, slot):
            sT = jnp.dot(key_chunk(c), qz_scr[...], preferred_element_type=F32)
            s_scr[slot] = sT
            cm_scr[slot] = jnp.max(sT, axis=0, keepdims=True)

        def absorb(c, slot):
            m_old = m_scr[...]
            m_new = jnp.maximum(m_old, cm_scr[slot])
            p = jnp.exp2(s_scr[slot] - m_new)
            alpha = jnp.exp2(m_old - m_new)
            l_scr[...] = alpha * l_scr[...] + jnp.sum(p, axis=0, keepdims=True)
            sacc_scr[...] = alpha * sacc_scr[...] + jnp.dot(
                vT_ref[0, 0, c, :HEAD_DIM, :], p.astype(BF16), preferred_element_type=F32)
            m_scr[...] = m_new

        scores(0, 0)

        def pair(c2, carry):
            c = 2 * c2
            scores(c + 1, 1)
            absorb(c, 0)
            scores(c + 2, 0)
            absorb(c + 1, 1)
            return carry

        lax.fori_loop(0, nc // 2 - 1, pair, 0)
        scores(nc - 1, 1)
        absorb(nc - 2, 0)
        absorb(nc - 1, 1)
        store_out(sacc_scr[...] / l_scr[...])


def _c_out_kernel(oT_ref, wo_ref, h_ref, out_ref):
    o = oT_ref[0].astype(F32).T.astype(BF16)
    out_ref[0] = h_ref[0] + jnp.dot(o, wo_ref[...], preferred_element_type=F32)


def _mixer_c(h, g1, w_qkv, q_gain, k_gain, w_o):
    b, s, d = h.shape
    nq = d
    nk = C_KV_HEADS * HEAD_DIM
    tm = 512
    nc = s // tm
    assert nc % 2 == 0 and nc >= 4 and (nc * tm) % (PASS1_UNROLL * PASS1_KEYS) == 0
    pos = jnp.arange(s)
    cr, sr = _rope_tables_T(pos // GRID_W, HEAD_DIM // 2, C_THETA)
    cc, sc = _rope_tables_T(pos % GRID_W, HEAD_DIM // 2, C_THETA)
    wq = w_qkv[:, :nq].T.astype(BF16)
    wk = w_qkv[:, nq:nq + nk].T.astype(BF16)
    wv = w_qkv[:, nq + nk:].T.astype(BF16)
    gq = jnp.broadcast_to(q_gain[:, None], (HEAD_DIM, tm))
    gk = jnp.broadcast_to(k_gain[:, None], (HEAD_DIM, tm))
    tab = pl.BlockSpec((HEAD_DIM // 4, tm), lambda bi, i: (0, i))
    qT, k, vT = pl.pallas_call(
        _c_proj_kernel,
        grid=(b, nc),
        in_specs=[
            pl.BlockSpec((1, tm, d), lambda bi, i: (bi, i, 0)),
            _resident((1, d)), _resident((nq, d)), _resident((nk, d)), _resident((nk, d)),
            _resident((HEAD_DIM, tm)), _resident((HEAD_DIM, tm)),
            tab, tab, tab, tab,
        ],
        out_specs=[
            pl.BlockSpec((1, nq, tm), lambda bi, i: (bi, 0, i)),
            pl.BlockSpec((1, tm, nk), lambda bi, i: (bi, i, 0)),
            pl.BlockSpec((1, C_KV_HEADS, 1, V_ROWS, tm), lambda bi, i: (bi, 0, i, 0, 0)),
        ],
        out_shape=[
            jax.ShapeDtypeStruct((b, nq, s), BF16),
            jax.ShapeDtypeStruct((b, s, nk), BF16),
            jax.ShapeDtypeStruct((b, C_KV_HEADS, nc, V_ROWS, tm), BF16),
        ],
        compiler_params=_cparams(("parallel", "parallel")),
        name="c_proj",
    )(h, g1, wq, wk, wv, gq, gk, cr, sr, cc, sc)

    grp = nq // nk
    tq = 512
    gw = grp * HEAD_DIM
    oT = pl.pallas_call(
        _c_attn_kernel,
        grid=(b, C_KV_HEADS, s // tq),
        in_specs=[
            pl.BlockSpec((1, gw, tq), lambda bi, hk, i: (bi, hk, i)),
            pl.BlockSpec((1, s, LANES), lambda bi, hk, i: (bi, 0, hk // 2)),
            pl.BlockSpec((1, 1, nc, V_ROWS, tm), lambda bi, hk, i: (bi, hk, 0, 0, 0)),
        ],
        out_specs=pl.BlockSpec((1, gw, tq), lambda bi, hk, i: (bi, hk, i)),
        out_shape=jax.ShapeDtypeStruct((b, nq, s), BF16),
        scratch_shapes=[pltpu.VMEM((LANES, grp * tq), BF16),
                        pltpu.VMEM((V_ROWS, grp * tq), F32),
                        pltpu.VMEM((2, grp * tq // PASS1_QUERIES, PASS1_KEYS, PASS1_QUERIES), BF16),
                        pltpu.VMEM((2, tm, grp * tq), F32), pltpu.VMEM((2, 1, grp * tq), F32),
                        pltpu.VMEM((1, grp * tq), F32), pltpu.VMEM((1, grp * tq), F32),
                        pltpu.VMEM((HEAD_DIM, grp * tq), F32)],
        compiler_params=_cparams(("parallel", "parallel", "parallel")),
        name="c_attn",
    )(qT, k, vT)

    return pl.pallas_call(
        _c_out_kernel,
        grid=(b, nc),
        in_specs=[
            pl.BlockSpec((1, d, tm), lambda bi, i: (bi, 0, i)),
            _resident((d, d)),
            pl.BlockSpec((1, tm, d), lambda bi, i: (bi, i, 0)),
        ],
        out_specs=pl.BlockSpec((1, tm, d), lambda bi, i: (bi, i, 0)),
        out_shape=jax.ShapeDtypeStruct((b, s, d), F32),
        compiler_params=_cparams(("parallel", "parallel")),
        name="c_out",
    )(oT, w_o.astype(BF16), h)


def _mlp_kernel(x_ref, g_ref, w1_ref, w2_ref, out_ref):
    x = x_ref[...]
    d = x.shape[-1]
    y = _rms_rows(x, g_ref[...]).astype(BF16)
    acc = x
    for c in range(w1_ref.shape[1] // d):
        cols = slice(c * d, (c + 1) * d)
        a = jnp.maximum(jnp.dot(y, w1_ref[:, cols], preferred_element_type=F32), 0.0)
        acc = acc + jnp.dot((a * a).astype(BF16), w2_ref[cols, :], preferred_element_type=F32)
    out_ref[...] = acc


def _mlp(h, g2, w1, w2):
    b, s, d = h.shape
    t = b * s
    tm = 512
    tok = pl.BlockSpec((tm, d), lambda i: (i, 0))
    out = pl.pallas_call(
        _mlp_kernel,
        grid=(t // tm,),
        in_specs=[tok, _resident((1, d)), _resident(w1.shape), _resident(w2.shape)],
        out_specs=tok,
        out_shape=jax.ShapeDtypeStruct((t, d), F32),
        compiler_params=_cparams(("parallel",)),
        name="mlp",
    )(h.reshape(t, d), g2, w1.astype(BF16), w2.astype(BF16))
    return out.reshape(b, s, d)


def kernel(x, norm1, norm2, a_wqkv, a_q_gain, a_k_gain, a_wo, b_win, b_conv, b_wout,
           c_wqkv, c_q_gain, c_k_gain, c_wo, mlp_w1, mlp_w2):
    h = x
    for i in range(norm1.shape[0]):
        kind, j = i % N_MIXERS, i // N_MIXERS
        g1 = norm1[i][None, :]
        if kind == 0:
            h = _mixer_a(h, g1, a_wqkv[j], a_q_gain[j], a_k_gain[j], a_wo[j])
        elif kind == 1:
            h = _mixer_b(h, g1, b_win[j], b_conv[j], b_wout[j])
        else:
            h = _mixer_c(h, g1, c_wqkv[j], c_q_gain[j], c_k_gain[j], c_wo[j])
        h = _mlp(h, norm2[i][None, :], mlp_w1[i], mlp_w2[i])
    return h
```

```python
import functools
import math

import jax
import jax.numpy as jnp
import numpy as np
from jax import lax
from jax.experimental import pallas as pl
from jax.experimental.pallas import tpu as pltpu

HEAD_DIM = 64
EPS = 1e-6
NEG_INF = -1e30
N_MIXERS = 3
A_WINDOWS = (128, 512, 2048)
A_DILATIONS = (1, 4, 16)
A_RADIUS = 64
ROPE_THETA = 500000.0
ROPE_DIMS = HEAD_DIM // 4
C_KV_HEADS = 4
C_THETA = 10000.0
GRID_W = 64
SCORE_SCALE = HEAD_DIM ** -0.5
LOG2E = math.log2(math.e)

LANES = 128
BF16_SUBLANES = 16
V_ROWS = HEAD_DIM + BF16_SUBLANES
PAIR_ROWS = 2 * HEAD_DIM + BF16_SUBLANES
SAFE_LOG2_SCORE = 64.0
PASS1_KEYS = 512
PASS1_UNROLL = 8
PASS1_QUERIES = 512
VMEM_LIMIT = 56 * 1024 * 1024

F32 = jnp.float32
BF16 = jnp.bfloat16

NT_DIMS = (((1,), (1,)), ((), ()))


def _cparams(sem):
    return pltpu.CompilerParams(dimension_semantics=sem, vmem_limit_bytes=VMEM_LIMIT)


def _resident(shape):
    nd = len(shape)
    return pl.BlockSpec(shape, lambda *_: (0,) * nd, pipeline_mode=pl.Buffered(1))


def _rms_rows(x, g):
    ms = jnp.mean(x * x, axis=-1, keepdims=True)
    return (x * lax.rsqrt(ms + EPS)) * g


def _head_norm_T(t, gain):
    n = t.shape[-1]
    t3 = t.reshape(t.shape[0] // HEAD_DIM, HEAD_DIM, n)
    ms = jnp.mean(t3 * t3, axis=1, keepdims=True)
    return (t3 * lax.rsqrt(ms + EPS)) * gain[None]


def _rot(a, b, c, s):
    return a * c - b * s, b * c + a * s


def _rope_tables_T(pos, dim, theta):
    inv = theta ** (-jnp.arange(0, dim, 2, dtype=F32) / dim)
    ang = inv[:, None] * pos.astype(F32)[None, :]
    return jnp.cos(ang), jnp.sin(ang)


def _mlp_tile(x, g_ref, w1_ref, w2_ref):
    d = x.shape[-1]
    y = _rms_rows(x, g_ref[...]).astype(BF16)
    acc = x
    for c in range(w1_ref.shape[1] // d):
        cols = slice(c * d, (c + 1) * d)
        a = jnp.maximum(jnp.dot(y, w1_ref[:, cols], preferred_element_type=F32), 0.0)
        acc = acc + jnp.dot((a * a).astype(BF16), w2_ref[cols, :], preferred_element_type=F32)
    return acc


def _out_mlp_kernel(tile_fn, n_in, *refs):
    ins = refs[:n_in]
    g_ref, w1_ref, w2_ref, out_ref, hm_even, hm_odd = refs[n_in:n_in + 6]
    tile_scr = refs[n_in + 6:]
    i = pl.program_id(0)

    @pl.when(i == 0)
    def _():
        hm_odd[...] = jnp.zeros_like(hm_odd)

    def step(dst, src):
        out_ref[...] = _mlp_tile(src[...], g_ref, w1_ref, w2_ref)
        dst[...] = tile_fn(*ins, *tile_scr)

    @pl.when(i % 2 == 0)
    def _():
        step(hm_even, hm_odd)

    @pl.when(i % 2 == 1)
    def _():
        step(hm_odd, hm_even)


def _out_mlp(tile_fn, ins, in_specs, tile_scratch, t, d, tm, g2, w1, w2, name):
    nt = t // tm
    clamp = lambda f: (lambda i: f(jnp.minimum(i, nt - 1)))
    specs = [pl.BlockSpec(sp.block_shape, clamp(sp.index_map), pipeline_mode=sp.pipeline_mode)
             for sp in in_specs]
    return pl.pallas_call(
        functools.partial(_out_mlp_kernel, tile_fn, len(ins)),
        grid=(nt + 1,),
        in_specs=specs + [_resident((1, d)), _resident(w1.shape), _resident(w2.shape)],
        out_specs=pl.BlockSpec((tm, d), lambda i: (jnp.maximum(i - 1, 0), 0)),
        out_shape=jax.ShapeDtypeStruct((t, d), F32),
        scratch_shapes=[pltpu.VMEM((tm, d), F32), pltpu.VMEM((tm, d), F32)] + tile_scratch,
        compiler_params=_cparams(("arbitrary",)),
        name=name,
    )(*ins, g2, w1, w2)


def _a_proj_kernel(dil, *refs):
    n_slab = len(refs) - 9
    x_slabs = refs[:n_slab]
    g_ref, wt_ref, gq_ref, gk_ref, cos_ref, sin_ref, qT_ref, k_ref, vT_ref = refs[n_slab:]
    d = wt_ref.shape[1]
    tl = cos_ref.shape[-1]
    width = gq_ref.shape[-1]
    half = ROPE_DIMS // 2
    if dil == 1:
        units = [[(0, p0, width)] for p0 in range(0, tl, width)]
    else:
        units = [[(r, 0, tl) for r in range(r0, r0 + width // tl)] for r0 in range(0, dil, width // tl)]

    def prep(unit):
        x = jnp.concatenate(
            [jnp.concatenate([xs[0, pl.ds(r + p0 * dil, cnt, stride=dil), :] for xs in x_slabs], axis=1)
             for r, p0, cnt in unit], axis=0)
        y = _rms_rows(x, g_ref[...]).astype(BF16)
        c = jnp.concatenate([cos_ref[r, :, p0:p0 + cnt] for r, p0, cnt in unit], axis=1)
        s = jnp.concatenate([sin_ref[r, :, p0:p0 + cnt] for r, p0, cnt in unit], axis=1)
        return y, c, s

    def qk_finish(t, gain, scale, c, s):
        tn = _head_norm_T(t, gain)
        ra, rb = _rot(tn[:, :half], tn[:, half:ROPE_DIMS], c[None], s[None])
        out = jnp.concatenate([ra, rb, tn[:, ROPE_DIMS:]], axis=1)
        if scale != 1.0:
            out = out * scale
        return out.reshape(t.shape)

    prepped = prep(units[0])
    for i, unit in enumerate(units):
        y, c, s = prepped
        tq, tk, tv = (lax.dot_general(wt_ref[j * d:(j + 1) * d, :], y, NT_DIMS, preferred_element_type=F32)
                      for j in range(3))
        if i + 1 < len(units):
            prepped = prep(units[i + 1])
        qT = qk_finish(tq, gq_ref[...], SCORE_SCALE * LOG2E, c, s).astype(BF16)
        k = qk_finish(tk, gk_ref[...], 1.0, c, s).T.astype(BF16)
        vT = tv.astype(BF16)
        npair = d // (2 * HEAD_DIM)
        ones_row = (lax.broadcasted_iota(jnp.int32, (npair, BF16_SUBLANES, width), 1) == 0).astype(BF16)
        vT = jnp.concatenate([vT.reshape(npair, 2 * HEAD_DIM, width), ones_row], axis=1)
        vT = vT.reshape(npair * PAIR_ROWS, width)
        col = 0
        for r, p0, cnt in unit:
            k_ref[0, r, p0:p0 + cnt] = k[col:col + cnt]
            for t in range(cnt // LANES):
                cols = slice(col + t * LANES, col + (t + 1) * LANES)
                qT_ref[0, r, p0 // LANES + t] = qT[:, cols]
                vT_ref[0, r, p0 // LANES + t] = vT[:, cols]
            col += cnt


def _a_proj(h, g1, wt, q_gain, k_gain, dil):
    b, s, d = h.shape
    l = s // dil
    width = 2 * LANES
    tl = max(LANES, 8 * LANES // dil)
    tt = tl * dil
    pos = (jnp.arange(l)[None, :] * dil + jnp.arange(dil)[:, None]).reshape(-1)
    cosT, sinT = _rope_tables_T(pos, ROPE_DIMS, ROPE_THETA)
    cosT = cosT.reshape(-1, dil, l).transpose(1, 0, 2)
    sinT = sinT.reshape(-1, dil, l).transpose(1, 0, 2)
    gq = jnp.broadcast_to(q_gain[:, None], (HEAD_DIM, width))
    gk = jnp.broadcast_to(k_gain[:, None], (HEAD_DIM, width))
    n_slab = d // LANES
    slab = lambda c: pl.BlockSpec((1, tt, LANES), lambda bi, i: (bi, i, c))
    vd = d // (2 * HEAD_DIM) * PAIR_ROWS
    feat = lambda rows: pl.BlockSpec((1, dil, tl // LANES, rows, LANES), lambda bi, i: (bi, 0, i, 0, 0))
    tab = pl.BlockSpec((dil, ROPE_DIMS // 2, tl), lambda bi, i: (0, 0, i))
    feat_shape = lambda rows: jax.ShapeDtypeStruct((b, dil, l // LANES, rows, LANES), BF16)
    return pl.pallas_call(
        functools.partial(_a_proj_kernel, dil),
        grid=(b, s // tt),
        in_specs=[slab(c) for c in range(n_slab)] + [
            _resident((1, d)), _resident((3 * d, d)),
            _resident((HEAD_DIM, width)), _resident((HEAD_DIM, width)),
            tab, tab,
        ],
        out_specs=[feat(d), pl.BlockSpec((1, dil, tl, d), lambda bi, i: (bi, 0, i, 0)), feat(vd)],
        out_shape=[feat_shape(d), jax.ShapeDtypeStruct((b, dil, l, d), BF16), feat_shape(vd)],
        compiler_params=_cparams(("parallel", "parallel")),
        name="a_proj",
    )(*([h] * n_slab), g1, wt, gq, gk, cosT, sinT)


def _a_attn_kernel(qT_ref, kprev_ref, kmain_ref, knext_ref, vprev_ref, vmain_ref, vnext_ref,
                   bias_ref, cap_ref, o_ref, m_ref, l_ref, oT_scr, st_scr):
    i = pl.program_id(1)
    tb = qT_ref.shape[1]
    d = qT_ref.shape[2]
    n_heads = d // HEAD_DIM
    quad = 4 * HEAD_DIM
    win = 2 * LANES
    first_sel = jnp.where(i == 0, 0, 1)
    last_sel = jnp.where(i == pl.num_programs(1) - 1, 2, 1)
    own_head = (lax.broadcasted_iota(jnp.int32, (quad, 4 * LANES), 0) // HEAD_DIM
                == lax.broadcasted_iota(jnp.int32, (quad, 4 * LANES), 1) // LANES)

    def window(s):
        lo, hi = s * LANES - A_RADIUS, s * LANES - A_RADIUS + win
        if s == 0:
            kw = jnp.concatenate([kprev_ref[0], kmain_ref[0, 0:hi]], axis=0)
            sel = first_sel
        elif s == tb - 1:
            kw = jnp.concatenate([kmain_ref[0, lo:tb * LANES], knext_ref[0]], axis=0)
            sel = last_sel
        else:
            kw = kmain_ref[0, lo:hi]
            sel = 1
        va = vprev_ref[0, 0] if s == 0 else vmain_ref[0, s - 1]
        vc = vnext_ref[0, 0] if s == tb - 1 else vmain_ref[0, s + 1]
        vwin = jnp.concatenate([va[:, LANES - A_RADIUS:], vmain_ref[0, s], vc[:, :A_RADIUS]], axis=1)
        return kw, sel, vwin

    def quad_scores(s, u, kw):
        q4 = qT_ref[0, s, u * quad:(u + 1) * quad, :]
        q4 = jnp.concatenate([q4] * 4, axis=1)
        qz = jnp.where(own_head, q4, jnp.zeros_like(q4))
        return jnp.dot(kw[:, u * quad:(u + 1) * quad], qz, preferred_element_type=F32)

    def pair_values(vcat, pair, p, pr):
        c0 = 2 * pr * LANES
        return jnp.dot(vcat[pair * PAIR_ROWS:(pair + 1) * PAIR_ROWS, :], p[:, c0:c0 + 2 * LANES],
                       preferred_element_type=F32)

    def finish(s):
        rows = slice(s * LANES, (s + 1) * LANES)
        o_ref[0, rows, :] = oT_scr[...].T.astype(BF16)
        m_ref[0, rows, :] = st_scr[0].T
        l_ref[0, rows, :] = st_scr[1].T

    def init_stats():
        st_scr[...] = jnp.zeros_like(st_scr)
        st_scr[1, n_heads:, :] = jnp.ones((LANES - n_heads, LANES), F32)

    init_stats()
    dens = []

    def quad_values(s, u, vcat, p):
        for pr in range(2):
            h0 = 4 * u + 2 * pr
            out = pair_values(vcat, h0 // 2, p, pr)
            for hh in range(2):
                l = out[2 * HEAD_DIM:2 * HEAD_DIM + 1, hh * LANES:(hh + 1) * LANES]
                oT_scr[(h0 + hh) * HEAD_DIM:(h0 + hh + 1) * HEAD_DIM, :] = (
                    out[hh * HEAD_DIM:(hh + 1) * HEAD_DIM, hh * LANES:(hh + 1) * LANES] / l)
                st_scr[1, h0 + hh:h0 + hh + 1, :] = l
                dens.append(l)
        if u == n_heads // 4 - 1:
            finish(s)

    pending = None
    for s in range(tb):
        kw, sel, vcat = window(s)
        cap = cap_ref[sel]
        for u in range(n_heads // 4):
            sT = quad_scores(s, u, kw)
            if pending is not None:
                quad_values(*pending)
            pending = (s, u, vcat, jnp.minimum(jnp.exp2(sT).astype(BF16), cap))
    quad_values(*pending)
    lmin, lmax = dens[0], dens[0]
    for l in dens[1:]:
        lmin, lmax = jnp.minimum(lmin, l), jnp.maximum(lmax, l)
    in_range = jnp.logical_and(jnp.max(lmax) <= 2.0 ** SAFE_LOG2_SCORE,
                               jnp.min(lmin) >= 2.0 ** -SAFE_LOG2_SCORE)

    @pl.when(jnp.logical_not(in_range))
    def _():
        init_stats()
        for s in range(tb):
            kw, sel, vcat = window(s)
            bias = bias_ref[sel]
            for u in range(n_heads // 4):
                sT = quad_scores(s, u, kw) + bias
                m = jnp.max(sT, axis=0, keepdims=True)
                p = jnp.exp2(sT - m)
                l = jnp.sum(p, axis=0, keepdims=True)
                for pr in range(2):
                    h0 = 4 * u + 2 * pr
                    out = pair_values(vcat, h0 // 2, p.astype(BF16), pr)
                    for hh in range(2):
                        cols = slice((2 * pr + hh) * LANES, (2 * pr + hh + 1) * LANES)
                        oT_scr[(h0 + hh) * HEAD_DIM:(h0 + hh + 1) * HEAD_DIM, :] = (
                            out[hh * HEAD_DIM:(hh + 1) * HEAD_DIM, hh * LANES:(hh + 1) * LANES] / l[:, cols])
                for g in range(4):
                    st_scr[0, 4 * u + g:4 * u + g + 1, :] = m[:, g * LANES:(g + 1) * LANES]
                    st_scr[1, 4 * u + g:4 * u + g + 1, :] = l[:, g * LANES:(g + 1) * LANES]
            finish(s)


def _band_masks():
    kk = np.arange(2 * LANES)[:, None]
    j = np.arange(LANES)[None, :]
    band = (kk >= j) & (kk <= j + 2 * A_RADIUS)
    variants = np.stack([band & (kk >= A_RADIUS), band, band & (kk < 2 * LANES - A_RADIUS)])
    variants = np.tile(variants, (1, 1, 4))
    bias = jnp.asarray(np.where(variants, 0.0, NEG_INF), F32)
    cap = jnp.asarray(np.where(variants, float(jnp.finfo(BF16).max), 0.0), BF16)
    return bias, cap


def _a_attn(qT, k, vT):
    b, dil, nt, d, _ = qT.shape
    vd = vT.shape[3]
    n, l = b * dil, nt * LANES
    qT, k, vT = qT.reshape(n, nt, d, LANES), k.reshape(n, l, d), vT.reshape(n, nt, vd, LANES)
    tb = min(4, nt)
    assert tb >= 2 and nt % tb == 0
    hb = A_RADIUS
    per = tb * LANES // hb
    tok = lambda width: pl.BlockSpec((1, tb * LANES, width), lambda ni, i: (ni, i, 0))
    feat = lambda rows: pl.BlockSpec((1, tb, rows, LANES), lambda ni, i: (ni, i, 0, 0))
    o, m, lsum = pl.pallas_call(
        _a_attn_kernel,
        grid=(n, nt // tb),
        in_specs=[
            feat(d),
            pl.BlockSpec((1, hb, d), lambda ni, i: (ni, jnp.maximum(i * per - 1, 0), 0)),
            tok(d),
            pl.BlockSpec((1, hb, d), lambda ni, i: (ni, jnp.minimum((i + 1) * per, l // hb - 1), 0)),
            pl.BlockSpec((1, 1, vd, LANES), lambda ni, i: (ni, jnp.maximum(i * tb - 1, 0), 0, 0)),
            feat(vd),
            pl.BlockSpec((1, 1, vd, LANES), lambda ni, i: (ni, jnp.minimum((i + 1) * tb, nt - 1), 0, 0)),
            _resident((3, 2 * LANES, 4 * LANES)),
            _resident((3, 2 * LANES, 4 * LANES)),
        ],
        out_specs=[tok(d), tok(LANES), tok(LANES)],
        out_shape=[
            jax.ShapeDtypeStruct((n, l, d), BF16),
            jax.ShapeDtypeStruct((n, l, LANES), F32),
            jax.ShapeDtypeStruct((n, l, LANES), F32),
        ],
        scratch_shapes=[pltpu.VMEM((d, LANES), F32), pltpu.VMEM((2, LANES, LANES), F32)],
        compiler_params=_cparams(("parallel", "parallel")),
        name="a_attn",
    )(qT, k, k, k, vT, vT, vT, *_band_masks())
    return (o.reshape(b, dil, l, d), m.reshape(b, dil, l, LANES), lsum.reshape(b, dil, l, LANES))


def _a_out_tile(n_heads, o0, o1, o2, m0, m1, m2, l0, l1, l2, e_ref, wo_ref, h_ref, o_scr, st_scr):
    tm = h_ref.shape[0]

    def stat_tokens(ref, slot):
        dil = ref.shape[1]
        if dil == 1:
            return ref[0, 0]
        for r in range(dil):
            st_scr[slot, pl.ds(r, tm // dil, stride=dil), :] = ref[0, r]
        return st_scr[slot]

    def out_tokens(ref):
        dil = ref.shape[1]
        if dil == 1:
            return ref[0, 0].astype(F32)
        for r in range(dil):
            blk = ref[0, r].astype(F32)
            for c in range(o_scr.shape[0]):
                o_scr[c, pl.ds(r, tm // dil, stride=dil), :] = blk[:, c * LANES:(c + 1) * LANES]
        return jnp.concatenate([o_scr[c] for c in range(o_scr.shape[0])], axis=1)

    ms = [stat_tokens(r, j) for j, r in enumerate((m0, m1, m2))]
    ls = [stat_tokens(r, 3 + j) for j, r in enumerate((l0, l1, l2))]
    mx = jnp.maximum(jnp.maximum(ms[0], ms[1]), ms[2])
    ws = [jnp.exp2(m - mx) * l for m, l in zip(ms, ls)]
    den = ws[0] + ws[1] + ws[2]
    head_lane = lax.broadcasted_iota(jnp.int32, den.shape, 1) < n_heads

    o = None
    for w, o_ref in zip(ws, (o0, o1, o2)):
        a = jnp.where(head_lane, w / den, 0.0)
        hi = a.astype(BF16).astype(F32)
        mid = (a - hi).astype(BF16).astype(F32)
        lo = (a - hi - mid).astype(BF16).astype(F32)
        packed = hi + pltpu.roll(mid, n_heads, axis=1) + pltpu.roll(lo, 2 * n_heads, axis=1)
        coef = jnp.dot(packed.astype(BF16), e_ref[...], preferred_element_type=F32)
        term = coef * out_tokens(o_ref)
        o = term if o is None else o + term
    return h_ref[...] + jnp.dot(o.astype(BF16), wo_ref[...], preferred_element_type=F32)


def _head_indicator(d):
    n_heads = d // HEAD_DIM
    e = np.zeros((LANES, d), np.float32)
    for part in range(3):
        for h in range(n_heads):
            e[part * n_heads + h, h * HEAD_DIM:(h + 1) * HEAD_DIM] = 1.0
    return jnp.asarray(e, BF16)


def _a_out_mlp(parts, wo, h, tm, g2, w1, w2):
    b, s, d = h.shape
    spb = s // tm
    blk = lambda dil, width: pl.BlockSpec((1, dil, tm // dil, width),
                                          lambda t: (t // spb, 0, t % spb, 0))
    os_, ms_, ls_ = zip(*parts)
    dils = [o.shape[1] for o in os_]
    tok = pl.BlockSpec((tm, d), lambda t: (t, 0))
    ins = [*os_, *ms_, *ls_, _head_indicator(d), wo, h.reshape(b * s, d)]
    in_specs = ([blk(dil, d) for dil in dils] + [blk(dil, LANES) for dil in dils] * 2
                + [_resident((LANES, d)), _resident((d, d)), tok])
    scratch = [pltpu.VMEM((d // LANES, tm, LANES), F32), pltpu.VMEM((6, tm, LANES), F32)]
    out = _out_mlp(functools.partial(_a_out_tile, d // HEAD_DIM), ins, in_specs, scratch,
                   b * s, d, tm, g2, w1, w2, "a_out_mlp")
    return out.reshape(b, s, d)


def _mixer_a(h, g1, w_qkv, q_gain, k_gain, w_o, g2, w1, w2):
    d = h.shape[-1]
    parts = []
    for g, dil in enumerate(A_DILATIONS):
        assert A_WINDOWS[g] // (2 * dil) == A_RADIUS
        wt = w_qkv[:, g * 3 * d:(g + 1) * 3 * d].T.astype(BF16)
        parts.append(_a_attn(*_a_proj(h, g1, wt, q_gain[g], k_gain[g], dil)))
    return _a_out_mlp(parts, w_o.astype(BF16), h, 512, g2, w1, w2)


def _b_in_kernel(x_ref, g_ref, w_ref, bg_ref, u_ref):
    d = x_ref.shape[-1]
    y = _rms_rows(x_ref[...], g_ref[...]).astype(BF16)
    bg_ref[...] = jnp.dot(y, w_ref[:, 0:d], preferred_element_type=F32).astype(BF16)
    cg = jnp.dot(y, w_ref[:, d:2 * d], preferred_element_type=F32)
    xt = jnp.dot(y, w_ref[:, 2 * d:3 * d], preferred_element_type=F32)
    u_ref[...] = (cg * xt).astype(BF16)


def _b_out_tile(tiles_per_seq, n_tiles, u_ref, up_ref, un_ref, bg_ref, cw_ref, wo_ref, h_ref, scr):
    i = jnp.minimum(pl.program_id(0), n_tiles - 1)
    tm = u_ref.shape[0]
    first = (i % tiles_per_seq) == 0
    last = (i % tiles_per_seq) == tiles_per_seq - 1
    hb = up_ref.shape[0]
    scr[0:hb, :] = jnp.where(first, 0.0, up_ref[...].astype(F32))
    scr[hb:hb + tm, :] = u_ref[...].astype(F32)
    scr[hb + tm:2 * hb + tm, :] = jnp.where(last, 0.0, un_ref[...].astype(F32))
    cw = cw_ref[...]
    y = (scr[hb - 1:hb - 1 + tm, :] * cw[0:1] + scr[hb:hb + tm, :] * cw[1:2]
         + scr[hb + 1:hb + 1 + tm, :] * cw[2:3])
    z = (bg_ref[...].astype(F32) * y).astype(BF16)
    return h_ref[...] + jnp.dot(z, wo_ref[...], preferred_element_type=F32)


def _mixer_b(h, g1, w_in, conv_w, w_out, g2, w1, w2):
    b, s, d = h.shape
    t = b * s
    tm = 512
    h2 = h.reshape(t, d)
    tok = pl.BlockSpec((tm, d), lambda i: (i, 0))
    bg, u = pl.pallas_call(
        _b_in_kernel,
        grid=(t // tm,),
        in_specs=[tok, _resident((1, d)), _resident((d, 3 * d))],
        out_specs=[tok, tok],
        out_shape=[jax.ShapeDtypeStruct((t, d), BF16)] * 2,
        compiler_params=_cparams(("parallel",)),
        name="b_in",
    )(h2, g1, w_in.astype(BF16))
    hb = BF16_SUBLANES
    per = tm // hb
    in_specs = [
        tok,
        pl.BlockSpec((hb, d), lambda i: (jnp.maximum(i * per - 1, 0), 0)),
        pl.BlockSpec((hb, d), lambda i: (jnp.minimum((i + 1) * per, t // hb - 1), 0)),
        tok, _resident((3, d)), _resident((d, d)), tok,
    ]
    out = _out_mlp(functools.partial(_b_out_tile, s // tm, t // tm),
                   [u, u, u, bg, conv_w, w_out.astype(BF16), h2], in_specs,
                   [pltpu.VMEM((tm + 2 * hb, d), F32)], t, d, tm, g2, w1, w2, "b_out_mlp")
    return out.reshape(b, s, d)


def _axial_rope_T(tn, cr, sr, cc, sc):
    q = HEAD_DIM // 4
    a0, b0 = _rot(tn[:, 0:q], tn[:, q:2 * q], cr[None], sr[None])
    a1, b1 = _rot(tn[:, 2 * q:3 * q], tn[:, 3 * q:], cc[None], sc[None])
    return jnp.concatenate([a0, b0, a1, b1], axis=1)


def _c_proj_kernel(x_ref, g_ref, wq_ref, wk_ref, wv_ref, gq_ref, gk_ref, cr_ref, sr_ref, cc_ref, sc_ref,
                   qT_ref, k_ref, vT_ref):
    y = _rms_rows(x_ref[0], g_ref[...]).astype(BF16)
    tabs = (cr_ref[...], sr_ref[...], cc_ref[...], sc_ref[...])
    qT = lax.dot_general(wq_ref[...], y, NT_DIMS, preferred_element_type=F32)
    qn = _axial_rope_T(_head_norm_T(qT, gq_ref[...]), *tabs) * (SCORE_SCALE * LOG2E)
    qT_ref[0] = qn.reshape(qT.shape).astype(BF16)
    kT = lax.dot_general(wk_ref[...], y, NT_DIMS, preferred_element_type=F32)
    kn = _axial_rope_T(_head_norm_T(kT, gk_ref[...]), *tabs)
    k_ref[0] = kn.reshape(kT.shape).T.astype(BF16)
    vT = lax.dot_general(wv_ref[...], y, NT_DIMS, preferred_element_type=F32).astype(BF16)
    hkv, tm = vT_ref.shape[1], vT.shape[-1]
    ones_row = (lax.broadcasted_iota(jnp.int32, (hkv, BF16_SUBLANES, tm), 1) == 0).astype(BF16)
    vT_ref[0, :, 0] = jnp.concatenate([vT.reshape(hkv, HEAD_DIM, tm), ones_row], axis=1)


def _c_attn_kernel(qT_ref, k_ref, vT_ref, oT_ref, qz_scr, acc_scr, p_scr, s_scr, cm_scr,
                   m_scr, l_scr, sacc_scr):
    grp = qT_ref.shape[1] // HEAD_DIM
    tq = qT_ref.shape[-1]
    nc, tk = vT_ref.shape[2], vT_ref.shape[-1]
    n = grp * tq

    half = lax.broadcasted_iota(jnp.int32, (LANES, tq), 0) // HEAD_DIM
    keep = half == pl.program_id(1) % 2
    for g in range(grp):
        q = qT_ref[0, g * HEAD_DIM:(g + 1) * HEAD_DIM, :]
        q2 = jnp.concatenate([q, q], axis=0)
        qz_scr[:, g * tq:(g + 1) * tq] = jnp.where(keep, q2, jnp.zeros_like(q2))

    def key_chunk(c):
        return k_ref[0, pl.ds(pl.multiple_of(c * tk, tk), tk), :]

    def store_out(o):
        for g in range(grp):
            oT_ref[0, g * HEAD_DIM:(g + 1) * HEAD_DIM, :] = o[:, g * tq:(g + 1) * tq].astype(BF16)

    acc_scr[...] = jnp.zeros_like(acc_scr)

    span = PASS1_KEYS // tk
    nq = n // PASS1_QUERIES

    def keys_of(c0):
        return k_ref[0, pl.ds(pl.multiple_of(c0 * tk, PASS1_KEYS), PASS1_KEYS), :]

    def group_values(c0, slot, j):
        pv = None
        for jc in range(span):
            pv_c = jnp.dot(vT_ref[0, 0, c0 + jc], p_scr[slot, j, jc * tk:(jc + 1) * tk, :],
                           preferred_element_type=F32)
            pv = pv_c if pv is None else pv + pv_c
        return pv

    def plain_group(cg, carry):
        pvs = [None] * nq
        pending = None
        for u in range(PASS1_UNROLL):
            c0 = (PASS1_UNROLL * cg + u) * span
            k = keys_of(c0)
            for j in range(nq):
                cols = slice(j * PASS1_QUERIES, (j + 1) * PASS1_QUERIES)
                sT = jnp.dot(k, qz_scr[:, cols], preferred_element_type=F32)
                if pending is not None:
                    pv_j = group_values(*pending, j)
                    pvs[j] = pv_j if pvs[j] is None else pvs[j] + pv_j
                p_scr[u % 2, j] = jnp.exp2(sT).astype(BF16)
            pending = (c0, u % 2)
        for j in range(nq):
            cols = slice(j * PASS1_QUERIES, (j + 1) * PASS1_QUERIES)
            acc_scr[:, cols] += pvs[j] + group_values(*pending, j)
        return carry

    lax.fori_loop(0, nc // (PASS1_UNROLL * span), plain_group, 0)
    den = acc_scr[HEAD_DIM:HEAD_DIM + 1, :]
    in_range = jnp.logical_and(jnp.max(den) <= 2.0 ** SAFE_LOG2_SCORE,
                               jnp.min(den) >= 2.0 ** -SAFE_LOG2_SCORE)

    @pl.when(in_range)
    def _():
        store_out(acc_scr[:HEAD_DIM, :] / den)

    @pl.when(jnp.logical_not(in_range))
    def _():
        m_scr[...] = jnp.full_like(m_scr, -jnp.inf)
        l_scr[...] = jnp.zeros_like(l_scr)
        sacc_scr[...] = jnp.zeros_like(sacc_scr)

        def scores(c, slot):
            sT = jnp.dot(key_chunk(c), qz_scr[...], preferred_element_type=F32)
            s_scr[slot] = sT
            cm_scr[slot] = jnp.max(sT, axis=0, keepdims=True)

        def absorb(c, slot):
            m_old = m_scr[...]
            m_new = jnp.maximum(m_old, cm_scr[slot])
            p = jnp.exp2(s_scr[slot] - m_new)
            alpha = jnp.exp2(m_old - m_new)
            l_scr[...] = alpha * l_scr[...] + jnp.sum(p, axis=0, keepdims=True)
            sacc_scr[...] = alpha * sacc_scr[...] + jnp.dot(
                vT_ref[0, 0, c, :HEAD_DIM, :], p.astype(BF16), preferred_element_type=F32)
            m_scr[...] = m_new

        scores(0, 0)

        def pair(c2, carry):
            c = 2 * c2
            scores(c + 1, 1)
            absorb(c, 0)
            scores(c + 2, 0)
            absorb(c + 1, 1)
            return carry

        lax.fori_loop(0, nc // 2 - 1, pair, 0)
        scores(nc - 1, 1)
        absorb(nc - 2, 0)
        absorb(nc - 1, 1)
        store_out(sacc_scr[...] / l_scr[...])


def _c_out_tile(oT_ref, wo_ref, h_ref):
    o = oT_ref[0].astype(F32).T.astype(BF16)
    return h_ref[...] + jnp.dot(o, wo_ref[...], preferred_element_type=F32)


def _mixer_c(h, g1, w_qkv, q_gain, k_gain, w_o, g2, w1, w2):
    b, s, d = h.shape
    nq = d
    nk = C_KV_HEADS * HEAD_DIM
    tm = 512
    nc = s // tm
    assert nc % 2 == 0 and nc >= 4 and (nc * tm) % (PASS1_UNROLL * PASS1_KEYS) == 0
    pos = jnp.arange(s)
    cr, sr = _rope_tables_T(pos // GRID_W, HEAD_DIM // 2, C_THETA)
    cc, sc = _rope_tables_T(pos % GRID_W, HEAD_DIM // 2, C_THETA)
    wq = w_qkv[:, :nq].T.astype(BF16)
    wk = w_qkv[:, nq:nq + nk].T.astype(BF16)
    wv = w_qkv[:, nq + nk:].T.astype(BF16)
    gq = jnp.broadcast_to(q_gain[:, None], (HEAD_DIM, tm))
    gk = jnp.broadcast_to(k_gain[:, None], (HEAD_DIM, tm))
    tab = pl.BlockSpec((HEAD_DIM // 4, tm), lambda bi, i: (0, i))
    qT, k, vT = pl.pallas_call(
        _c_proj_kernel,
        grid=(b, nc),
        in_specs=[
            pl.BlockSpec((1, tm, d), lambda bi, i: (bi, i, 0)),
            _resident((1, d)), _resident((nq, d)), _resident((nk, d)), _resident((nk, d)),
            _resident((HEAD_DIM, tm)), _resident((HEAD_DIM, tm)),
            tab, tab, tab, tab,
        ],
        out_specs=[
            pl.BlockSpec((1, nq, tm), lambda bi, i: (bi, 0, i)),
            pl.BlockSpec((1, tm, nk), lambda bi, i: (bi, i, 0)),
            pl.BlockSpec((1, C_KV_HEADS, 1, V_ROWS, tm), lambda bi, i: (bi, 0, i, 0, 0)),
        ],
        out_shape=[
            jax.ShapeDtypeStruct((b, nq, s), BF16),
            jax.ShapeDtypeStruct((b, s, nk), BF16),
            jax.ShapeDtypeStruct((b, C_KV_HEADS, nc, V_ROWS, tm), BF16),
        ],
        compiler_params=_cparams(("parallel", "parallel")),
        name="c_proj",
    )(h, g1, wq, wk, wv, gq, gk, cr, sr, cc, sc)

    grp = nq // nk
    tq = 512
    gw = grp * HEAD_DIM
    oT = pl.pallas_call(
        _c_attn_kernel,
        grid=(b, C_KV_HEADS, s // tq),
        in_specs=[
            pl.BlockSpec((1, gw, tq), lambda bi, hk, i: (bi, hk, i)),
            pl.BlockSpec((1, s, LANES), lambda bi, hk, i: (bi, 0, hk // 2)),
            pl.BlockSpec((1, 1, nc, V_ROWS, tm), lambda bi, hk, i: (bi, hk, 0, 0, 0)),
        ],
        out_specs=pl.BlockSpec((1, gw, tq), lambda bi, hk, i: (bi, hk, i)),
        out_shape=jax.ShapeDtypeStruct((b, nq, s), BF16),
        scratch_shapes=[pltpu.VMEM((LANES, grp * tq), BF16),
                        pltpu.VMEM((V_ROWS, grp * tq), F32),
                        pltpu.VMEM((2, grp * tq // PASS1_QUERIES, PASS1_KEYS, PASS1_QUERIES), BF16),
                        pltpu.VMEM((2, tm, grp * tq), F32), pltpu.VMEM((2, 1, grp * tq), F32),
                        pltpu.VMEM((1, grp * tq), F32), pltpu.VMEM((1, grp * tq), F32),
                        pltpu.VMEM((HEAD_DIM, grp * tq), F32)],
        compiler_params=_cparams(("parallel", "parallel", "parallel")),
        name="c_attn",
    )(qT, k, vT)

    in_specs = [
        pl.BlockSpec((1, d, tm), lambda t: (t // nc, 0, t % nc)),
        _resident((d, d)),
        pl.BlockSpec((tm, d), lambda t: (t, 0)),
    ]
    out = _out_mlp(_c_out_tile, [oT, w_o.astype(BF16), h.reshape(b * s, d)], in_specs, [],
                   b * s, d, tm, g2, w1, w2, "c_out_mlp")
    return out.reshape(b, s, d)


def kernel(x, norm1, norm2, a_wqkv, a_q_gain, a_k_gain, a_wo, b_win, b_conv, b_wout,
           c_wqkv, c_q_gain, c_k_gain, c_wo, mlp_w1, mlp_w2):
    h = x
    for i in range(norm1.shape[0]):
        kind, j = i % N_MIXERS, i // N_MIXERS
        g1 = norm1[i][None, :]
        mlp = (norm2[i][None, :], mlp_w1[i].astype(BF16), mlp_w2[i].astype(BF16))
        if kind == 0:
            h = _mixer_a(h, g1, a_wqkv[j], a_q_gain[j], a_k_gain[j], a_wo[j], *mlp)
        elif kind == 1:
            h = _mixer_b(h, g1, b_win[j], b_conv[j], b_wout[j], *mlp)
        else:
            h = _mixer_c(h, g1, c_wqkv[j], c_q_gain[j], c_k_gain[j], c_wo[j], *mlp)
    return h
```

```python
import functools
import math

import jax
import jax.numpy as jnp
import numpy as np
from jax import lax
from jax.experimental import pallas as pl
from jax.experimental.pallas import tpu as pltpu

HEAD_DIM = 64
EPS = 1e-6
NEG_INF = -1e30
N_MIXERS = 3
A_WINDOWS = (128, 512, 2048)
A_DILATIONS = (1, 4, 16)
A_RADIUS = 64
ROPE_THETA = 500000.0
ROPE_DIMS = HEAD_DIM // 4
C_KV_HEADS = 4
C_THETA = 10000.0
GRID_W = 64
SCORE_SCALE = HEAD_DIM ** -0.5
LOG2E = math.log2(math.e)

LANES = 128
BF16_SUBLANES = 16
V_ROWS = HEAD_DIM + BF16_SUBLANES
PAIR_ROWS = 2 * HEAD_DIM + BF16_SUBLANES
SAFE_LOG2_SCORE = 64.0
PASS1_KEYS = 512
PASS1_UNROLL = 8
PASS1_QUERIES = 512
VMEM_LIMIT = 56 * 1024 * 1024

F32 = jnp.float32
BF16 = jnp.bfloat16

NT_DIMS = (((1,), (1,)), ((), ()))


def _cparams(sem):
    return pltpu.CompilerParams(dimension_semantics=sem, vmem_limit_bytes=VMEM_LIMIT)


def _resident(shape):
    nd = len(shape)
    return pl.BlockSpec(shape, lambda *_: (0,) * nd, pipeline_mode=pl.Buffered(1))


def _rms_rows(x, g):
    ms = jnp.mean(x * x, axis=-1, keepdims=True)
    return (x * lax.rsqrt(ms + EPS)) * g


def _head_norm_T(t, gain):
    n = t.shape[-1]
    t3 = t.reshape(t.shape[0] // HEAD_DIM, HEAD_DIM, n)
    ms = jnp.mean(t3 * t3, axis=1, keepdims=True)
    return (t3 * lax.rsqrt(ms + EPS)) * gain[None]


def _rot(a, b, c, s):
    return a * c - b * s, b * c + a * s


def _rope_tables_T(pos, dim, theta):
    inv = theta ** (-jnp.arange(0, dim, 2, dtype=F32) / dim)
    ang = inv[:, None] * pos.astype(F32)[None, :]
    return jnp.cos(ang), jnp.sin(ang)


def _mlp_tile(x, g_ref, w1_ref, w2_ref):
    d = x.shape[-1]
    y = _rms_rows(x, g_ref[...]).astype(BF16)
    acc = x
    for c in range(w1_ref.shape[1] // d):
        cols = slice(c * d, (c + 1) * d)
        a = jnp.maximum(jnp.dot(y, w1_ref[:, cols], preferred_element_type=F32), 0.0)
        acc = acc + jnp.dot((a * a).astype(BF16), w2_ref[cols, :], preferred_element_type=F32)
    return acc


def _out_mlp_kernel(tile_fn, n_in, *refs):
    ins = refs[:n_in]
    g_ref, w1_ref, w2_ref, out_ref, hm_even, hm_odd = refs[n_in:n_in + 6]
    tile_scr = refs[n_in + 6:]
    i = pl.program_id(0)

    @pl.when(i == 0)
    def _():
        hm_odd[...] = jnp.zeros_like(hm_odd)

    *tile_ins, wo_ref, h_ref = ins

    def step(dst, src):
        lhs = tile_fn(*tile_ins, *tile_scr)
        out_ref[...] = _mlp_tile(src[...], g_ref, w1_ref, w2_ref)
        dst[...] = h_ref[...] + jnp.dot(lhs, wo_ref[...], preferred_element_type=F32)

    @pl.when(i % 2 == 0)
    def _():
        step(hm_even, hm_odd)

    @pl.when(i % 2 == 1)
    def _():
        step(hm_odd, hm_even)


def _out_mlp(tile_fn, ins, in_specs, tile_scratch, t, d, tm, g2, w1, w2, name):
    nt = t // tm
    clamp = lambda f: (lambda i: f(jnp.minimum(i, nt - 1)))
    specs = [pl.BlockSpec(sp.block_shape, clamp(sp.index_map), pipeline_mode=sp.pipeline_mode)
             for sp in in_specs]
    return pl.pallas_call(
        functools.partial(_out_mlp_kernel, tile_fn, len(ins)),
        grid=(nt + 1,),
        in_specs=specs + [_resident((1, d)), _resident(w1.shape), _resident(w2.shape)],
        out_specs=pl.BlockSpec((tm, d), lambda i: (jnp.maximum(i - 1, 0), 0)),
        out_shape=jax.ShapeDtypeStruct((t, d), F32),
        scratch_shapes=[pltpu.VMEM((tm, d), F32), pltpu.VMEM((tm, d), F32)] + tile_scratch,
        compiler_params=_cparams(("arbitrary",)),
        name=name,
    )(*ins, g2, w1, w2)


def _a_proj_kernel(dil, *refs):
    n_slab = len(refs) - 9
    x_slabs = refs[:n_slab]
    g_ref, wt_ref, gq_ref, gk_ref, cos_ref, sin_ref, qT_ref, k_ref, vT_ref = refs[n_slab:]
    d = wt_ref.shape[1]
    tl = cos_ref.shape[-1]
    width = gq_ref.shape[-1]
    half = ROPE_DIMS // 2
    if dil == 1:
        units = [[(0, p0, width)] for p0 in range(0, tl, width)]
    else:
        units = [[(r, 0, tl) for r in range(r0, r0 + width // tl)] for r0 in range(0, dil, width // tl)]

    def prep(unit):
        x = jnp.concatenate(
            [jnp.concatenate([xs[0, pl.ds(r + p0 * dil, cnt, stride=dil), :] for xs in x_slabs], axis=1)
             for r, p0, cnt in unit], axis=0)
        y = _rms_rows(x, g_ref[...]).astype(BF16)
        c = jnp.concatenate([cos_ref[r, :, p0:p0 + cnt] for r, p0, cnt in unit], axis=1)
        s = jnp.concatenate([sin_ref[r, :, p0:p0 + cnt] for r, p0, cnt in unit], axis=1)
        return y, c, s

    def qk_finish(t, gain, scale, c, s):
        tn = _head_norm_T(t, gain)
        ra, rb = _rot(tn[:, :half], tn[:, half:ROPE_DIMS], c[None], s[None])
        out = jnp.concatenate([ra, rb, tn[:, ROPE_DIMS:]], axis=1)
        if scale != 1.0:
            out = out * scale
        return out.reshape(t.shape)

    prepped = prep(units[0])
    for i, unit in enumerate(units):
        y, c, s = prepped
        tq, tk, tv = (lax.dot_general(wt_ref[j * d:(j + 1) * d, :], y, NT_DIMS, preferred_element_type=F32)
                      for j in range(3))
        if i + 1 < len(units):
            prepped = prep(units[i + 1])
        qT = qk_finish(tq, gq_ref[...], SCORE_SCALE * LOG2E, c, s).astype(BF16)
        k = qk_finish(tk, gk_ref[...], 1.0, c, s).T.astype(BF16)
        vT = tv.astype(BF16)
        npair = d // (2 * HEAD_DIM)
        ones_row = (lax.broadcasted_iota(jnp.int32, (npair, BF16_SUBLANES, width), 1) == 0).astype(BF16)
        vT = jnp.concatenate([vT.reshape(npair, 2 * HEAD_DIM, width), ones_row], axis=1)
        vT = vT.reshape(npair * PAIR_ROWS, width)
        col = 0
        for r, p0, cnt in unit:
            k_ref[0, r, p0:p0 + cnt] = k[col:col + cnt]
            for t in range(cnt // LANES):
                cols = slice(col + t * LANES, col + (t + 1) * LANES)
                qT_ref[0, r, p0 // LANES + t] = qT[:, cols]
                vT_ref[0, r, p0 // LANES + t] = vT[:, cols]
            col += cnt


def _a_proj(h, g1, wt, q_gain, k_gain, dil):
    b, s, d = h.shape
    l = s // dil
    width = 2 * LANES
    tl = max(LANES, 8 * LANES // dil)
    tt = tl * dil
    pos = (jnp.arange(l)[None, :] * dil + jnp.arange(dil)[:, None]).reshape(-1)
    cosT, sinT = _rope_tables_T(pos, ROPE_DIMS, ROPE_THETA)
    cosT = cosT.reshape(-1, dil, l).transpose(1, 0, 2)
    sinT = sinT.reshape(-1, dil, l).transpose(1, 0, 2)
    gq = jnp.broadcast_to(q_gain[:, None], (HEAD_DIM, width))
    gk = jnp.broadcast_to(k_gain[:, None], (HEAD_DIM, width))
    n_slab = d // LANES
    slab = lambda c: pl.BlockSpec((1, tt, LANES), lambda bi, i: (bi, i, c))
    vd = d // (2 * HEAD_DIM) * PAIR_ROWS
    feat = lambda rows: pl.BlockSpec((1, dil, tl // LANES, rows, LANES), lambda bi, i: (bi, 0, i, 0, 0))
    tab = pl.BlockSpec((dil, ROPE_DIMS // 2, tl), lambda bi, i: (0, 0, i))
    feat_shape = lambda rows: jax.ShapeDtypeStruct((b, dil, l // LANES, rows, LANES), BF16)
    return pl.pallas_call(
        functools.partial(_a_proj_kernel, dil),
        grid=(b, s // tt),
        in_specs=[slab(c) for c in range(n_slab)] + [
            _resident((1, d)), _resident((3 * d, d)),
            _resident((HEAD_DIM, width)), _resident((HEAD_DIM, width)),
            tab, tab,
        ],
        out_specs=[feat(d), pl.BlockSpec((1, dil, tl, d), lambda bi, i: (bi, 0, i, 0)), feat(vd)],
        out_shape=[feat_shape(d), jax.ShapeDtypeStruct((b, dil, l, d), BF16), feat_shape(vd)],
        compiler_params=_cparams(("parallel", "parallel")),
        name="a_proj",
    )(*([h] * n_slab), g1, wt, gq, gk, cosT, sinT)


def _a_attn_kernel(qT_ref, kprev_ref, kmain_ref, knext_ref, vprev_ref, vmain_ref, vnext_ref,
                   bias_ref, cap_ref, o_ref, m_ref, l_ref, oT_scr, st_scr):
    i = pl.program_id(1)
    tb = qT_ref.shape[1]
    d = qT_ref.shape[2]
    n_heads = d // HEAD_DIM
    quad = 4 * HEAD_DIM
    win = 2 * LANES
    first_sel = jnp.where(i == 0, 0, 1)
    last_sel = jnp.where(i == pl.num_programs(1) - 1, 2, 1)
    own_head = (lax.broadcasted_iota(jnp.int32, (quad, 4 * LANES), 0) // HEAD_DIM
                == lax.broadcasted_iota(jnp.int32, (quad, 4 * LANES), 1) // LANES)

    def window(s):
        lo, hi = s * LANES - A_RADIUS, s * LANES - A_RADIUS + win
        if s == 0:
            kw = jnp.concatenate([kprev_ref[0], kmain_ref[0, 0:hi]], axis=0)
            sel = first_sel
        elif s == tb - 1:
            kw = jnp.concatenate([kmain_ref[0, lo:tb * LANES], knext_ref[0]], axis=0)
            sel = last_sel
        else:
            kw = kmain_ref[0, lo:hi]
            sel = 1
        va = vprev_ref[0, 0] if s == 0 else vmain_ref[0, s - 1]
        vc = vnext_ref[0, 0] if s == tb - 1 else vmain_ref[0, s + 1]
        vwin = jnp.concatenate([va[:, LANES - A_RADIUS:], vmain_ref[0, s], vc[:, :A_RADIUS]], axis=1)
        return kw, sel, vwin

    def quad_scores(s, u, kw):
        q4 = qT_ref[0, s, u * quad:(u + 1) * quad, :]
        q4 = jnp.concatenate([q4] * 4, axis=1)
        qz = jnp.where(own_head, q4, jnp.zeros_like(q4))
        return jnp.dot(kw[:, u * quad:(u + 1) * quad], qz, preferred_element_type=F32)

    def pair_values(vcat, pair, p, pr):
        c0 = 2 * pr * LANES
        return jnp.dot(vcat[pair * PAIR_ROWS:(pair + 1) * PAIR_ROWS, :], p[:, c0:c0 + 2 * LANES],
                       preferred_element_type=F32)

    def finish(s):
        rows = slice(s * LANES, (s + 1) * LANES)
        o_ref[0, rows, :] = oT_scr[...].T.astype(BF16)
        m_ref[0, rows, :] = st_scr[0].T
        l_ref[0, rows, :] = st_scr[1].T

    def init_stats():
        st_scr[...] = jnp.zeros_like(st_scr)
        st_scr[1, n_heads:, :] = jnp.ones((LANES - n_heads, LANES), F32)

    init_stats()
    dens = []

    def quad_values(s, u, vcat, p):
        for pr in range(2):
            h0 = 4 * u + 2 * pr
            out = pair_values(vcat, h0 // 2, p, pr)
            for hh in range(2):
                l = out[2 * HEAD_DIM:2 * HEAD_DIM + 1, hh * LANES:(hh + 1) * LANES]
                oT_scr[(h0 + hh) * HEAD_DIM:(h0 + hh + 1) * HEAD_DIM, :] = (
                    out[hh * HEAD_DIM:(hh + 1) * HEAD_DIM, hh * LANES:(hh + 1) * LANES] / l)
                st_scr[1, h0 + hh:h0 + hh + 1, :] = l
                dens.append(l)
        if u == n_heads // 4 - 1:
            finish(s)

    pending = None
    for s in range(tb):
        kw, sel, vcat = window(s)
        cap = cap_ref[sel]
        for u in range(n_heads // 4):
            sT = quad_scores(s, u, kw)
            if pending is not None:
                quad_values(*pending)
            pending = (s, u, vcat, jnp.minimum(jnp.exp2(sT).astype(BF16), cap))
    quad_values(*pending)
    lmin, lmax = dens[0], dens[0]
    for l in dens[1:]:
        lmin, lmax = jnp.minimum(lmin, l), jnp.maximum(lmax, l)
    in_range = jnp.logical_and(jnp.max(lmax) <= 2.0 ** SAFE_LOG2_SCORE,
                               jnp.min(lmin) >= 2.0 ** -SAFE_LOG2_SCORE)

    @pl.when(jnp.logical_not(in_range))
    def _():
        init_stats()
        for s in range(tb):
            kw, sel, vcat = window(s)
            bias = bias_ref[sel]
            for u in range(n_heads // 4):
                sT = quad_scores(s, u, kw) + bias
                m = jnp.max(sT, axis=0, keepdims=True)
                p = jnp.exp2(sT - m)
                l = jnp.sum(p, axis=0, keepdims=True)
                for pr in range(2):
                    h0 = 4 * u + 2 * pr
                    out = pair_values(vcat, h0 // 2, p.astype(BF16), pr)
                    for hh in range(2):
                        cols = slice((2 * pr + hh) * LANES, (2 * pr + hh + 1) * LANES)
                        oT_scr[(h0 + hh) * HEAD_DIM:(h0 + hh + 1) * HEAD_DIM, :] = (
                            out[hh * HEAD_DIM:(hh + 1) * HEAD_DIM, hh * LANES:(hh + 1) * LANES] / l[:, cols])
                for g in range(4):
                    st_scr[0, 4 * u + g:4 * u + g + 1, :] = m[:, g * LANES:(g + 1) * LANES]
                    st_scr[1, 4 * u + g:4 * u + g + 1, :] = l[:, g * LANES:(g + 1) * LANES]
            finish(s)


def _band_masks():
    kk = np.arange(2 * LANES)[:, None]
    j = np.arange(LANES)[None, :]
    band = (kk >= j) & (kk <= j + 2 * A_RADIUS)
    variants = np.stack([band & (kk >= A_RADIUS), band, band & (kk < 2 * LANES - A_RADIUS)])
    variants = np.tile(variants, (1, 1, 4))
    bias = jnp.asarray(np.where(variants, 0.0, NEG_INF), F32)
    cap = jnp.asarray(np.where(variants, float(jnp.finfo(BF16).max), 0.0), BF16)
    return bias, cap


def _a_attn(qT, k, vT):
    b, dil, nt, d, _ = qT.shape
    vd = vT.shape[3]
    n, l = b * dil, nt * LANES
    qT, k, vT = qT.reshape(n, nt, d, LANES), k.reshape(n, l, d), vT.reshape(n, nt, vd, LANES)
    tb = min(8, nt)
    assert tb >= 2 and nt % tb == 0
    hb = A_RADIUS
    per = tb * LANES // hb
    tok = lambda width: pl.BlockSpec((1, tb * LANES, width), lambda ni, i: (ni, i, 0))
    feat = lambda rows: pl.BlockSpec((1, tb, rows, LANES), lambda ni, i: (ni, i, 0, 0))
    o, m, lsum = pl.pallas_call(
        _a_attn_kernel,
        grid=(n, nt // tb),
        in_specs=[
            feat(d),
            pl.BlockSpec((1, hb, d), lambda ni, i: (ni, jnp.maximum(i * per - 1, 0), 0)),
            tok(d),
            pl.BlockSpec((1, hb, d), lambda ni, i: (ni, jnp.minimum((i + 1) * per, l // hb - 1), 0)),
            pl.BlockSpec((1, 1, vd, LANES), lambda ni, i: (ni, jnp.maximum(i * tb - 1, 0), 0, 0)),
            feat(vd),
            pl.BlockSpec((1, 1, vd, LANES), lambda ni, i: (ni, jnp.minimum((i + 1) * tb, nt - 1), 0, 0)),
            _resident((3, 2 * LANES, 4 * LANES)),
            _resident((3, 2 * LANES, 4 * LANES)),
        ],
        out_specs=[tok(d), tok(LANES), tok(LANES)],
        out_shape=[
            jax.ShapeDtypeStruct((n, l, d), BF16),
            jax.ShapeDtypeStruct((n, l, LANES), F32),
            jax.ShapeDtypeStruct((n, l, LANES), F32),
        ],
        scratch_shapes=[pltpu.VMEM((d, LANES), F32), pltpu.VMEM((2, LANES, LANES), F32)],
        compiler_params=_cparams(("parallel", "parallel")),
        name="a_attn",
    )(qT, k, k, k, vT, vT, vT, *_band_masks())
    return (o.reshape(b, dil, l, d), m.reshape(b, dil, l, LANES), lsum.reshape(b, dil, l, LANES))


def _a_out_tile(n_heads, o0, o1, o2, m0, m1, m2, l0, l1, l2, e_ref, o_scr, st_scr):
    tm = o_scr.shape[1]

    def stat_tokens(ref, slot):
        dil = ref.shape[1]
        if dil == 1:
            return ref[0, 0]
        for r in range(dil):
            st_scr[slot, pl.ds(r, tm // dil, stride=dil), :] = ref[0, r]
        return st_scr[slot]

    def out_tokens(ref):
        dil = ref.shape[1]
        if dil == 1:
            return ref[0, 0].astype(F32)
        for r in range(dil):
            blk = ref[0, r].astype(F32)
            for c in range(o_scr.shape[0]):
                o_scr[c, pl.ds(r, tm // dil, stride=dil), :] = blk[:, c * LANES:(c + 1) * LANES]
        return jnp.concatenate([o_scr[c] for c in range(o_scr.shape[0])], axis=1)

    ms = [stat_tokens(r, j) for j, r in enumerate((m0, m1, m2))]
    ls = [stat_tokens(r, 3 + j) for j, r in enumerate((l0, l1, l2))]
    mx = jnp.maximum(jnp.maximum(ms[0], ms[1]), ms[2])
    ws = [jnp.exp2(m - mx) * l for m, l in zip(ms, ls)]
    den = ws[0] + ws[1] + ws[2]
    head_lane = lax.broadcasted_iota(jnp.int32, den.shape, 1) < n_heads

    coefs = []
    for w in ws:
        a = jnp.where(head_lane, w / den, 0.0)
        hi = a.astype(BF16).astype(F32)
        mid = (a - hi).astype(BF16).astype(F32)
        lo = (a - hi - mid).astype(BF16).astype(F32)
        packed = hi + pltpu.roll(mid, n_heads, axis=1) + pltpu.roll(lo, 2 * n_heads, axis=1)
        coefs.append(jnp.dot(packed.astype(BF16), e_ref[...], preferred_element_type=F32))
    o = None
    for coef, o_ref in zip(coefs, (o0, o1, o2)):
        term = coef * out_tokens(o_ref)
        o = term if o is None else o + term
    return o.astype(BF16)


def _head_indicator(d):
    n_heads = d // HEAD_DIM
    e = np.zeros((LANES, d), np.float32)
    for part in range(3):
        for h in range(n_heads):
            e[part * n_heads + h, h * HEAD_DIM:(h + 1) * HEAD_DIM] = 1.0
    return jnp.asarray(e, BF16)


def _a_out_mlp(parts, wo, h, tm, g2, w1, w2):
    b, s, d = h.shape
    spb = s // tm
    blk = lambda dil, width: pl.BlockSpec((1, dil, tm // dil, width),
                                          lambda t: (t // spb, 0, t % spb, 0))
    os_, ms_, ls_ = zip(*parts)
    dils = [o.shape[1] for o in os_]
    tok = pl.BlockSpec((tm, d), lambda t: (t, 0))
    ins = [*os_, *ms_, *ls_, _head_indicator(d), wo, h.reshape(b * s, d)]
    in_specs = ([blk(dil, d) for dil in dils] + [blk(dil, LANES) for dil in dils] * 2
                + [_resident((LANES, d)), _resident((d, d)), tok])
    scratch = [pltpu.VMEM((d // LANES, tm, LANES), F32), pltpu.VMEM((6, tm, LANES), F32)]
    out = _out_mlp(functools.partial(_a_out_tile, d // HEAD_DIM), ins, in_specs, scratch,
                   b * s, d, tm, g2, w1, w2, "a_out_mlp")
    return out.reshape(b, s, d)


def _mixer_a(h, g1, w_qkv, q_gain, k_gain, w_o, g2, w1, w2):
    d = h.shape[-1]
    parts = []
    for g, dil in enumerate(A_DILATIONS):
        assert A_WINDOWS[g] // (2 * dil) == A_RADIUS
        wt = w_qkv[:, g * 3 * d:(g + 1) * 3 * d].T.astype(BF16)
        parts.append(_a_attn(*_a_proj(h, g1, wt, q_gain[g], k_gain[g], dil)))
    return _a_out_mlp(parts, w_o.astype(BF16), h, 512, g2, w1, w2)


def _b_in_kernel(x_ref, g_ref, w_ref, bg_ref, u_ref):
    d = x_ref.shape[-1]
    y = _rms_rows(x_ref[...], g_ref[...]).astype(BF16)
    bg_ref[...] = jnp.dot(y, w_ref[:, 0:d], preferred_element_type=F32).astype(BF16)
    cg = jnp.dot(y, w_ref[:, d:2 * d], preferred_element_type=F32)
    xt = jnp.dot(y, w_ref[:, 2 * d:3 * d], preferred_element_type=F32)
    u_ref[...] = (cg * xt).astype(BF16)


def _b_out_tile(tiles_per_seq, n_tiles, u_ref, up_ref, un_ref, bg_ref, cw_ref, scr):
    i = jnp.minimum(pl.program_id(0), n_tiles - 1)
    tm = u_ref.shape[0]
    first = (i % tiles_per_seq) == 0
    last = (i % tiles_per_seq) == tiles_per_seq - 1
    hb = up_ref.shape[0]
    scr[0:hb, :] = jnp.where(first, 0.0, up_ref[...].astype(F32))
    scr[hb:hb + tm, :] = u_ref[...].astype(F32)
    scr[hb + tm:2 * hb + tm, :] = jnp.where(last, 0.0, un_ref[...].astype(F32))
    cw = cw_ref[...]
    y = (scr[hb - 1:hb - 1 + tm, :] * cw[0:1] + scr[hb:hb + tm, :] * cw[1:2]
         + scr[hb + 1:hb + 1 + tm, :] * cw[2:3])
    return (bg_ref[...].astype(F32) * y).astype(BF16)


def _mixer_b(h, g1, w_in, conv_w, w_out, g2, w1, w2):
    b, s, d = h.shape
    t = b * s
    tm = 512
    h2 = h.reshape(t, d)
    tok = pl.BlockSpec((tm, d), lambda i: (i, 0))
    bg, u = pl.pallas_call(
        _b_in_kernel,
        grid=(t // tm,),
        in_specs=[tok, _resident((1, d)), _resident((d, 3 * d))],
        out_specs=[tok, tok],
        out_shape=[jax.ShapeDtypeStruct((t, d), BF16)] * 2,
        compiler_params=_cparams(("parallel",)),
        name="b_in",
    )(h2, g1, w_in.astype(BF16))
    hb = BF16_SUBLANES
    per = tm // hb
    in_specs = [
        tok,
        pl.BlockSpec((hb, d), lambda i: (jnp.maximum(i * per - 1, 0), 0)),
        pl.BlockSpec((hb, d), lambda i: (jnp.minimum((i + 1) * per, t // hb - 1), 0)),
        tok, _resident((3, d)), _resident((d, d)), tok,
    ]
    out = _out_mlp(functools.partial(_b_out_tile, s // tm, t // tm),
                   [u, u, u, bg, conv_w, w_out.astype(BF16), h2], in_specs,
                   [pltpu.VMEM((tm + 2 * hb, d), F32)], t, d, tm, g2, w1, w2, "b_out_mlp")
    return out.reshape(b, s, d)


def _axial_rope_T(tn, cr, sr, cc, sc):
    q = HEAD_DIM // 4
    a0, b0 = _rot(tn[:, 0:q], tn[:, q:2 * q], cr[None], sr[None])
    a1, b1 = _rot(tn[:, 2 * q:3 * q], tn[:, 3 * q:], cc[None], sc[None])
    return jnp.concatenate([a0, b0, a1, b1], axis=1)


def _c_proj_kernel(x_ref, g_ref, wq_ref, wk_ref, wv_ref, gq_ref, gk_ref, cr_ref, sr_ref, cc_ref, sc_ref,
                   qT_ref, k_ref, vT_ref):
    y = _rms_rows(x_ref[0], g_ref[...]).astype(BF16)
    tabs = (cr_ref[...], sr_ref[...], cc_ref[...], sc_ref[...])
    qT = lax.dot_general(wq_ref[...], y, NT_DIMS, preferred_element_type=F32)
    kT = lax.dot_general(wk_ref[...], y, NT_DIMS, preferred_element_type=F32)
    vT = lax.dot_general(wv_ref[...], y, NT_DIMS, preferred_element_type=F32).astype(BF16)
    qn = _axial_rope_T(_head_norm_T(qT, gq_ref[...]), *tabs) * (SCORE_SCALE * LOG2E)
    qT_ref[0] = qn.reshape(qT.shape).astype(BF16)
    kn = _axial_rope_T(_head_norm_T(kT, gk_ref[...]), *tabs)
    k_ref[0] = kn.reshape(kT.shape).T.astype(BF16)
    hkv, tm = vT_ref.shape[1], vT.shape[-1]
    ones_row = (lax.broadcasted_iota(jnp.int32, (hkv, BF16_SUBLANES, tm), 1) == 0).astype(BF16)
    vT_ref[0, :, 0] = jnp.concatenate([vT.reshape(hkv, HEAD_DIM, tm), ones_row], axis=1)


def _c_attn_kernel(qT_ref, k_ref, vT_ref, oT_ref, qz_scr, acc_scr, p_scr, s_scr, cm_scr,
                   m_scr, l_scr, sacc_scr):
    grp = qT_ref.shape[1] // HEAD_DIM
    tq = qT_ref.shape[-1]
    nc, tk = vT_ref.shape[2], vT_ref.shape[-1]
    n = grp * tq

    half = lax.broadcasted_iota(jnp.int32, (LANES, tq), 0) // HEAD_DIM
    keep = half == pl.program_id(1) % 2
    for g in range(grp):
        q = qT_ref[0, g * HEAD_DIM:(g + 1) * HEAD_DIM, :]
        q2 = jnp.concatenate([q, q], axis=0)
        qz_scr[:, g * tq:(g + 1) * tq] = jnp.where(keep, q2, jnp.zeros_like(q2))

    def key_chunk(c):
        return k_ref[0, pl.ds(pl.multiple_of(c * tk, tk), tk), :]

    def store_out(o):
        for g in range(grp):
            oT_ref[0, g * HEAD_DIM:(g + 1) * HEAD_DIM, :] = o[:, g * tq:(g + 1) * tq].astype(BF16)

    acc_scr[...] = jnp.zeros_like(acc_scr)

    span = PASS1_KEYS // tk
    nq = n // PASS1_QUERIES

    def keys_of(c0):
        return k_ref[0, pl.ds(pl.multiple_of(c0 * tk, PASS1_KEYS), PASS1_KEYS), :]

    def group_values(c0, slot, j):
        pv = None
        for jc in range(span):
            pv_c = jnp.dot(vT_ref[0, 0, c0 + jc], p_scr[slot, j, jc * tk:(jc + 1) * tk, :],
                           preferred_element_type=F32)
            pv = pv_c if pv is None else pv + pv_c
        return pv

    def plain_group(cg, carry):
        pvs = [None] * nq
        pending = None
        for u in range(PASS1_UNROLL):
            c0 = (PASS1_UNROLL * cg + u) * span
            k = keys_of(c0)
            for j in range(nq):
                cols = slice(j * PASS1_QUERIES, (j + 1) * PASS1_QUERIES)
                sT = jnp.dot(k, qz_scr[:, cols], preferred_element_type=F32)
                if pending is not None:
                    pv_j = group_values(*pending, j)
                    pvs[j] = pv_j if pvs[j] is None else pvs[j] + pv_j
                p_scr[u % 2, j] = jnp.exp2(sT).astype(BF16)
            pending = (c0, u % 2)
        for j in range(nq):
            cols = slice(j * PASS1_QUERIES, (j + 1) * PASS1_QUERIES)
            acc_scr[:, cols] += pvs[j] + group_values(*pending, j)
        return carry

    lax.fori_loop(0, nc // (PASS1_UNROLL * span), plain_group, 0)
    den = acc_scr[HEAD_DIM:HEAD_DIM + 1, :]
    in_range = jnp.logical_and(jnp.max(den) <= 2.0 ** SAFE_LOG2_SCORE,
                               jnp.min(den) >= 2.0 ** -SAFE_LOG2_SCORE)

    @pl.when(in_range)
    def _():
        store_out(acc_scr[:HEAD_DIM, :] / den)

    @pl.when(jnp.logical_not(in_range))
    def _():
        m_scr[...] = jnp.full_like(m_scr, -jnp.inf)
        l_scr[...] = jnp.zeros_like(l_scr)
        sacc_scr[...] = jnp.zeros_like(sacc_scr)

        def scores(c, slot):
            sT = jnp.dot(key_chunk(c), qz_scr[...], preferred_element_type=F32)
            s_scr[slot] = sT
            cm_scr[slot] = jnp.max(sT, axis=0, keepdims=True)

        def absorb(c, slot):
            m_old = m_scr[...]
            m_new = jnp.maximum(m_old, cm_scr[slot])
            p = jnp.exp2(s_scr[slot] - m_new)
            alpha = jnp.exp2(m_old - m_new)
            l_scr[...] = alpha * l_scr[...] + jnp.sum(p, axis=0, keepdims=True)
            sacc_scr[...] = alpha * sacc_scr[...] + jnp.dot(
                vT_ref[0, 0, c, :HEAD_DIM, :], p.astype(BF16), preferred_element_type=F32)
            m_scr[...] = m_new

        scores(0, 0)

        def pair(c2, carry):
            c = 2 * c2
            scores(c + 1, 1)
            absorb(c, 0)
            scores(c + 2, 0)
            absorb(c + 1, 1)
            return carry

        lax.fori_loop(0, nc // 2 - 1, pair, 0)
        scores(nc - 1, 1)
        absorb(nc - 2, 0)
        absorb(nc - 1, 1)
        store_out(sacc_scr[...] / l_scr[...])


def _c_out_tile(oT_ref):
    return oT_ref[0].astype(F32).T.astype(BF16)


def _mixer_c(h, g1, w_qkv, q_gain, k_gain, w_o, g2, w1, w2):
    b, s, d = h.shape
    nq = d
    nk = C_KV_HEADS * HEAD_DIM
    tm = 512
    nc = s // tm
    assert nc % 2 == 0 and nc >= 4 and (nc * tm) % (PASS1_UNROLL * PASS1_KEYS) == 0
    pos = jnp.arange(s)
    cr, sr = _rope_tables_T(pos // GRID_W, HEAD_DIM // 2, C_THETA)
    cc, sc = _rope_tables_T(pos % GRID_W, HEAD_DIM // 2, C_THETA)
    wq = w_qkv[:, :nq].T.astype(BF16)
    wk = w_qkv[:, nq:nq + nk].T.astype(BF16)
    wv = w_qkv[:, nq + nk:].T.astype(BF16)
    gq = jnp.broadcast_to(q_gain[:, None], (HEAD_DIM, tm))
    gk = jnp.broadcast_to(k_gain[:, None], (HEAD_DIM, tm))
    tab = pl.BlockSpec((HEAD_DIM // 4, tm), lambda bi, i: (0, i))
    qT, k, vT = pl.pallas_call(
        _c_proj_kernel,
        grid=(b, nc),
        in_specs=[
            pl.BlockSpec((1, tm, d), lambda bi, i: (bi, i, 0)),
            _resident((1, d)), _resident((nq, d)), _resident((nk, d)), _resident((nk, d)),
            _resident((HEAD_DIM, tm)), _resident((HEAD_DIM, tm)),
            tab, tab, tab, tab,
        ],
        out_specs=[
            pl.BlockSpec((1, nq, tm), lambda bi, i: (bi, 0, i)),
            pl.BlockSpec((1, tm, nk), lambda bi, i: (bi, i, 0)),
            pl.BlockSpec((1, C_KV_HEADS, 1, V_ROWS, tm), lambda bi, i: (bi, 0, i, 0, 0)),
        ],
        out_shape=[
            jax.ShapeDtypeStruct((b, nq, s), BF16),
            jax.ShapeDtypeStruct((b, s, nk), BF16),
            jax.ShapeDtypeStruct((b, C_KV_HEADS, nc, V_ROWS, tm), BF16),
        ],
        compiler_params=_cparams(("parallel", "parallel")),
        name="c_proj",
    )(h, g1, wq, wk, wv, gq, gk, cr, sr, cc, sc)

    grp = nq // nk
    tq = 512
    gw = grp * HEAD_DIM
    oT = pl.pallas_call(
        _c_attn_kernel,
        grid=(b, C_KV_HEADS, s // tq),
        in_specs=[
            pl.BlockSpec((1, gw, tq), lambda bi, hk, i: (bi, hk, i)),
            pl.BlockSpec((1, s, LANES), lambda bi, hk, i: (bi, 0, hk // 2)),
            pl.BlockSpec((1, 1, nc, V_ROWS, tm), lambda bi, hk, i: (bi, hk, 0, 0, 0)),
        ],
        out_specs=pl.BlockSpec((1, gw, tq), lambda bi, hk, i: (bi, hk, i)),
        out_shape=jax.ShapeDtypeStruct((b, nq, s), BF16),
        scratch_shapes=[pltpu.VMEM((LANES, grp * tq), BF16),
                        pltpu.VMEM((V_ROWS, grp * tq), F32),
                        pltpu.VMEM((2, grp * tq // PASS1_QUERIES, PASS1_KEYS, PASS1_QUERIES), BF16),
                        pltpu.VMEM((2, tm, grp * tq), F32), pltpu.VMEM((2, 1, grp * tq), F32),
                        pltpu.VMEM((1, grp * tq), F32), pltpu.VMEM((1, grp * tq), F32),
                        pltpu.VMEM((HEAD_DIM, grp * tq), F32)],
        compiler_params=_cparams(("parallel", "parallel", "parallel")),
        name="c_attn",
    )(qT, k, vT)

    in_specs = [
        pl.BlockSpec((1, d, tm), lambda t: (t // nc, 0, t % nc)),
        _resident((d, d)),
        pl.BlockSpec((tm, d), lambda t: (t, 0)),
    ]
    out = _out_mlp(_c_out_tile, [oT, w_o.astype(BF16), h.reshape(b * s, d)], in_specs, [],
                   b * s, d, tm, g2, w1, w2, "c_out_mlp")
    return out.reshape(b, s, d)


def kernel(x, norm1, norm2, a_wqkv, a_q_gain, a_k_gain, a_wo, b_win, b_conv, b_wout,
           c_wqkv, c_q_gain, c_k_gain, c_wo, mlp_w1, mlp_w2):
    h = x
    for i in range(norm1.shape[0]):
        kind, j = i % N_MIXERS, i // N_MIXERS
        g1 = norm1[i][None, :]
        mlp = (norm2[i][None, :], mlp_w1[i].astype(BF16), mlp_w2[i].astype(BF16))
        if kind == 0:
            h = _mixer_a(h, g1, a_wqkv[j], a_q_gain[j], a_k_gain[j], a_wo[j], *mlp)
        elif kind == 1:
            h = _mixer_b(h, g1, b_win[j], b_conv[j], b_wout[j], *mlp)
        else:
            h = _mixer_c(h, g1, c_wqkv[j], c_q_gain[j], c_k_gain[j], c_wo[j], *mlp)
    return h
```

```python
import functools
import math

import jax
import jax.numpy as jnp
import numpy as np
from jax import lax
from jax.experimental import pallas as pl
from jax.experimental.pallas import tpu as pltpu

HEAD_DIM = 64
EPS = 1e-6
NEG_INF = -1e30
N_MIXERS = 3
A_WINDOWS = (128, 512, 2048)
A_DILATIONS = (1, 4, 16)
A_RADIUS = 64
ROPE_THETA = 500000.0
ROPE_DIMS = HEAD_DIM // 4
C_KV_HEADS = 4
C_THETA = 10000.0
GRID_W = 64
SCORE_SCALE = HEAD_DIM ** -0.5
LOG2E = math.log2(math.e)

LANES = 128
BF16_SUBLANES = 16
MXU_WIDTH = 256

TOKEN_TILE = 512
A_PROJ_TOKENS = 1024
A_ATTN_SUBTILES = 8
C_QUERY_TILE = 512
V_ROWS = HEAD_DIM + BF16_SUBLANES
PAIR_ROWS = 2 * HEAD_DIM + BF16_SUBLANES
SAFE_LOG2_SCORE = 64.0
PASS1_KEYS = 512
PASS1_UNROLL = 16
PASS1_QUERIES = 512
VMEM_LIMIT = 56 * 1024 * 1024

F32 = jnp.float32
BF16 = jnp.bfloat16

NT_DIMS = (((1,), (1,)), ((), ()))


def _cparams(sem):
    return pltpu.CompilerParams(dimension_semantics=sem, vmem_limit_bytes=VMEM_LIMIT)


def _resident(shape):
    nd = len(shape)
    return pl.BlockSpec(shape, lambda *_: (0,) * nd, pipeline_mode=pl.Buffered(1))


def _rms_rows(x, g):
    ms = jnp.mean(x * x, axis=-1, keepdims=True)
    return (x * lax.rsqrt(ms + EPS)) * g


def _head_norm_T(t, gain):
    n = t.shape[-1]
    t3 = t.reshape(t.shape[0] // HEAD_DIM, HEAD_DIM, n)
    ms = jnp.mean(t3 * t3, axis=1, keepdims=True)
    return (t3 * lax.rsqrt(ms + EPS)) * gain[None]


def _rot(a, b, c, s):
    return a * c - b * s, b * c + a * s


def _rope_tables_T(pos, dim, theta):
    inv = theta ** (-jnp.arange(0, dim, 2, dtype=F32) / dim)
    ang = inv[:, None] * pos.astype(F32)[None, :]
    return jnp.cos(ang), jnp.sin(ang)


def _mlp_tile(x, g_ref, w1_ref, w2_ref):
    d = x.shape[-1]
    y = _rms_rows(x, g_ref[...]).astype(BF16)
    acc = x
    for c in range(w1_ref.shape[1] // d):
        cols = slice(c * d, (c + 1) * d)
        a = jnp.maximum(jnp.dot(y, w1_ref[:, cols], preferred_element_type=F32), 0.0)
        acc = acc + jnp.dot((a * a).astype(BF16), w2_ref[cols, :], preferred_element_type=F32)
    return acc


def _out_mlp_kernel(tile_fn, n_in, *refs):
    ins = refs[:n_in]
    g_ref, w1_ref, w2_ref, out_ref, hm_even, hm_odd = refs[n_in:n_in + 6]
    tile_scr = refs[n_in + 6:]
    i = pl.program_id(0)

    @pl.when(i == 0)
    def _():
        hm_odd[...] = jnp.zeros_like(hm_odd)

    *tile_ins, wo_ref, h_ref = ins

    def step(dst, src):
        lhs = tile_fn(*tile_ins, *tile_scr)
        out_ref[...] = _mlp_tile(src[...], g_ref, w1_ref, w2_ref)
        dst[...] = h_ref[...] + jnp.dot(lhs, wo_ref[...], preferred_element_type=F32)

    @pl.when(i % 2 == 0)
    def _():
        step(hm_even, hm_odd)

    @pl.when(i % 2 == 1)
    def _():
        step(hm_odd, hm_even)


def _out_mlp(tile_fn, ins, in_specs, tile_scratch, t, d, tm, g2, w1, w2, name):
    nt = t // tm
    clamp = lambda f: (lambda i: f(jnp.minimum(i, nt - 1)))
    specs = [pl.BlockSpec(sp.block_shape, clamp(sp.index_map), pipeline_mode=sp.pipeline_mode)
             for sp in in_specs]
    return pl.pallas_call(
        functools.partial(_out_mlp_kernel, tile_fn, len(ins)),
        grid=(nt + 1,),
        in_specs=specs + [_resident((1, d)), _resident(w1.shape), _resident(w2.shape)],
        out_specs=pl.BlockSpec((tm, d), lambda i: (jnp.maximum(i - 1, 0), 0)),
        out_shape=jax.ShapeDtypeStruct((t, d), F32),
        scratch_shapes=[pltpu.VMEM((tm, d), F32), pltpu.VMEM((tm, d), F32)] + tile_scratch,
        compiler_params=_cparams(("arbitrary",)),
        name=name,
    )(*ins, g2, w1, w2)


def _a_proj_kernel(dil, *refs):
    n_slab = len(refs) - 9
    x_slabs = refs[:n_slab]
    g_ref, wt_ref, gq_ref, gk_ref, cos_ref, sin_ref, qT_ref, k_ref, vT_ref = refs[n_slab:]
    d = wt_ref.shape[1]
    tl = cos_ref.shape[-1]
    width = gq_ref.shape[-1]
    half = ROPE_DIMS // 2
    if dil == 1:
        units = [[(0, p0, width)] for p0 in range(0, tl, width)]
    else:
        units = [[(r, 0, tl) for r in range(r0, r0 + width // tl)] for r0 in range(0, dil, width // tl)]

    def prep(unit):
        x = jnp.concatenate(
            [jnp.concatenate([xs[0, pl.ds(r + p0 * dil, cnt, stride=dil), :] for xs in x_slabs], axis=1)
             for r, p0, cnt in unit], axis=0)
        y = _rms_rows(x, g_ref[...]).astype(BF16)
        c = jnp.concatenate([cos_ref[r, :, p0:p0 + cnt] for r, p0, cnt in unit], axis=1)
        s = jnp.concatenate([sin_ref[r, :, p0:p0 + cnt] for r, p0, cnt in unit], axis=1)
        return y, c, s

    def qk_finish(t, gain, scale, c, s):
        tn = _head_norm_T(t, gain)
        ra, rb = _rot(tn[:, :half], tn[:, half:ROPE_DIMS], c[None], s[None])
        out = jnp.concatenate([ra, rb, tn[:, ROPE_DIMS:]], axis=1)
        if scale != 1.0:
            out = out * scale
        return out.reshape(t.shape)

    prepped = prep(units[0])
    for i, unit in enumerate(units):
        y, c, s = prepped
        tq, tk, tv = (lax.dot_general(wt_ref[j * d:(j + 1) * d, :], y, NT_DIMS, preferred_element_type=F32)
                      for j in range(3))
        if i + 1 < len(units):
            prepped = prep(units[i + 1])
        qT = qk_finish(tq, gq_ref[...], SCORE_SCALE * LOG2E, c, s).astype(BF16)
        k = qk_finish(tk, gk_ref[...], 1.0, c, s).T.astype(BF16)
        vT = tv.astype(BF16)
        npair = d // (2 * HEAD_DIM)
        ones_row = (lax.broadcasted_iota(jnp.int32, (npair, BF16_SUBLANES, width), 1) == 0).astype(BF16)
        vT = jnp.concatenate([vT.reshape(npair, 2 * HEAD_DIM, width), ones_row], axis=1)
        vT = vT.reshape(npair * PAIR_ROWS, width)
        col = 0
        for r, p0, cnt in unit:
            k_ref[0, r, p0:p0 + cnt] = k[col:col + cnt]
            for t in range(cnt // LANES):
                cols = slice(col + t * LANES, col + (t + 1) * LANES)
                qT_ref[0, r, p0 // LANES + t] = qT[:, cols]
                vT_ref[0, r, p0 // LANES + t] = vT[:, cols]
            col += cnt


def _a_proj(h, g1, wt, q_gain, k_gain, dil):
    b, s, d = h.shape
    l = s // dil
    width = MXU_WIDTH
    tl = max(LANES, A_PROJ_TOKENS // dil)
    tt = tl * dil
    pos = (jnp.arange(l)[None, :] * dil + jnp.arange(dil)[:, None]).reshape(-1)
    cosT, sinT = _rope_tables_T(pos, ROPE_DIMS, ROPE_THETA)
    cosT = cosT.reshape(-1, dil, l).transpose(1, 0, 2)
    sinT = sinT.reshape(-1, dil, l).transpose(1, 0, 2)
    gq = jnp.broadcast_to(q_gain[:, None], (HEAD_DIM, width))
    gk = jnp.broadcast_to(k_gain[:, None], (HEAD_DIM, width))
    n_slab = d // LANES
    slab = lambda c: pl.BlockSpec((1, tt, LANES), lambda bi, i: (bi, i, c))
    vd = d // (2 * HEAD_DIM) * PAIR_ROWS
    feat = lambda rows: pl.BlockSpec((1, dil, tl // LANES, rows, LANES), lambda bi, i: (bi, 0, i, 0, 0))
    tab = pl.BlockSpec((dil, ROPE_DIMS // 2, tl), lambda bi, i: (0, 0, i))
    feat_shape = lambda rows: jax.ShapeDtypeStruct((b, dil, l // LANES, rows, LANES), BF16)
    return pl.pallas_call(
        functools.partial(_a_proj_kernel, dil),
        grid=(b, s // tt),
        in_specs=[slab(c) for c in range(n_slab)] + [
            _resident((1, d)), _resident((3 * d, d)),
            _resident((HEAD_DIM, width)), _resident((HEAD_DIM, width)),
            tab, tab,
        ],
        out_specs=[feat(d), pl.BlockSpec((1, dil, tl, d), lambda bi, i: (bi, 0, i, 0)), feat(vd)],
        out_shape=[feat_shape(d), jax.ShapeDtypeStruct((b, dil, l, d), BF16), feat_shape(vd)],
        compiler_params=_cparams(("parallel", "parallel")),
        name="a_proj",
    )(*([h] * n_slab), g1, wt, gq, gk, cosT, sinT)


def _a_attn_kernel(qT_ref, kprev_ref, kmain_ref, knext_ref, vprev_ref, vmain_ref, vnext_ref,
                   bias_ref, cap_ref, o_ref, m_ref, l_ref, oT_scr, st_scr):
    i = pl.program_id(1)
    tb = qT_ref.shape[1]
    d = qT_ref.shape[2]
    n_heads = d // HEAD_DIM
    quad = 4 * HEAD_DIM
    win = 2 * LANES
    first_sel = jnp.where(i == 0, 0, 1)
    last_sel = jnp.where(i == pl.num_programs(1) - 1, 2, 1)
    own_head = (lax.broadcasted_iota(jnp.int32, (quad, 4 * LANES), 0) // HEAD_DIM
                == lax.broadcasted_iota(jnp.int32, (quad, 4 * LANES), 1) // LANES)

    def window(s):
        lo, hi = s * LANES - A_RADIUS, s * LANES - A_RADIUS + win
        if s == 0:
            kw = jnp.concatenate([kprev_ref[0], kmain_ref[0, 0:hi]], axis=0)
            sel = first_sel
        elif s == tb - 1:
            kw = jnp.concatenate([kmain_ref[0, lo:tb * LANES], knext_ref[0]], axis=0)
            sel = last_sel
        else:
            kw = kmain_ref[0, lo:hi]
            sel = 1
        va = vprev_ref[0, 0] if s == 0 else vmain_ref[0, s - 1]
        vc = vnext_ref[0, 0] if s == tb - 1 else vmain_ref[0, s + 1]
        vwin = jnp.concatenate([va[:, LANES - A_RADIUS:], vmain_ref[0, s], vc[:, :A_RADIUS]], axis=1)
        return kw, sel, vwin

    def quad_scores(s, u, kw):
        q4 = qT_ref[0, s, u * quad:(u + 1) * quad, :]
        q4 = jnp.concatenate([q4] * 4, axis=1)
        qz = jnp.where(own_head, q4, jnp.zeros_like(q4))
        return jnp.dot(kw[:, u * quad:(u + 1) * quad], qz, preferred_element_type=F32)

    def pair_values(vcat, pair, p, pr):
        c0 = 2 * pr * LANES
        return jnp.dot(vcat[pair * PAIR_ROWS:(pair + 1) * PAIR_ROWS, :], p[:, c0:c0 + 2 * LANES],
                       preferred_element_type=F32)

    def finish(s):
        rows = slice(s * LANES, (s + 1) * LANES)
        o_ref[0, rows, :] = oT_scr[...].T.astype(BF16)
        m_ref[0, rows, :] = st_scr[0].T
        l_ref[0, rows, :] = st_scr[1].T

    def init_stats():
        st_scr[...] = jnp.zeros_like(st_scr)
        st_scr[1, n_heads:, :] = jnp.ones((LANES - n_heads, LANES), F32)

    init_stats()
    dens = []

    def quad_values(s, u, vcat, p):
        for pr in range(2):
            h0 = 4 * u + 2 * pr
            out = pair_values(vcat, h0 // 2, p, pr)
            for hh in range(2):
                l = out[2 * HEAD_DIM:2 * HEAD_DIM + 1, hh * LANES:(hh + 1) * LANES]
                oT_scr[(h0 + hh) * HEAD_DIM:(h0 + hh + 1) * HEAD_DIM, :] = (
                    out[hh * HEAD_DIM:(hh + 1) * HEAD_DIM, hh * LANES:(hh + 1) * LANES] / l)
                st_scr[1, h0 + hh:h0 + hh + 1, :] = l
                dens.append(l)
        if u == n_heads // 4 - 1:
            finish(s)

    pending = None
    for s in range(tb):
        kw, sel, vcat = window(s)
        cap = cap_ref[sel]
        for u in range(n_heads // 4):
            sT = quad_scores(s, u, kw)
            if pending is not None:
                quad_values(*pending)
            pending = (s, u, vcat, jnp.minimum(jnp.exp2(sT).astype(BF16), cap))
    quad_values(*pending)
    lmin, lmax = dens[0], dens[0]
    for l in dens[1:]:
        lmin, lmax = jnp.minimum(lmin, l), jnp.maximum(lmax, l)
    in_range = jnp.logical_and(jnp.max(lmax) <= 2.0 ** SAFE_LOG2_SCORE,
                               jnp.min(lmin) >= 2.0 ** -SAFE_LOG2_SCORE)

    @pl.when(jnp.logical_not(in_range))
    def _():
        init_stats()
        for s in range(tb):
            kw, sel, vcat = window(s)
            bias = bias_ref[sel]
            for u in range(n_heads // 4):
                sT = quad_scores(s, u, kw) + bias
                m = jnp.max(sT, axis=0, keepdims=True)
                p = jnp.exp2(sT - m)
                l = jnp.sum(p, axis=0, keepdims=True)
                for pr in range(2):
                    h0 = 4 * u + 2 * pr
                    out = pair_values(vcat, h0 // 2, p.astype(BF16), pr)
                    for hh in range(2):
                        cols = slice((2 * pr + hh) * LANES, (2 * pr + hh + 1) * LANES)
                        oT_scr[(h0 + hh) * HEAD_DIM:(h0 + hh + 1) * HEAD_DIM, :] = (
                            out[hh * HEAD_DIM:(hh + 1) * HEAD_DIM, hh * LANES:(hh + 1) * LANES] / l[:, cols])
                for g in range(4):
                    st_scr[0, 4 * u + g:4 * u + g + 1, :] = m[:, g * LANES:(g + 1) * LANES]
                    st_scr[1, 4 * u + g:4 * u + g + 1, :] = l[:, g * LANES:(g + 1) * LANES]
            finish(s)


def _band_masks():
    kk = np.arange(2 * LANES)[:, None]
    j = np.arange(LANES)[None, :]
    band = (kk >= j) & (kk <= j + 2 * A_RADIUS)
    variants = np.stack([band & (kk >= A_RADIUS), band, band & (kk < 2 * LANES - A_RADIUS)])
    variants = np.tile(variants, (1, 1, 4))
    bias = jnp.asarray(np.where(variants, 0.0, NEG_INF), F32)
    cap = jnp.asarray(np.where(variants, float(jnp.finfo(BF16).max), 0.0), BF16)
    return bias, cap


def _a_attn(qT, k, vT):
    b, dil, nt, d, _ = qT.shape
    vd = vT.shape[3]
    n, l = b * dil, nt * LANES
    qT, k, vT = qT.reshape(n, nt, d, LANES), k.reshape(n, l, d), vT.reshape(n, nt, vd, LANES)
    tb = min(A_ATTN_SUBTILES, nt)
    assert tb >= 2 and nt % tb == 0
    hb = A_RADIUS
    per = tb * LANES // hb
    tok = lambda width: pl.BlockSpec((1, tb * LANES, width), lambda ni, i: (ni, i, 0))
    feat = lambda rows: pl.BlockSpec((1, tb, rows, LANES), lambda ni, i: (ni, i, 0, 0))
    o, m, lsum = pl.pallas_call(
        _a_attn_kernel,
        grid=(n, nt // tb),
        in_specs=[
            feat(d),
            pl.BlockSpec((1, hb, d), lambda ni, i: (ni, jnp.maximum(i * per - 1, 0), 0)),
            tok(d),
            pl.BlockSpec((1, hb, d), lambda ni, i: (ni, jnp.minimum((i + 1) * per, l // hb - 1), 0)),
            pl.BlockSpec((1, 1, vd, LANES), lambda ni, i: (ni, jnp.maximum(i * tb - 1, 0), 0, 0)),
            feat(vd),
            pl.BlockSpec((1, 1, vd, LANES), lambda ni, i: (ni, jnp.minimum((i + 1) * tb, nt - 1), 0, 0)),
            _resident((3, 2 * LANES, 4 * LANES)),
            _resident((3, 2 * LANES, 4 * LANES)),
        ],
        out_specs=[tok(d), tok(LANES), tok(LANES)],
        out_shape=[
            jax.ShapeDtypeStruct((n, l, d), BF16),
            jax.ShapeDtypeStruct((n, l, LANES), F32),
            jax.ShapeDtypeStruct((n, l, LANES), F32),
        ],
        scratch_shapes=[pltpu.VMEM((d, LANES), F32), pltpu.VMEM((2, LANES, LANES), F32)],
        compiler_params=_cparams(("parallel", "parallel")),
        name="a_attn",
    )(qT, k, k, k, vT, vT, vT, *_band_masks())
    return (o.reshape(b, dil, l, d), m.reshape(b, dil, l, LANES), lsum.reshape(b, dil, l, LANES))


def _a_out_tile(n_heads, o0, o1, o2, m0, m1, m2, l0, l1, l2, e_ref, o_scr, st_scr):
    tm = o_scr.shape[1]

    def stat_tokens(ref, slot):
        dil = ref.shape[1]
        if dil == 1:
            return ref[0, 0]
        for r in range(dil):
            st_scr[slot, pl.ds(r, tm // dil, stride=dil), :] = ref[0, r]
        return st_scr[slot]

    def out_tokens(ref):
        dil = ref.shape[1]
        if dil == 1:
            return ref[0, 0].astype(F32)
        for r in range(dil):
            blk = ref[0, r].astype(F32)
            for c in range(o_scr.shape[0]):
                o_scr[c, pl.ds(r, tm // dil, stride=dil), :] = blk[:, c * LANES:(c + 1) * LANES]
        return jnp.concatenate([o_scr[c] for c in range(o_scr.shape[0])], axis=1)

    ms = [stat_tokens(r, j) for j, r in enumerate((m0, m1, m2))]
    ls = [stat_tokens(r, 3 + j) for j, r in enumerate((l0, l1, l2))]
    mx = jnp.maximum(jnp.maximum(ms[0], ms[1]), ms[2])
    ws = [jnp.exp2(m - mx) * l for m, l in zip(ms, ls)]
    den = ws[0] + ws[1] + ws[2]
    head_lane = lax.broadcasted_iota(jnp.int32, den.shape, 1) < n_heads

    coefs = []
    for w in ws:
        a = jnp.where(head_lane, w / den, 0.0)
        hi = a.astype(BF16).astype(F32)
        mid = (a - hi).astype(BF16).astype(F32)
        lo = (a - hi - mid).astype(BF16).astype(F32)
        packed = hi + pltpu.roll(mid, n_heads, axis=1) + pltpu.roll(lo, 2 * n_heads, axis=1)
        coefs.append(jnp.dot(packed.astype(BF16), e_ref[...], preferred_element_type=F32))
    o = None
    for coef, o_ref in zip(coefs, (o0, o1, o2)):
        term = coef * out_tokens(o_ref)
        o = term if o is None else o + term
    return o.astype(BF16)


def _head_indicator(d):
    n_heads = d // HEAD_DIM
    e = np.zeros((LANES, d), np.float32)
    for part in range(3):
        for h in range(n_heads):
            e[part * n_heads + h, h * HEAD_DIM:(h + 1) * HEAD_DIM] = 1.0
    return jnp.asarray(e, BF16)


def _a_out_mlp(parts, wo, h, tm, g2, w1, w2):
    b, s, d = h.shape
    spb = s // tm
    blk = lambda dil, width: pl.BlockSpec((1, dil, tm // dil, width),
                                          lambda t: (t // spb, 0, t % spb, 0))
    os_, ms_, ls_ = zip(*parts)
    dils = [o.shape[1] for o in os_]
    tok = pl.BlockSpec((tm, d), lambda t: (t, 0))
    ins = [*os_, *ms_, *ls_, _head_indicator(d), wo, h.reshape(b * s, d)]
    in_specs = ([blk(dil, d) for dil in dils] + [blk(dil, LANES) for dil in dils] * 2
                + [_resident((LANES, d)), _resident((d, d)), tok])
    scratch = [pltpu.VMEM((d // LANES, tm, LANES), F32), pltpu.VMEM((6, tm, LANES), F32)]
    out = _out_mlp(functools.partial(_a_out_tile, d // HEAD_DIM), ins, in_specs, scratch,
                   b * s, d, tm, g2, w1, w2, "a_out_mlp")
    return out.reshape(b, s, d)


def _mixer_a(h, g1, w_qkv, q_gain, k_gain, w_o, g2, w1, w2):
    d = h.shape[-1]
    parts = []
    for g, dil in enumerate(A_DILATIONS):
        assert A_WINDOWS[g] // (2 * dil) == A_RADIUS
        wt = w_qkv[:, g * 3 * d:(g + 1) * 3 * d].T.astype(BF16)
        parts.append(_a_attn(*_a_proj(h, g1, wt, q_gain[g], k_gain[g], dil)))
    return _a_out_mlp(parts, w_o.astype(BF16), h, TOKEN_TILE, g2, w1, w2)


def _b_in_kernel(x_ref, g_ref, w_ref, bg_ref, u_ref):
    d = x_ref.shape[-1]
    y = _rms_rows(x_ref[...], g_ref[...]).astype(BF16)
    bg_ref[...] = jnp.dot(y, w_ref[:, 0:d], preferred_element_type=F32).astype(BF16)
    cg = jnp.dot(y, w_ref[:, d:2 * d], preferred_element_type=F32)
    xt = jnp.dot(y, w_ref[:, 2 * d:3 * d], preferred_element_type=F32)
    u_ref[...] = (cg * xt).astype(BF16)


def _b_out_tile(tiles_per_seq, n_tiles, u_ref, up_ref, un_ref, bg_ref, cw_ref, scr):
    i = jnp.minimum(pl.program_id(0), n_tiles - 1)
    tm = u_ref.shape[0]
    first = (i % tiles_per_seq) == 0
    last = (i % tiles_per_seq) == tiles_per_seq - 1
    hb = up_ref.shape[0]
    scr[0:hb, :] = jnp.where(first, 0.0, up_ref[...].astype(F32))
    scr[hb:hb + tm, :] = u_ref[...].astype(F32)
    scr[hb + tm:2 * hb + tm, :] = jnp.where(last, 0.0, un_ref[...].astype(F32))
    cw = cw_ref[...]
    y = (scr[hb - 1:hb - 1 + tm, :] * cw[0:1] + scr[hb:hb + tm, :] * cw[1:2]
         + scr[hb + 1:hb + 1 + tm, :] * cw[2:3])
    return (bg_ref[...].astype(F32) * y).astype(BF16)


def _mixer_b(h, g1, w_in, conv_w, w_out, g2, w1, w2):
    b, s, d = h.shape
    t = b * s
    tm = TOKEN_TILE
    h2 = h.reshape(t, d)
    tok = pl.BlockSpec((tm, d), lambda i: (i, 0))
    bg, u = pl.pallas_call(
        _b_in_kernel,
        grid=(t // tm,),
        in_specs=[tok, _resident((1, d)), _resident((d, 3 * d))],
        out_specs=[tok, tok],
        out_shape=[jax.ShapeDtypeStruct((t, d), BF16)] * 2,
        compiler_params=_cparams(("parallel",)),
        name="b_in",
    )(h2, g1, w_in.astype(BF16))
    hb = BF16_SUBLANES
    per = tm // hb
    in_specs = [
        tok,
        pl.BlockSpec((hb, d), lambda i: (jnp.maximum(i * per - 1, 0), 0)),
        pl.BlockSpec((hb, d), lambda i: (jnp.minimum((i + 1) * per, t // hb - 1), 0)),
        tok, _resident((3, d)), _resident((d, d)), tok,
    ]
    out = _out_mlp(functools.partial(_b_out_tile, s // tm, t // tm),
                   [u, u, u, bg, conv_w, w_out.astype(BF16), h2], in_specs,
                   [pltpu.VMEM((tm + 2 * hb, d), F32)], t, d, tm, g2, w1, w2, "b_out_mlp")
    return out.reshape(b, s, d)


def _axial_rope_T(tn, cr, sr, cc, sc):
    q = HEAD_DIM // 4
    a0, b0 = _rot(tn[:, 0:q], tn[:, q:2 * q], cr[None], sr[None])
    a1, b1 = _rot(tn[:, 2 * q:3 * q], tn[:, 3 * q:], cc[None], sc[None])
    return jnp.concatenate([a0, b0, a1, b1], axis=1)


def _c_proj_kernel(x_ref, g_ref, wq_ref, wk_ref, wv_ref, gq_ref, gk_ref, cr_ref, sr_ref, cc_ref, sc_ref,
                   qT_ref, k_ref, vT_ref):
    y = _rms_rows(x_ref[0], g_ref[...]).astype(BF16)
    tabs = (cr_ref[...], sr_ref[...], cc_ref[...], sc_ref[...])
    qT = lax.dot_general(wq_ref[...], y, NT_DIMS, preferred_element_type=F32)
    kT = lax.dot_general(wk_ref[...], y, NT_DIMS, preferred_element_type=F32)
    vT = lax.dot_general(wv_ref[...], y, NT_DIMS, preferred_element_type=F32).astype(BF16)
    qn = _axial_rope_T(_head_norm_T(qT, gq_ref[...]), *tabs) * (SCORE_SCALE * LOG2E)
    qT_ref[0] = qn.reshape(qT.shape).astype(BF16)
    kn = _axial_rope_T(_head_norm_T(kT, gk_ref[...]), *tabs)
    k_ref[0] = kn.reshape(kT.shape).T.astype(BF16)
    hkv, tm = vT_ref.shape[1], vT.shape[-1]
    ones_row = (lax.broadcasted_iota(jnp.int32, (hkv, BF16_SUBLANES, tm), 1) == 0).astype(BF16)
    vT_ref[0, :, 0] = jnp.concatenate([vT.reshape(hkv, HEAD_DIM, tm), ones_row], axis=1)


def _c_attn_kernel(qT_ref, k_ref, vT_ref, oT_ref, qz_scr, acc_scr, p_scr, s_scr, cm_scr,
                   m_scr, l_scr, sacc_scr):
    grp = qT_ref.shape[1] // HEAD_DIM
    tq = qT_ref.shape[-1]
    nc, tk = vT_ref.shape[2], vT_ref.shape[-1]
    n = grp * tq

    half = lax.broadcasted_iota(jnp.int32, (LANES, tq), 0) // HEAD_DIM
    keep = half == pl.program_id(1) % 2
    for g in range(grp):
        q = qT_ref[0, g * HEAD_DIM:(g + 1) * HEAD_DIM, :]
        q2 = jnp.concatenate([q, q], axis=0)
        qz_scr[:, g * tq:(g + 1) * tq] = jnp.where(keep, q2, jnp.zeros_like(q2))

    def key_chunk(c):
        return k_ref[0, pl.ds(pl.multiple_of(c * tk, tk), tk), :]

    def store_out(o):
        for g in range(grp):
            oT_ref[0, g * HEAD_DIM:(g + 1) * HEAD_DIM, :] = o[:, g * tq:(g + 1) * tq].astype(BF16)

    acc_scr[...] = jnp.zeros_like(acc_scr)

    span = PASS1_KEYS // tk
    nq = n // PASS1_QUERIES

    def keys_of(c0):
        return k_ref[0, pl.ds(pl.multiple_of(c0 * tk, PASS1_KEYS), PASS1_KEYS), :]

    def group_values(c0, slot, j):
        pv = None
        for jc in range(span):
            pv_c = jnp.dot(vT_ref[0, 0, c0 + jc], p_scr[slot, j, jc * tk:(jc + 1) * tk, :],
                           preferred_element_type=F32)
            pv = pv_c if pv is None else pv + pv_c
        return pv

    def plain_group(cg, carry):
        pvs = [None] * nq
        pending = None
        for u in range(PASS1_UNROLL):
            c0 = (PASS1_UNROLL * cg + u) * span
            k = keys_of(c0)
            for j in range(nq):
                cols = slice(j * PASS1_QUERIES, (j + 1) * PASS1_QUERIES)
                sT = jnp.dot(k, qz_scr[:, cols], preferred_element_type=F32)
                if pending is not None:
                    pv_j = group_values(*pending, j)
                    pvs[j] = pv_j if pvs[j] is None else pvs[j] + pv_j
                p_scr[u % 2, j] = jnp.exp2(sT).astype(BF16)
            pending = (c0, u % 2)
        for j in range(nq):
            cols = slice(j * PASS1_QUERIES, (j + 1) * PASS1_QUERIES)
            acc_scr[:, cols] += pvs[j] + group_values(*pending, j)
        return carry

    lax.fori_loop(0, nc // (PASS1_UNROLL * span), plain_group, 0)
    den = acc_scr[HEAD_DIM:HEAD_DIM + 1, :]
    in_range = jnp.logical_and(jnp.max(den) <= 2.0 ** SAFE_LOG2_SCORE,
                               jnp.min(den) >= 2.0 ** -SAFE_LOG2_SCORE)

    @pl.when(in_range)
    def _():
        store_out(acc_scr[:HEAD_DIM, :] / den)

    @pl.when(jnp.logical_not(in_range))
    def _():
        m_scr[...] = jnp.full_like(m_scr, -jnp.inf)
        l_scr[...] = jnp.zeros_like(l_scr)
        sacc_scr[...] = jnp.zeros_like(sacc_scr)

        def scores(c, slot):
            sT = jnp.dot(key_chunk(c), qz_scr[...], preferred_element_type=F32)
            s_scr[slot] = sT
            cm_scr[slot] = jnp.max(sT, axis=0, keepdims=True)

        def absorb(c, slot):
            m_old = m_scr[...]
            m_new = jnp.maximum(m_old, cm_scr[slot])
            p = jnp.exp2(s_scr[slot] - m_new)
            alpha = jnp.exp2(m_old - m_new)
            l_scr[...] = alpha * l_scr[...] + jnp.sum(p, axis=0, keepdims=True)
            sacc_scr[...] = alpha * sacc_scr[...] + jnp.dot(
                vT_ref[0, 0, c, :HEAD_DIM, :], p.astype(BF16), preferred_element_type=F32)
            m_scr[...] = m_new

        scores(0, 0)

        def pair(c2, carry):
            c = 2 * c2
            scores(c + 1, 1)
            absorb(c, 0)
            scores(c + 2, 0)
            absorb(c + 1, 1)
            return carry

        lax.fori_loop(0, nc // 2 - 1, pair, 0)
        scores(nc - 1, 1)
        absorb(nc - 2, 0)
        absorb(nc - 1, 1)
        store_out(sacc_scr[...] / l_scr[...])


def _c_out_tile(oT_ref):
    return oT_ref[0].astype(F32).T.astype(BF16)


def _mixer_c(h, g1, w_qkv, q_gain, k_gain, w_o, g2, w1, w2):
    b, s, d = h.shape
    nq = d
    nk = C_KV_HEADS * HEAD_DIM
    tm = TOKEN_TILE
    nc = s // tm
    assert nc % 2 == 0 and nc >= 4 and (nc * tm) % (PASS1_UNROLL * PASS1_KEYS) == 0
    pos = jnp.arange(s)
    cr, sr = _rope_tables_T(pos // GRID_W, HEAD_DIM // 2, C_THETA)
    cc, sc = _rope_tables_T(pos % GRID_W, HEAD_DIM // 2, C_THETA)
    wq = w_qkv[:, :nq].T.astype(BF16)
    wk = w_qkv[:, nq:nq + nk].T.astype(BF16)
    wv = w_qkv[:, nq + nk:].T.astype(BF16)
    gq = jnp.broadcast_to(q_gain[:, None], (HEAD_DIM, tm))
    gk = jnp.broadcast_to(k_gain[:, None], (HEAD_DIM, tm))
    tab = pl.BlockSpec((HEAD_DIM // 4, tm), lambda bi, i: (0, i))
    qT, k, vT = pl.pallas_call(
        _c_proj_kernel,
        grid=(b, nc),
        in_specs=[
            pl.BlockSpec((1, tm, d), lambda bi, i: (bi, i, 0)),
            _resident((1, d)), _resident((nq, d)), _resident((nk, d)), _resident((nk, d)),
            _resident((HEAD_DIM, tm)), _resident((HEAD_DIM, tm)),
            tab, tab, tab, tab,
        ],
        out_specs=[
            pl.BlockSpec((1, nq, tm), lambda bi, i: (bi, 0, i)),
            pl.BlockSpec((1, tm, nk), lambda bi, i: (bi, i, 0)),
            pl.BlockSpec((1, C_KV_HEADS, 1, V_ROWS, tm), lambda bi, i: (bi, 0, i, 0, 0)),
        ],
        out_shape=[
            jax.ShapeDtypeStruct((b, nq, s), BF16),
            jax.ShapeDtypeStruct((b, s, nk), BF16),
            jax.ShapeDtypeStruct((b, C_KV_HEADS, nc, V_ROWS, tm), BF16),
        ],
        compiler_params=_cparams(("parallel", "parallel")),
        name="c_proj",
    )(h, g1, wq, wk, wv, gq, gk, cr, sr, cc, sc)

    grp = nq // nk
    tq = C_QUERY_TILE
    gw = grp * HEAD_DIM
    oT = pl.pallas_call(
        _c_attn_kernel,
        grid=(b, C_KV_HEADS, s // tq),
        in_specs=[
            pl.BlockSpec((1, gw, tq), lambda bi, hk, i: (bi, hk, i)),
            pl.BlockSpec((1, s, LANES), lambda bi, hk, i: (bi, 0, hk // 2)),
            pl.BlockSpec((1, 1, nc, V_ROWS, tm), lambda bi, hk, i: (bi, hk, 0, 0, 0)),
        ],
        out_specs=pl.BlockSpec((1, gw, tq), lambda bi, hk, i: (bi, hk, i)),
        out_shape=jax.ShapeDtypeStruct((b, nq, s), BF16),
        scratch_shapes=[pltpu.VMEM((LANES, grp * tq), BF16),
                        pltpu.VMEM((V_ROWS, grp * tq), F32),
                        pltpu.VMEM((2, grp * tq // PASS1_QUERIES, PASS1_KEYS, PASS1_QUERIES), BF16),
                        pltpu.VMEM((2, tm, grp * tq), F32), pltpu.VMEM((2, 1, grp * tq), F32),
                        pltpu.VMEM((1, grp * tq), F32), pltpu.VMEM((1, grp * tq), F32),
                        pltpu.VMEM((HEAD_DIM, grp * tq), F32)],
        compiler_params=_cparams(("parallel", "parallel", "parallel")),
        name="c_attn",
    )(qT, k, vT)

    in_specs = [
        pl.BlockSpec((1, d, tm), lambda t: (t // nc, 0, t % nc)),
        _resident((d, d)),
        pl.BlockSpec((tm, d), lambda t: (t, 0)),
    ]
    out = _out_mlp(_c_out_tile, [oT, w_o.astype(BF16), h.reshape(b * s, d)], in_specs, [],
                   b * s, d, tm, g2, w1, w2, "c_out_mlp")
    return out.reshape(b, s, d)


def kernel(x, norm1, norm2, a_wqkv, a_q_gain, a_k_gain, a_wo, b_win, b_conv, b_wout,
           c_wqkv, c_q_gain, c_k_gain, c_wo, mlp_w1, mlp_w2):
    h = x
    for i in range(norm1.shape[0]):
        kind, j = i % N_MIXERS, i // N_MIXERS
        g1 = norm1[i][None, :]
        mlp = (norm2[i][None, :], mlp_w1[i].astype(BF16), mlp_w2[i].astype(BF16))
        if kind == 0:
            h = _mixer_a(h, g1, a_wqkv[j], a_q_gain[j], a_k_gain[j], a_wo[j], *mlp)
        elif kind == 1:
            h = _mixer_b(h, g1, b_win[j], b_conv[j], b_wout[j], *mlp)
        else:
            h = _mixer_c(h, g1, c_wqkv[j], c_q_gain[j], c_k_gain[j], c_wo[j], *mlp)
    return h
```

```python
import functools
import math

import jax
import jax.numpy as jnp
import numpy as np
from jax import lax
from jax.experimental import pallas as pl
from jax.experimental.pallas import tpu as pltpu

HEAD_DIM = 64
EPS = 1e-6
NEG_INF = -1e30
N_MIXERS = 3
A_WINDOWS = (128, 512, 2048)
A_DILATIONS = (1, 4, 16)
A_RADIUS = 64
ROPE_THETA = 500000.0
ROPE_DIMS = HEAD_DIM // 4
C_KV_HEADS = 4
C_THETA = 10000.0
GRID_W = 64
SCORE_SCALE = HEAD_DIM ** -0.5
LOG2E = math.log2(math.e)

LANES = 128
BF16_SUBLANES = 16
MXU_WIDTH = 256
INTERLEAVE_STEP = 4

TOKEN_TILE = 512
A_PROJ_TOKENS = 1024
A_ATTN_SUBTILES = 8
C_QUERY_TILE = 512
V_ROWS = HEAD_DIM + BF16_SUBLANES
PAIR_ROWS = 2 * HEAD_DIM + BF16_SUBLANES
SAFE_LOG2_SCORE = 64.0
PASS1_KEYS = 512
PASS1_UNROLL = 16
PASS1_QUERIES = 512
VMEM_LIMIT = 56 * 1024 * 1024

F32 = jnp.float32
BF16 = jnp.bfloat16

NT_DIMS = (((1,), (1,)), ((), ()))


def _cparams(sem):
    return pltpu.CompilerParams(dimension_semantics=sem, vmem_limit_bytes=VMEM_LIMIT)


def _resident(shape):
    nd = len(shape)
    return pl.BlockSpec(shape, lambda *_: (0,) * nd, pipeline_mode=pl.Buffered(1))


def _rms_rows(x, g):
    ms = jnp.mean(x * x, axis=-1, keepdims=True)
    return (x * lax.rsqrt(ms + EPS)) * g


def _head_norm_T(t, gain):
    n = t.shape[-1]
    t3 = t.reshape(t.shape[0] // HEAD_DIM, HEAD_DIM, n)
    ms = jnp.mean(t3 * t3, axis=1, keepdims=True)
    return (t3 * lax.rsqrt(ms + EPS)) * gain[None]


def _rot(a, b, c, s):
    return a * c - b * s, b * c + a * s


def _rope_tables_T(pos, dim, theta):
    inv = theta ** (-jnp.arange(0, dim, 2, dtype=F32) / dim)
    ang = inv[:, None] * pos.astype(F32)[None, :]
    return jnp.cos(ang), jnp.sin(ang)


def _mlp_tile(x, g_ref, w1_ref, w2_ref):
    d = x.shape[-1]
    y = _rms_rows(x, g_ref[...]).astype(BF16)
    acc = x
    for c in range(w1_ref.shape[1] // d):
        cols = slice(c * d, (c + 1) * d)
        a = jnp.maximum(jnp.dot(y, w1_ref[:, cols], preferred_element_type=F32), 0.0)
        acc = acc + jnp.dot((a * a).astype(BF16), w2_ref[cols, :], preferred_element_type=F32)
    return acc


def _out_mlp_kernel(tile_fn, n_in, *refs):
    ins = refs[:n_in]
    g_ref, w1_ref, w2_ref, out_ref, hm_even, hm_odd = refs[n_in:n_in + 6]
    tile_scr = refs[n_in + 6:]
    i = pl.program_id(0)

    @pl.when(i == 0)
    def _():
        hm_odd[...] = jnp.zeros_like(hm_odd)

    *tile_ins, wo_ref, h_ref = ins

    def step(dst, src):
        lhs = tile_fn(*tile_ins, *tile_scr)
        out_ref[...] = _mlp_tile(src[...], g_ref, w1_ref, w2_ref)
        dst[...] = h_ref[...] + jnp.dot(lhs, wo_ref[...], preferred_element_type=F32)

    @pl.when(i % 2 == 0)
    def _():
        step(hm_even, hm_odd)

    @pl.when(i % 2 == 1)
    def _():
        step(hm_odd, hm_even)


def _out_mlp(tile_fn, ins, in_specs, tile_scratch, t, d, tm, g2, w1, w2, name):
    nt = t // tm
    clamp = lambda f: (lambda i: f(jnp.minimum(i, nt - 1)))
    specs = [pl.BlockSpec(sp.block_shape, clamp(sp.index_map), pipeline_mode=sp.pipeline_mode)
             for sp in in_specs]
    return pl.pallas_call(
        functools.partial(_out_mlp_kernel, tile_fn, len(ins)),
        grid=(nt + 1,),
        in_specs=specs + [_resident((1, d)), _resident(w1.shape), _resident(w2.shape)],
        out_specs=pl.BlockSpec((tm, d), lambda i: (jnp.maximum(i - 1, 0), 0)),
        out_shape=jax.ShapeDtypeStruct((t, d), F32),
        scratch_shapes=[pltpu.VMEM((tm, d), F32), pltpu.VMEM((tm, d), F32)] + tile_scratch,
        compiler_params=_cparams(("arbitrary",)),
        name=name,
    )(*ins, g2, w1, w2)


def _a_proj_kernel(dil, *refs):
    n_slab = len(refs) - 9
    x_slabs = refs[:n_slab]
    g_ref, wt_ref, gq_ref, gk_ref, cos_ref, sin_ref, qT_ref, k_ref, vT_ref = refs[n_slab:]
    d = wt_ref.shape[1]
    tl = cos_ref.shape[-1]
    width = gq_ref.shape[-1]
    half = ROPE_DIMS // 2
    if dil == 1:
        units = [[(0, p0, width)] for p0 in range(0, tl, width)]
    else:
        units = [[(r, 0, tl) for r in range(r0, r0 + width // tl)] for r0 in range(0, dil, width // tl)]

    def prep(unit):
        x = jnp.concatenate(
            [jnp.concatenate([xs[0, pl.ds(r + p0 * dil, cnt, stride=dil), :] for xs in x_slabs], axis=1)
             for r, p0, cnt in unit], axis=0)
        y = _rms_rows(x, g_ref[...]).astype(BF16)
        c = jnp.concatenate([cos_ref[r, :, p0:p0 + cnt] for r, p0, cnt in unit], axis=1)
        s = jnp.concatenate([sin_ref[r, :, p0:p0 + cnt] for r, p0, cnt in unit], axis=1)
        return y, c, s

    def qk_finish(t, gain, scale, c, s):
        tn = _head_norm_T(t, gain)
        ra, rb = _rot(tn[:, :half], tn[:, half:ROPE_DIMS], c[None], s[None])
        out = jnp.concatenate([ra, rb, tn[:, ROPE_DIMS:]], axis=1)
        if scale != 1.0:
            out = out * scale
        return out.reshape(t.shape)

    prepped = prep(units[0])
    for i, unit in enumerate(units):
        y, c, s = prepped
        tq, tk, tv = (lax.dot_general(wt_ref[j * d:(j + 1) * d, :], y, NT_DIMS, preferred_element_type=F32)
                      for j in range(3))
        if i + 1 < len(units):
            prepped = prep(units[i + 1])
        qT = qk_finish(tq, gq_ref[...], SCORE_SCALE * LOG2E, c, s).astype(BF16)
        k = qk_finish(tk, gk_ref[...], 1.0, c, s).T.astype(BF16)
        vT = tv.astype(BF16)
        npair = d // (2 * HEAD_DIM)
        ones_row = (lax.broadcasted_iota(jnp.int32, (npair, BF16_SUBLANES, width), 1) == 0).astype(BF16)
        vT = jnp.concatenate([vT.reshape(npair, 2 * HEAD_DIM, width), ones_row], axis=1)
        vT = vT.reshape(npair * PAIR_ROWS, width)
        col = 0
        for r, p0, cnt in unit:
            k_ref[0, r, p0:p0 + cnt] = k[col:col + cnt]
            for t in range(cnt // LANES):
                cols = slice(col + t * LANES, col + (t + 1) * LANES)
                qT_ref[0, r, p0 // LANES + t] = qT[:, cols]
                vT_ref[0, r, p0 // LANES + t] = vT[:, cols]
            col += cnt


def _a_proj(h, g1, wt, q_gain, k_gain, dil):
    b, s, d = h.shape
    l = s // dil
    width = MXU_WIDTH
    tl = max(LANES, A_PROJ_TOKENS // dil)
    tt = tl * dil
    pos = (jnp.arange(l)[None, :] * dil + jnp.arange(dil)[:, None]).reshape(-1)
    cosT, sinT = _rope_tables_T(pos, ROPE_DIMS, ROPE_THETA)
    cosT = cosT.reshape(-1, dil, l).transpose(1, 0, 2)
    sinT = sinT.reshape(-1, dil, l).transpose(1, 0, 2)
    gq = jnp.broadcast_to(q_gain[:, None], (HEAD_DIM, width))
    gk = jnp.broadcast_to(k_gain[:, None], (HEAD_DIM, width))
    n_slab = d // LANES
    slab = lambda c: pl.BlockSpec((1, tt, LANES), lambda bi, i: (bi, i, c))
    vd = d // (2 * HEAD_DIM) * PAIR_ROWS
    feat = lambda rows: pl.BlockSpec((1, dil, tl // LANES, rows, LANES), lambda bi, i: (bi, 0, i, 0, 0))
    tab = pl.BlockSpec((dil, ROPE_DIMS // 2, tl), lambda bi, i: (0, 0, i))
    feat_shape = lambda rows: jax.ShapeDtypeStruct((b, dil, l // LANES, rows, LANES), BF16)
    return pl.pallas_call(
        functools.partial(_a_proj_kernel, dil),
        grid=(b, s // tt),
        in_specs=[slab(c) for c in range(n_slab)] + [
            _resident((1, d)), _resident((3 * d, d)),
            _resident((HEAD_DIM, width)), _resident((HEAD_DIM, width)),
            tab, tab,
        ],
        out_specs=[feat(d), pl.BlockSpec((1, dil, tl, d), lambda bi, i: (bi, 0, i, 0)), feat(vd)],
        out_shape=[feat_shape(d), jax.ShapeDtypeStruct((b, dil, l, d), BF16), feat_shape(vd)],
        compiler_params=_cparams(("parallel", "parallel")),
        name="a_proj",
    )(*([h] * n_slab), g1, wt, gq, gk, cosT, sinT)


def _a_attn_kernel(qT_ref, kprev_ref, kmain_ref, knext_ref, vprev_ref, vmain_ref, vnext_ref,
                   bias_ref, cap_ref, o_ref, m_ref, l_ref, oT_scr, st_scr):
    i = pl.program_id(1)
    tb = qT_ref.shape[1]
    d = qT_ref.shape[2]
    n_heads = d // HEAD_DIM
    quad = 4 * HEAD_DIM
    win = 2 * LANES
    first_sel = jnp.where(i == 0, 0, 1)
    last_sel = jnp.where(i == pl.num_programs(1) - 1, 2, 1)
    own_head = (lax.broadcasted_iota(jnp.int32, (quad, 4 * LANES), 0) // HEAD_DIM
                == lax.broadcasted_iota(jnp.int32, (quad, 4 * LANES), 1) // LANES)

    def window(s):
        lo, hi = s * LANES - A_RADIUS, s * LANES - A_RADIUS + win
        if s == 0:
            kw = jnp.concatenate([kprev_ref[0], kmain_ref[0, 0:hi]], axis=0)
            sel = first_sel
        elif s == tb - 1:
            kw = jnp.concatenate([kmain_ref[0, lo:tb * LANES], knext_ref[0]], axis=0)
            sel = last_sel
        else:
            kw = kmain_ref[0, lo:hi]
            sel = 1
        va = vprev_ref[0, 0] if s == 0 else vmain_ref[0, s - 1]
        vc = vnext_ref[0, 0] if s == tb - 1 else vmain_ref[0, s + 1]
        vwin = jnp.concatenate([va[:, LANES - A_RADIUS:], vmain_ref[0, s], vc[:, :A_RADIUS]], axis=1)
        return kw, sel, vwin

    def quad_scores(s, u, kw):
        q4 = qT_ref[0, s, u * quad:(u + 1) * quad, :]
        q4 = jnp.concatenate([q4] * 4, axis=1)
        qz = jnp.where(own_head, q4, jnp.zeros_like(q4))
        return jnp.dot(kw[:, u * quad:(u + 1) * quad], qz, preferred_element_type=F32)

    def pair_values(vcat, pair, p, pr):
        c0 = 2 * pr * LANES
        return jnp.dot(vcat[pair * PAIR_ROWS:(pair + 1) * PAIR_ROWS, :], p[:, c0:c0 + 2 * LANES],
                       preferred_element_type=F32)

    def finish(s):
        rows = slice(s * LANES, (s + 1) * LANES)
        o_ref[0, rows, :] = oT_scr[...].T.astype(BF16)
        m_ref[0, rows, :] = st_scr[0].T
        l_ref[0, rows, :] = st_scr[1].T

    def init_stats():
        st_scr[...] = jnp.zeros_like(st_scr)
        st_scr[1, n_heads:, :] = jnp.ones((LANES - n_heads, LANES), F32)

    init_stats()
    dens = []

    def quad_values(s, u, vcat, p):
        for pr in range(2):
            h0 = 4 * u + 2 * pr
            out = pair_values(vcat, h0 // 2, p, pr)
            for hh in range(2):
                l = out[2 * HEAD_DIM:2 * HEAD_DIM + 1, hh * LANES:(hh + 1) * LANES]
                oT_scr[(h0 + hh) * HEAD_DIM:(h0 + hh + 1) * HEAD_DIM, :] = (
                    out[hh * HEAD_DIM:(hh + 1) * HEAD_DIM, hh * LANES:(hh + 1) * LANES] / l)
                st_scr[1, h0 + hh:h0 + hh + 1, :] = l
                dens.append(l)
        if u == n_heads // 4 - 1:
            finish(s)

    pending = None
    for s in range(tb):
        kw, sel, vcat = window(s)
        cap = cap_ref[sel]
        for u in range(n_heads // 4):
            sT = quad_scores(s, u, kw)
            if pending is not None:
                quad_values(*pending)
            pending = (s, u, vcat, jnp.minimum(jnp.exp2(sT).astype(BF16), cap))
    quad_values(*pending)
    lmin, lmax = dens[0], dens[0]
    for l in dens[1:]:
        lmin, lmax = jnp.minimum(lmin, l), jnp.maximum(lmax, l)
    in_range = jnp.logical_and(jnp.max(lmax) <= 2.0 ** SAFE_LOG2_SCORE,
                               jnp.min(lmin) >= 2.0 ** -SAFE_LOG2_SCORE)

    @pl.when(jnp.logical_not(in_range))
    def _():
        init_stats()
        for s in range(tb):
            kw, sel, vcat = window(s)
            bias = bias_ref[sel]
            for u in range(n_heads // 4):
                sT = quad_scores(s, u, kw) + bias
                m = jnp.max(sT, axis=0, keepdims=True)
                p = jnp.exp2(sT - m)
                l = jnp.sum(p, axis=0, keepdims=True)
                for pr in range(2):
                    h0 = 4 * u + 2 * pr
                    out = pair_values(vcat, h0 // 2, p.astype(BF16), pr)
                    for hh in range(2):
                        cols = slice((2 * pr + hh) * LANES, (2 * pr + hh + 1) * LANES)
                        oT_scr[(h0 + hh) * HEAD_DIM:(h0 + hh + 1) * HEAD_DIM, :] = (
                            out[hh * HEAD_DIM:(hh + 1) * HEAD_DIM, hh * LANES:(hh + 1) * LANES] / l[:, cols])
                for g in range(4):
                    st_scr[0, 4 * u + g:4 * u + g + 1, :] = m[:, g * LANES:(g + 1) * LANES]
                    st_scr[1, 4 * u + g:4 * u + g + 1, :] = l[:, g * LANES:(g + 1) * LANES]
            finish(s)


def _band_masks():
    kk = np.arange(2 * LANES)[:, None]
    j = np.arange(LANES)[None, :]
    band = (kk >= j) & (kk <= j + 2 * A_RADIUS)
    variants = np.stack([band & (kk >= A_RADIUS), band, band & (kk < 2 * LANES - A_RADIUS)])
    variants = np.tile(variants, (1, 1, 4))
    bias = jnp.asarray(np.where(variants, 0.0, NEG_INF), F32)
    cap = jnp.asarray(np.where(variants, float(jnp.finfo(BF16).max), 0.0), BF16)
    return bias, cap


def _a_attn(qT, k, vT):
    b, dil, nt, d, _ = qT.shape
    vd = vT.shape[3]
    n, l = b * dil, nt * LANES
    qT, k, vT = qT.reshape(n, nt, d, LANES), k.reshape(n, l, d), vT.reshape(n, nt, vd, LANES)
    tb = min(A_ATTN_SUBTILES, nt)
    assert tb >= 2 and nt % tb == 0
    hb = A_RADIUS
    per = tb * LANES // hb
    tok = lambda width: pl.BlockSpec((1, tb * LANES, width), lambda ni, i: (ni, i, 0))
    feat = lambda rows: pl.BlockSpec((1, tb, rows, LANES), lambda ni, i: (ni, i, 0, 0))
    o, m, lsum = pl.pallas_call(
        _a_attn_kernel,
        grid=(n, nt // tb),
        in_specs=[
            feat(d),
            pl.BlockSpec((1, hb, d), lambda ni, i: (ni, jnp.maximum(i * per - 1, 0), 0)),
            tok(d),
            pl.BlockSpec((1, hb, d), lambda ni, i: (ni, jnp.minimum((i + 1) * per, l // hb - 1), 0)),
            pl.BlockSpec((1, 1, vd, LANES), lambda ni, i: (ni, jnp.maximum(i * tb - 1, 0), 0, 0)),
            feat(vd),
            pl.BlockSpec((1, 1, vd, LANES), lambda ni, i: (ni, jnp.minimum((i + 1) * tb, nt - 1), 0, 0)),
            _resident((3, 2 * LANES, 4 * LANES)),
            _resident((3, 2 * LANES, 4 * LANES)),
        ],
        out_specs=[tok(d), tok(LANES), tok(LANES)],
        out_shape=[
            jax.ShapeDtypeStruct((n, l, d), BF16),
            jax.ShapeDtypeStruct((n, l, LANES), F32),
            jax.ShapeDtypeStruct((n, l, LANES), F32),
        ],
        scratch_shapes=[pltpu.VMEM((d, LANES), F32), pltpu.VMEM((2, LANES, LANES), F32)],
        compiler_params=_cparams(("parallel", "parallel")),
        name="a_attn",
    )(qT, k, k, k, vT, vT, vT, *_band_masks())
    return (o.reshape(b, dil, l, d), m.reshape(b, dil, l, LANES), lsum.reshape(b, dil, l, LANES))


def _a_out_tile(n_heads, o0, o1, o2, m0, m1, m2, l0, l1, l2, e_ref, o_scr, mid_scr, st_scr):
    tm = o_scr.shape[1]

    def stat_tokens(ref, slot):
        dil = ref.shape[1]
        if dil == 1:
            return ref[0, 0]
        for r in range(dil):
            st_scr[slot, pl.ds(r, tm // dil, stride=dil), :] = ref[0, r]
        return st_scr[slot]

    def out_tokens(ref):
        dil = ref.shape[1]
        n_slab = o_scr.shape[0]
        if dil == 1:
            return ref[0, 0].astype(F32)
        if dil <= INTERLEAVE_STEP:
            for r in range(dil):
                blk = ref[0, r].astype(F32)
                for c in range(n_slab):
                    o_scr[c, pl.ds(r, tm // dil, stride=dil), :] = blk[:, c * LANES:(c + 1) * LANES]
        else:
            outer = dil // INTERLEAVE_STEP
            rows = tm // INTERLEAVE_STEP
            for a in range(INTERLEAVE_STEP):
                for q in range(outer):
                    blk = ref[0, q * INTERLEAVE_STEP + a].astype(F32)
                    for c in range(n_slab):
                        mid_scr[c, pl.ds(q, tm // dil, stride=outer), :] = blk[:, c * LANES:(c + 1) * LANES]
                for c in range(n_slab):
                    o_scr[c, pl.ds(a, rows, stride=INTERLEAVE_STEP), :] = mid_scr[c]
        return jnp.concatenate([o_scr[c] for c in range(n_slab)], axis=1)

    ms = [stat_tokens(r, j) for j, r in enumerate((m0, m1, m2))]
    ls = [stat_tokens(r, 3 + j) for j, r in enumerate((l0, l1, l2))]
    mx = jnp.maximum(jnp.maximum(ms[0], ms[1]), ms[2])
    ws = [jnp.exp2(m - mx) * l for m, l in zip(ms, ls)]
    den = ws[0] + ws[1] + ws[2]
    head_lane = lax.broadcasted_iota(jnp.int32, den.shape, 1) < n_heads

    coefs = []
    for w in ws:
        a = jnp.where(head_lane, w / den, 0.0)
        hi = a.astype(BF16).astype(F32)
        mid = (a - hi).astype(BF16).astype(F32)
        lo = (a - hi - mid).astype(BF16).astype(F32)
        packed = hi + pltpu.roll(mid, n_heads, axis=1) + pltpu.roll(lo, 2 * n_heads, axis=1)
        coefs.append(jnp.dot(packed.astype(BF16), e_ref[...], preferred_element_type=F32))
    o = None
    for coef, o_ref in zip(coefs, (o0, o1, o2)):
        term = coef * out_tokens(o_ref)
        o = term if o is None else o + term
    return o.astype(BF16)


def _head_indicator(d):
    n_heads = d // HEAD_DIM
    e = np.zeros((LANES, d), np.float32)
    for part in range(3):
        for h in range(n_heads):
            e[part * n_heads + h, h * HEAD_DIM:(h + 1) * HEAD_DIM] = 1.0
    return jnp.asarray(e, BF16)


def _a_out_mlp(parts, wo, h, tm, g2, w1, w2):
    b, s, d = h.shape
    spb = s // tm
    blk = lambda dil, width: pl.BlockSpec((1, dil, tm // dil, width),
                                          lambda t: (t // spb, 0, t % spb, 0))
    os_, ms_, ls_ = zip(*parts)
    dils = [o.shape[1] for o in os_]
    tok = pl.BlockSpec((tm, d), lambda t: (t, 0))
    ins = [*os_, *ms_, *ls_, _head_indicator(d), wo, h.reshape(b * s, d)]
    in_specs = ([blk(dil, d) for dil in dils] + [blk(dil, LANES) for dil in dils] * 2
                + [_resident((LANES, d)), _resident((d, d)), tok])
    scratch = [pltpu.VMEM((d // LANES, tm, LANES), F32),
               pltpu.VMEM((d // LANES, tm // INTERLEAVE_STEP, LANES), F32),
               pltpu.VMEM((6, tm, LANES), F32)]
    out = _out_mlp(functools.partial(_a_out_tile, d // HEAD_DIM), ins, in_specs, scratch,
                   b * s, d, tm, g2, w1, w2, "a_out_mlp")
    return out.reshape(b, s, d)


def _mixer_a(h, g1, w_qkv, q_gain, k_gain, w_o, g2, w1, w2):
    d = h.shape[-1]
    parts = []
    for g, dil in enumerate(A_DILATIONS):
        assert A_WINDOWS[g] // (2 * dil) == A_RADIUS
        wt = w_qkv[:, g * 3 * d:(g + 1) * 3 * d].T.astype(BF16)
        parts.append(_a_attn(*_a_proj(h, g1, wt, q_gain[g], k_gain[g], dil)))
    return _a_out_mlp(parts, w_o.astype(BF16), h, TOKEN_TILE, g2, w1, w2)


def _b_in_kernel(x_ref, g_ref, w_ref, bg_ref, u_ref):
    d = x_ref.shape[-1]
    y = _rms_rows(x_ref[...], g_ref[...]).astype(BF16)
    bg_ref[...] = jnp.dot(y, w_ref[:, 0:d], preferred_element_type=F32).astype(BF16)
    cg = jnp.dot(y, w_ref[:, d:2 * d], preferred_element_type=F32)
    xt = jnp.dot(y, w_ref[:, 2 * d:3 * d], preferred_element_type=F32)
    u_ref[...] = (cg * xt).astype(BF16)


def _b_out_tile(tiles_per_seq, n_tiles, u_ref, up_ref, un_ref, bg_ref, cw_ref, scr):
    i = jnp.minimum(pl.program_id(0), n_tiles - 1)
    tm = u_ref.shape[0]
    first = (i % tiles_per_seq) == 0
    last = (i % tiles_per_seq) == tiles_per_seq - 1
    hb = up_ref.shape[0]
    scr[0:hb, :] = jnp.where(first, 0.0, up_ref[...].astype(F32))
    scr[hb:hb + tm, :] = u_ref[...].astype(F32)
    scr[hb + tm:2 * hb + tm, :] = jnp.where(last, 0.0, un_ref[...].astype(F32))
    cw = cw_ref[...]
    y = (scr[hb - 1:hb - 1 + tm, :] * cw[0:1] + scr[hb:hb + tm, :] * cw[1:2]
         + scr[hb + 1:hb + 1 + tm, :] * cw[2:3])
    return (bg_ref[...].astype(F32) * y).astype(BF16)


def _mixer_b(h, g1, w_in, conv_w, w_out, g2, w1, w2):
    b, s, d = h.shape
    t = b * s
    tm = TOKEN_TILE
    h2 = h.reshape(t, d)
    tok = pl.BlockSpec((tm, d), lambda i: (i, 0))
    bg, u = pl.pallas_call(
        _b_in_kernel,
        grid=(t // tm,),
        in_specs=[tok, _resident((1, d)), _resident((d, 3 * d))],
        out_specs=[tok, tok],
        out_shape=[jax.ShapeDtypeStruct((t, d), BF16)] * 2,
        compiler_params=_cparams(("parallel",)),
        name="b_in",
    )(h2, g1, w_in.astype(BF16))
    hb = BF16_SUBLANES
    per = tm // hb
    in_specs = [
        tok,
        pl.BlockSpec((hb, d), lambda i: (jnp.maximum(i * per - 1, 0), 0)),
        pl.BlockSpec((hb, d), lambda i: (jnp.minimum((i + 1) * per, t // hb - 1), 0)),
        tok, _resident((3, d)), _resident((d, d)), tok,
    ]
    out = _out_mlp(functools.partial(_b_out_tile, s // tm, t // tm),
                   [u, u, u, bg, conv_w, w_out.astype(BF16), h2], in_specs,
                   [pltpu.VMEM((tm + 2 * hb, d), F32)], t, d, tm, g2, w1, w2, "b_out_mlp")
    return out.reshape(b, s, d)


def _axial_rope_T(tn, cr, sr, cc, sc):
    q = HEAD_DIM // 4
    a0, b0 = _rot(tn[:, 0:q], tn[:, q:2 * q], cr[None], sr[None])
    a1, b1 = _rot(tn[:, 2 * q:3 * q], tn[:, 3 * q:], cc[None], sc[None])
    return jnp.concatenate([a0, b0, a1, b1], axis=1)


def _c_proj_kernel(x_ref, g_ref, wq_ref, wk_ref, wv_ref, gq_ref, gk_ref, cr_ref, sr_ref, cc_ref, sc_ref,
                   qT_ref, k_ref, vT_ref):
    y = _rms_rows(x_ref[0], g_ref[...]).astype(BF16)
    tabs = (cr_ref[...], sr_ref[...], cc_ref[...], sc_ref[...])
    qT = lax.dot_general(wq_ref[...], y, NT_DIMS, preferred_element_type=F32)
    kT = lax.dot_general(wk_ref[...], y, NT_DIMS, preferred_element_type=F32)
    vT = lax.dot_general(wv_ref[...], y, NT_DIMS, preferred_element_type=F32).astype(BF16)
    qn = _axial_rope_T(_head_norm_T(qT, gq_ref[...]), *tabs) * (SCORE_SCALE * LOG2E)
    qT_ref[0] = qn.reshape(qT.shape).astype(BF16)
    kn = _axial_rope_T(_head_norm_T(kT, gk_ref[...]), *tabs)
    k_ref[0] = kn.reshape(kT.shape).T.astype(BF16)
    hkv, tm = vT_ref.shape[1], vT.shape[-1]
    ones_row = (lax.broadcasted_iota(jnp.int32, (hkv, BF16_SUBLANES, tm), 1) == 0).astype(BF16)
    vT_ref[0, :, 0] = jnp.concatenate([vT.reshape(hkv, HEAD_DIM, tm), ones_row], axis=1)


def _c_attn_kernel(qT_ref, k_ref, vT_ref, oT_ref, qz_scr, acc_scr, p_scr, s_scr, cm_scr,
                   m_scr, l_scr, sacc_scr):
    grp = qT_ref.shape[1] // HEAD_DIM
    tq = qT_ref.shape[-1]
    nc, tk = vT_ref.shape[2], vT_ref.shape[-1]
    n = grp * tq

    half = lax.broadcasted_iota(jnp.int32, (LANES, tq), 0) // HEAD_DIM
    keep = half == pl.program_id(1) % 2
    for g in range(grp):
        q = qT_ref[0, g * HEAD_DIM:(g + 1) * HEAD_DIM, :]
        q2 = jnp.concatenate([q, q], axis=0)
        qz_scr[:, g * tq:(g + 1) * tq] = jnp.where(keep, q2, jnp.zeros_like(q2))

    def key_chunk(c):
        return k_ref[0, pl.ds(pl.multiple_of(c * tk, tk), tk), :]

    def store_out(o):
        for g in range(grp):
            oT_ref[0, g * HEAD_DIM:(g + 1) * HEAD_DIM, :] = o[:, g * tq:(g + 1) * tq].astype(BF16)

    acc_scr[...] = jnp.zeros_like(acc_scr)

    span = PASS1_KEYS // tk
    nq = n // PASS1_QUERIES

    def keys_of(c0):
        return k_ref[0, pl.ds(pl.multiple_of(c0 * tk, PASS1_KEYS), PASS1_KEYS), :]

    def group_values(c0, slot, j):
        pv = None
        for jc in range(span):
            pv_c = jnp.dot(vT_ref[0, 0, c0 + jc], p_scr[slot, j, jc * tk:(jc + 1) * tk, :],
                           preferred_element_type=F32)
            pv = pv_c if pv is None else pv + pv_c
        return pv

    def plain_group(cg, carry):
        pvs = [None] * nq
        pending = None
        for u in range(PASS1_UNROLL):
            c0 = (PASS1_UNROLL * cg + u) * span
            k = keys_of(c0)
            for j in range(nq):
                cols = slice(j * PASS1_QUERIES, (j + 1) * PASS1_QUERIES)
                sT = jnp.dot(k, qz_scr[:, cols], preferred_element_type=F32)
                if pending is not None:
                    pv_j = group_values(*pending, j)
                    pvs[j] = pv_j if pvs[j] is None else pvs[j] + pv_j
                p_scr[u % 2, j] = jnp.exp2(sT).astype(BF16)
            pending = (c0, u % 2)
        for j in range(nq):
            cols = slice(j * PASS1_QUERIES, (j + 1) * PASS1_QUERIES)
            acc_scr[:, cols] += pvs[j] + group_values(*pending, j)
        return carry

    lax.fori_loop(0, nc // (PASS1_UNROLL * span), plain_group, 0)
    den = acc_scr[HEAD_DIM:HEAD_DIM + 1, :]
    in_range = jnp.logical_and(jnp.max(den) <= 2.0 ** SAFE_LOG2_SCORE,
                               jnp.min(den) >= 2.0 ** -SAFE_LOG2_SCORE)

    @pl.when(in_range)
    def _():
        store_out(acc_scr[:HEAD_DIM, :] / den)

    @pl.when(jnp.logical_not(in_range))
    def _():
        m_scr[...] = jnp.full_like(m_scr, -jnp.inf)
        l_scr[...] = jnp.zeros_like(l_scr)
        sacc_scr[...] = jnp.zeros_like(sacc_scr)

        def scores(c, slot):
            sT = jnp.dot(key_chunk(c), qz_scr[...], preferred_element_type=F32)
            s_scr[slot] = sT
            cm_scr[slot] = jnp.max(sT, axis=0, keepdims=True)

        def absorb(c, slot):
            m_old = m_scr[...]
            m_new = jnp.maximum(m_old, cm_scr[slot])
            p = jnp.exp2(s_scr[slot] - m_new)
            alpha = jnp.exp2(m_old - m_new)
            l_scr[...] = alpha * l_scr[...] + jnp.sum(p, axis=0, keepdims=True)
            sacc_scr[...] = alpha * sacc_scr[...] + jnp.dot(
                vT_ref[0, 0, c, :HEAD_DIM, :], p.astype(BF16), preferred_element_type=F32)
            m_scr[...] = m_new

        scores(0, 0)

        def pair(c2, carry):
            c = 2 * c2
            scores(c + 1, 1)
            absorb(c, 0)
            scores(c + 2, 0)
            absorb(c + 1, 1)
            return carry

        lax.fori_loop(0, nc // 2 - 1, pair, 0)
        scores(nc - 1, 1)
        absorb(nc - 2, 0)
        absorb(nc - 1, 1)
        store_out(sacc_scr[...] / l_scr[...])


def _c_out_tile(oT_ref):
    return oT_ref[0].astype(F32).T.astype(BF16)


def _mixer_c(h, g1, w_qkv, q_gain, k_gain, w_o, g2, w1, w2):
    b, s, d = h.shape
    nq = d
    nk = C_KV_HEADS * HEAD_DIM
    tm = TOKEN_TILE
    nc = s // tm
    assert nc % 2 == 0 and nc >= 4 and (nc * tm) % (PASS1_UNROLL * PASS1_KEYS) == 0
    pos = jnp.arange(s)
    cr, sr = _rope_tables_T(pos // GRID_W, HEAD_DIM // 2, C_THETA)
    cc, sc = _rope_tables_T(pos % GRID_W, HEAD_DIM // 2, C_THETA)
    wq = w_qkv[:, :nq].T.astype(BF16)
    wk = w_qkv[:, nq:nq + nk].T.astype(BF16)
    wv = w_qkv[:, nq + nk:].T.astype(BF16)
    gq = jnp.broadcast_to(q_gain[:, None], (HEAD_DIM, tm))
    gk = jnp.broadcast_to(k_gain[:, None], (HEAD_DIM, tm))
    tab = pl.BlockSpec((HEAD_DIM // 4, tm), lambda bi, i: (0, i))
    qT, k, vT = pl.pallas_call(
        _c_proj_kernel,
        grid=(b, nc),
        in_specs=[
            pl.BlockSpec((1, tm, d), lambda bi, i: (bi, i, 0)),
            _resident((1, d)), _resident((nq, d)), _resident((nk, d)), _resident((nk, d)),
            _resident((HEAD_DIM, tm)), _resident((HEAD_DIM, tm)),
            tab, tab, tab, tab,
        ],
        out_specs=[
            pl.BlockSpec((1, nq, tm), lambda bi, i: (bi, 0, i)),
            pl.BlockSpec((1, tm, nk), lambda bi, i: (bi, i, 0)),
            pl.BlockSpec((1, C_KV_HEADS, 1, V_ROWS, tm), lambda bi, i: (bi, 0, i, 0, 0)),
        ],
        out_shape=[
            jax.ShapeDtypeStruct((b, nq, s), BF16),
            jax.ShapeDtypeStruct((b, s, nk), BF16),
            jax.ShapeDtypeStruct((b, C_KV_HEADS, nc, V_ROWS, tm), BF16),
        ],
        compiler_params=_cparams(("parallel", "parallel")),
        name="c_proj",
    )(h, g1, wq, wk, wv, gq, gk, cr, sr, cc, sc)

    grp = nq // nk
    tq = C_QUERY_TILE
    gw = grp * HEAD_DIM
    oT = pl.pallas_call(
        _c_attn_kernel,
        grid=(b, C_KV_HEADS, s // tq),
        in_specs=[
            pl.BlockSpec((1, gw, tq), lambda bi, hk, i: (bi, hk, i)),
            pl.BlockSpec((1, s, LANES), lambda bi, hk, i: (bi, 0, hk // 2)),
            pl.BlockSpec((1, 1, nc, V_ROWS, tm), lambda bi, hk, i: (bi, hk, 0, 0, 0)),
        ],
        out_specs=pl.BlockSpec((1, gw, tq), lambda bi, hk, i: (bi, hk, i)),
        out_shape=jax.ShapeDtypeStruct((b, nq, s), BF16),
        scratch_shapes=[pltpu.VMEM((LANES, grp * tq), BF16),
                        pltpu.VMEM((V_ROWS, grp * tq), F32),
                        pltpu.VMEM((2, grp * tq // PASS1_QUERIES, PASS1_KEYS, PASS1_QUERIES), BF16),
                        pltpu.VMEM((2, tm, grp * tq), F32), pltpu.VMEM((2, 1, grp * tq), F32),
                        pltpu.VMEM((1, grp * tq), F32), pltpu.VMEM((1, grp * tq), F32),
                        pltpu.VMEM((HEAD_DIM, grp * tq), F32)],
        compiler_params=_cparams(("parallel", "parallel", "parallel")),
        name="c_attn",
    )(qT, k, vT)

    in_specs = [
        pl.BlockSpec((1, d, tm), lambda t: (t // nc, 0, t % nc)),
        _resident((d, d)),
        pl.BlockSpec((tm, d), lambda t: (t, 0)),
    ]
    out = _out_mlp(_c_out_tile, [oT, w_o.astype(BF16), h.reshape(b * s, d)], in_specs, [],
                   b * s, d, tm, g2, w1, w2, "c_out_mlp")
    return out.reshape(b, s, d)


def kernel(x, norm1, norm2, a_wqkv, a_q_gain, a_k_gain, a_wo, b_win, b_conv, b_wout,
           c_wqkv, c_q_gain, c_k_gain, c_wo, mlp_w1, mlp_w2):
    h = x
    for i in range(norm1.shape[0]):
        kind, j = i % N_MIXERS, i // N_MIXERS
        g1 = norm1[i][None, :]
        mlp = (norm2[i][None, :], mlp_w1[i].astype(BF16), mlp_w2[i].astype(BF16))
        if kind == 0:
            h = _mixer_a(h, g1, a_wqkv[j], a_q_gain[j], a_k_gain[j], a_wo[j], *mlp)
        elif kind == 1:
            h = _mixer_b(h, g1, b_win[j], b_conv[j], b_wout[j], *mlp)
        else:
            h = _mixer_c(h, g1, c_wqkv[j], c_q_gain[j], c_k_gain[j], c_wo[j], *mlp)
    return h
```

```python
import functools
import math

import jax
import jax.numpy as jnp
import numpy as np
from jax import lax
from jax.experimental import pallas as pl
from jax.experimental.pallas import tpu as pltpu

HEAD_DIM = 64
EPS = 1e-6
NEG_INF = -1e30
N_MIXERS = 3
A_WINDOWS = (128, 512, 2048)
A_DILATIONS = (1, 4, 16)
A_RADIUS = 64
ROPE_THETA = 500000.0
ROPE_DIMS = HEAD_DIM // 4
C_KV_HEADS = 4
C_THETA = 10000.0
GRID_W = 64
SCORE_SCALE = HEAD_DIM ** -0.5
LOG2E = math.log2(math.e)

LANES = 128
BF16_SUBLANES = 16
MXU_WIDTH = 256
INTERLEAVE_STEP = 4

TOKEN_TILE = 512
A_PROJ_TOKENS = 1024
A_ATTN_SUBTILES = 8
C_QUERY_TILE = 512
V_ROWS = HEAD_DIM + BF16_SUBLANES
PAIR_ROWS = 2 * HEAD_DIM + BF16_SUBLANES
SAFE_LOG2_SCORE = 64.0
PASS1_KEYS = 512
PASS1_UNROLL = 16
PASS1_QUERIES = 512
VMEM_LIMIT = 56 * 1024 * 1024

F32 = jnp.float32
BF16 = jnp.bfloat16

NT_DIMS = (((1,), (1,)), ((), ()))


def _cparams(sem):
    return pltpu.CompilerParams(dimension_semantics=sem, vmem_limit_bytes=VMEM_LIMIT)


def _resident(shape):
    nd = len(shape)
    return pl.BlockSpec(shape, lambda *_: (0,) * nd, pipeline_mode=pl.Buffered(1))


def _rms_rows(x, g):
    ms = jnp.mean(x * x, axis=-1, keepdims=True)
    return (x * lax.rsqrt(ms + EPS)) * g


def _head_norm_T(t, gain):
    n = t.shape[-1]
    t3 = t.reshape(t.shape[0] // HEAD_DIM, HEAD_DIM, n)
    ms = jnp.mean(t3 * t3, axis=1, keepdims=True)
    return (t3 * lax.rsqrt(ms + EPS)) * gain[None]


def _rot(a, b, c, s):
    return a * c - b * s, b * c + a * s


def _rope_tables_T(pos, dim, theta):
    inv = theta ** (-jnp.arange(0, dim, 2, dtype=F32) / dim)
    ang = inv[:, None] * pos.astype(F32)[None, :]
    return jnp.cos(ang), jnp.sin(ang)


def _mlp_tile(x, g_ref, w1_ref, w2_ref):
    d = x.shape[-1]
    y = _rms_rows(x, g_ref[...]).astype(BF16)
    acc = x
    for c in range(w1_ref.shape[1] // d):
        cols = slice(c * d, (c + 1) * d)
        a = jnp.maximum(jnp.dot(y, w1_ref[:, cols], preferred_element_type=F32), 0.0)
        acc = acc + jnp.dot((a * a).astype(BF16), w2_ref[cols, :], preferred_element_type=F32)
    return acc


def _out_mlp_kernel(tile_fn, n_in, *refs):
    ins = refs[:n_in]
    g_ref, w1_ref, w2_ref, out_ref, hm_even, hm_odd = refs[n_in:n_in + 6]
    tile_scr = refs[n_in + 6:]
    i = pl.program_id(0)

    @pl.when(i == 0)
    def _():
        hm_odd[...] = jnp.zeros_like(hm_odd)

    *tile_ins, wo_ref, h_ref = ins

    def step(dst, src):
        lhs = tile_fn(*tile_ins, *tile_scr)
        out_ref[...] = _mlp_tile(src[...], g_ref, w1_ref, w2_ref)
        dst[...] = h_ref[...] + jnp.dot(lhs, wo_ref[...], preferred_element_type=F32)

    @pl.when(i % 2 == 0)
    def _():
        step(hm_even, hm_odd)

    @pl.when(i % 2 == 1)
    def _():
        step(hm_odd, hm_even)


def _out_mlp(tile_fn, ins, in_specs, tile_scratch, t, d, tm, g2, w1, w2, name):
    nt = t // tm
    clamp = lambda f: (lambda i: f(jnp.minimum(i, nt - 1)))
    specs = [pl.BlockSpec(sp.block_shape, clamp(sp.index_map), pipeline_mode=sp.pipeline_mode)
             for sp in in_specs]
    return pl.pallas_call(
        functools.partial(_out_mlp_kernel, tile_fn, len(ins)),
        grid=(nt + 1,),
        in_specs=specs + [_resident((1, d)), _resident(w1.shape), _resident(w2.shape)],
        out_specs=pl.BlockSpec((tm, d), lambda i: (jnp.maximum(i - 1, 0), 0)),
        out_shape=jax.ShapeDtypeStruct((t, d), F32),
        scratch_shapes=[pltpu.VMEM((tm, d), F32), pltpu.VMEM((tm, d), F32)] + tile_scratch,
        compiler_params=_cparams(("arbitrary",)),
        name=name,
    )(*ins, g2, w1, w2)


def _a_proj_kernel(dil, *refs):
    two_pass = dil > INTERLEAVE_STEP
    n_slab = len(refs) - 9 - two_pass
    x_slabs = refs[:n_slab]
    g_ref, wt_ref, gq_ref, gk_ref, cos_ref, sin_ref, qT_ref, k_ref, vT_ref = refs[n_slab:n_slab + 9]
    d = wt_ref.shape[1]
    tl = cos_ref.shape[-1]
    width = gq_ref.shape[-1]
    half = ROPE_DIMS // 2
    if dil == 1:
        units = [[(0, p0, width)] for p0 in range(0, tl, width)]
    elif not two_pass:
        units = [[(r, 0, tl) for r in range(r0, r0 + width // tl)] for r0 in range(0, dil, width // tl)]
    else:
        outer, per = dil // INTERLEAVE_STEP, width // tl
        units = [[(q * INTERLEAVE_STEP + a, 0, tl) for q in range(q0, q0 + per)]
                 for a in range(INTERLEAVE_STEP) for q0 in range(0, outer, per)]

    def stream_rows(unit):
        if not two_pass:
            return [jnp.concatenate([xs[0, pl.ds(r + p0 * dil, cnt, stride=dil), :] for xs in x_slabs], axis=1)
                    for r, p0, cnt in unit]
        mid_ref = refs[-1]
        a = unit[0][0] % INTERLEAVE_STEP
        if unit[0][0] // INTERLEAVE_STEP == 0:
            for c, xs in enumerate(x_slabs):
                mid_ref[c] = xs[0, pl.ds(a, mid_ref.shape[1], stride=INTERLEAVE_STEP), :]
        return [jnp.concatenate([mid_ref[c, pl.ds(r // INTERLEAVE_STEP, cnt, stride=dil // INTERLEAVE_STEP), :]
                                 for c in range(n_slab)], axis=1) for r, _, cnt in unit]

    def prep(unit):
        x = jnp.concatenate(stream_rows(unit), axis=0)
        y = _rms_rows(x, g_ref[...]).astype(BF16)
        c = jnp.concatenate([cos_ref[r, :, p0:p0 + cnt] for r, p0, cnt in unit], axis=1)
        s = jnp.concatenate([sin_ref[r, :, p0:p0 + cnt] for r, p0, cnt in unit], axis=1)
        return y, c, s

    def qk_finish(t, gain, scale, c, s):
        tn = _head_norm_T(t, gain)
        ra, rb = _rot(tn[:, :half], tn[:, half:ROPE_DIMS], c[None], s[None])
        out = jnp.concatenate([ra, rb, tn[:, ROPE_DIMS:]], axis=1)
        if scale != 1.0:
            out = out * scale
        return out.reshape(t.shape)

    prepped = prep(units[0])
    for i, unit in enumerate(units):
        y, c, s = prepped
        tq, tk, tv = (lax.dot_general(wt_ref[j * d:(j + 1) * d, :], y, NT_DIMS, preferred_element_type=F32)
                      for j in range(3))
        if i + 1 < len(units):
            prepped = prep(units[i + 1])
        qT = qk_finish(tq, gq_ref[...], SCORE_SCALE * LOG2E, c, s).astype(BF16)
        k = qk_finish(tk, gk_ref[...], 1.0, c, s).T.astype(BF16)
        vT = tv.astype(BF16)
        npair = d // (2 * HEAD_DIM)
        ones_row = (lax.broadcasted_iota(jnp.int32, (npair, BF16_SUBLANES, width), 1) == 0).astype(BF16)
        vT = jnp.concatenate([vT.reshape(npair, 2 * HEAD_DIM, width), ones_row], axis=1)
        vT = vT.reshape(npair * PAIR_ROWS, width)
        col = 0
        for r, p0, cnt in unit:
            k_ref[0, r, p0:p0 + cnt] = k[col:col + cnt]
            for t in range(cnt // LANES):
                cols = slice(col + t * LANES, col + (t + 1) * LANES)
                qT_ref[0, r, p0 // LANES + t] = qT[:, cols]
                vT_ref[0, r, p0 // LANES + t] = vT[:, cols]
            col += cnt


def _a_proj(h, g1, wt, q_gain, k_gain, dil):
    b, s, d = h.shape
    l = s // dil
    width = MXU_WIDTH
    tl = max(LANES, A_PROJ_TOKENS // dil)
    tt = tl * dil
    pos = (jnp.arange(l)[None, :] * dil + jnp.arange(dil)[:, None]).reshape(-1)
    cosT, sinT = _rope_tables_T(pos, ROPE_DIMS, ROPE_THETA)
    cosT = cosT.reshape(-1, dil, l).transpose(1, 0, 2)
    sinT = sinT.reshape(-1, dil, l).transpose(1, 0, 2)
    gq = jnp.broadcast_to(q_gain[:, None], (HEAD_DIM, width))
    gk = jnp.broadcast_to(k_gain[:, None], (HEAD_DIM, width))
    n_slab = d // LANES
    slab = lambda c: pl.BlockSpec((1, tt, LANES), lambda bi, i: (bi, i, c))
    vd = d // (2 * HEAD_DIM) * PAIR_ROWS
    feat = lambda rows: pl.BlockSpec((1, dil, tl // LANES, rows, LANES), lambda bi, i: (bi, 0, i, 0, 0))
    tab = pl.BlockSpec((dil, ROPE_DIMS // 2, tl), lambda bi, i: (0, 0, i))
    feat_shape = lambda rows: jax.ShapeDtypeStruct((b, dil, l // LANES, rows, LANES), BF16)
    return pl.pallas_call(
        functools.partial(_a_proj_kernel, dil),
        grid=(b, s // tt),
        in_specs=[slab(c) for c in range(n_slab)] + [
            _resident((1, d)), _resident((3 * d, d)),
            _resident((HEAD_DIM, width)), _resident((HEAD_DIM, width)),
            tab, tab,
        ],
        out_specs=[feat(d), pl.BlockSpec((1, dil, tl, d), lambda bi, i: (bi, 0, i, 0)), feat(vd)],
        out_shape=[feat_shape(d), jax.ShapeDtypeStruct((b, dil, l, d), BF16), feat_shape(vd)],
        scratch_shapes=([pltpu.VMEM((n_slab, tt // INTERLEAVE_STEP, LANES), F32)]
                        if dil > INTERLEAVE_STEP else []),
        compiler_params=_cparams(("parallel", "parallel")),
        name="a_proj",
    )(*([h] * n_slab), g1, wt, gq, gk, cosT, sinT)


def _a_attn_kernel(qT_ref, kprev_ref, kmain_ref, knext_ref, vprev_ref, vmain_ref, vnext_ref,
                   bias_ref, cap_ref, o_ref, m_ref, l_ref, oT_scr, st_scr):
    i = pl.program_id(1)
    tb = qT_ref.shape[1]
    d = qT_ref.shape[2]
    n_heads = d // HEAD_DIM
    quad = 4 * HEAD_DIM
    win = 2 * LANES
    first_sel = jnp.where(i == 0, 0, 1)
    last_sel = jnp.where(i == pl.num_programs(1) - 1, 2, 1)
    own_head = (lax.broadcasted_iota(jnp.int32, (quad, 4 * LANES), 0) // HEAD_DIM
                == lax.broadcasted_iota(jnp.int32, (quad, 4 * LANES), 1) // LANES)

    def window(s):
        lo, hi = s * LANES - A_RADIUS, s * LANES - A_RADIUS + win
        if s == 0:
            kw = jnp.concatenate([kprev_ref[0], kmain_ref[0, 0:hi]], axis=0)
            sel = first_sel
        elif s == tb - 1:
            kw = jnp.concatenate([kmain_ref[0, lo:tb * LANES], knext_ref[0]], axis=0)
            sel = last_sel
        else:
            kw = kmain_ref[0, lo:hi]
            sel = 1
        va = vprev_ref[0, 0] if s == 0 else vmain_ref[0, s - 1]
        vc = vnext_ref[0, 0] if s == tb - 1 else vmain_ref[0, s + 1]
        vwin = jnp.concatenate([va[:, LANES - A_RADIUS:], vmain_ref[0, s], vc[:, :A_RADIUS]], axis=1)
        return kw, sel, vwin

    def quad_scores(s, u, kw):
        q4 = qT_ref[0, s, u * quad:(u + 1) * quad, :]
        q4 = jnp.concatenate([q4] * 4, axis=1)
        qz = jnp.where(own_head, q4, jnp.zeros_like(q4))
        return jnp.dot(kw[:, u * quad:(u + 1) * quad], qz, preferred_element_type=F32)

    def pair_values(vcat, pair, p, pr):
        c0 = 2 * pr * LANES
        return jnp.dot(vcat[pair * PAIR_ROWS:(pair + 1) * PAIR_ROWS, :], p[:, c0:c0 + 2 * LANES],
                       preferred_element_type=F32)

    def finish(s):
        rows = slice(s * LANES, (s + 1) * LANES)
        o_ref[0, rows, :] = oT_scr[...].T.astype(BF16)
        m_ref[0, rows, :] = st_scr[0].T
        l_ref[0, rows, :] = st_scr[1].T

    def init_stats():
        st_scr[...] = jnp.zeros_like(st_scr)
        st_scr[1, n_heads:, :] = jnp.ones((LANES - n_heads, LANES), F32)

    init_stats()
    dens = []

    def quad_values(s, u, vcat, p):
        for pr in range(2):
            h0 = 4 * u + 2 * pr
            out = pair_values(vcat, h0 // 2, p, pr)
            for hh in range(2):
                l = out[2 * HEAD_DIM:2 * HEAD_DIM + 1, hh * LANES:(hh + 1) * LANES]
                oT_scr[(h0 + hh) * HEAD_DIM:(h0 + hh + 1) * HEAD_DIM, :] = (
                    out[hh * HEAD_DIM:(hh + 1) * HEAD_DIM, hh * LANES:(hh + 1) * LANES] / l)
                st_scr[1, h0 + hh:h0 + hh + 1, :] = l
                dens.append(l)
        if u == n_heads // 4 - 1:
            finish(s)

    pending = None
    for s in range(tb):
        kw, sel, vcat = window(s)
        cap = cap_ref[sel]
        for u in range(n_heads // 4):
            sT = quad_scores(s, u, kw)
            if pending is not None:
                quad_values(*pending)
            pending = (s, u, vcat, jnp.minimum(jnp.exp2(sT).astype(BF16), cap))
    quad_values(*pending)
    lmin, lmax = dens[0], dens[0]
    for l in dens[1:]:
        lmin, lmax = jnp.minimum(lmin, l), jnp.maximum(lmax, l)
    in_range = jnp.logical_and(jnp.max(lmax) <= 2.0 ** SAFE_LOG2_SCORE,
                               jnp.min(lmin) >= 2.0 ** -SAFE_LOG2_SCORE)

    @pl.when(jnp.logical_not(in_range))
    def _():
        init_stats()
        for s in range(tb):
            kw, sel, vcat = window(s)
            bias = bias_ref[sel]
            for u in range(n_heads // 4):
                sT = quad_scores(s, u, kw) + bias
                m = jnp.max(sT, axis=0, keepdims=True)
                p = jnp.exp2(sT - m)
                l = jnp.sum(p, axis=0, keepdims=True)
                for pr in range(2):
                    h0 = 4 * u + 2 * pr
                    out = pair_values(vcat, h0 // 2, p.astype(BF16), pr)
                    for hh in range(2):
                        cols = slice((2 * pr + hh) * LANES, (2 * pr + hh + 1) * LANES)
                        oT_scr[(h0 + hh) * HEAD_DIM:(h0 + hh + 1) * HEAD_DIM, :] = (
                            out[hh * HEAD_DIM:(hh + 1) * HEAD_DIM, hh * LANES:(hh + 1) * LANES] / l[:, cols])
                for g in range(4):
                    st_scr[0, 4 * u + g:4 * u + g + 1, :] = m[:, g * LANES:(g + 1) * LANES]
                    st_scr[1, 4 * u + g:4 * u + g + 1, :] = l[:, g * LANES:(g + 1) * LANES]
            finish(s)


def _band_masks():
    kk = np.arange(2 * LANES)[:, None]
    j = np.arange(LANES)[None, :]
    band = (kk >= j) & (kk <= j + 2 * A_RADIUS)
    variants = np.stack([band & (kk >= A_RADIUS), band, band & (kk < 2 * LANES - A_RADIUS)])
    variants = np.tile(variants, (1, 1, 4))
    bias = jnp.asarray(np.where(variants, 0.0, NEG_INF), F32)
    cap = jnp.asarray(np.where(variants, float(jnp.finfo(BF16).max), 0.0), BF16)
    return bias, cap


def _a_attn(qT, k, vT):
    b, dil, nt, d, _ = qT.shape
    vd = vT.shape[3]
    n, l = b * dil, nt * LANES
    qT, k, vT = qT.reshape(n, nt, d, LANES), k.reshape(n, l, d), vT.reshape(n, nt, vd, LANES)
    tb = min(A_ATTN_SUBTILES, nt)
    assert tb >= 2 and nt % tb == 0
    hb = A_RADIUS
    per = tb * LANES // hb
    tok = lambda width: pl.BlockSpec((1, tb * LANES, width), lambda ni, i: (ni, i, 0))
    feat = lambda rows: pl.BlockSpec((1, tb, rows, LANES), lambda ni, i: (ni, i, 0, 0))
    o, m, lsum = pl.pallas_call(
        _a_attn_kernel,
        grid=(n, nt // tb),
        in_specs=[
            feat(d),
            pl.BlockSpec((1, hb, d), lambda ni, i: (ni, jnp.maximum(i * per - 1, 0), 0)),
            tok(d),
            pl.BlockSpec((1, hb, d), lambda ni, i: (ni, jnp.minimum((i + 1) * per, l // hb - 1), 0)),
            pl.BlockSpec((1, 1, vd, LANES), lambda ni, i: (ni, jnp.maximum(i * tb - 1, 0), 0, 0)),
            feat(vd),
            pl.BlockSpec((1, 1, vd, LANES), lambda ni, i: (ni, jnp.minimum((i + 1) * tb, nt - 1), 0, 0)),
            _resident((3, 2 * LANES, 4 * LANES)),
            _resident((3, 2 * LANES, 4 * LANES)),
        ],
        out_specs=[tok(d), tok(LANES), tok(LANES)],
        out_shape=[
            jax.ShapeDtypeStruct((n, l, d), BF16),
            jax.ShapeDtypeStruct((n, l, LANES), F32),
            jax.ShapeDtypeStruct((n, l, LANES), F32),
        ],
        scratch_shapes=[pltpu.VMEM((d, LANES), F32), pltpu.VMEM((2, LANES, LANES), F32)],
        compiler_params=_cparams(("parallel", "parallel")),
        name="a_attn",
    )(qT, k, k, k, vT, vT, vT, *_band_masks())
    return (o.reshape(b, dil, l, d), m.reshape(b, dil, l, LANES), lsum.reshape(b, dil, l, LANES))


def _a_out_tile(n_heads, o0, o1, o2, m0, m1, m2, l0, l1, l2, e_ref, o_scr, mid_scr, st_scr):
    tm = o_scr.shape[1]

    def stat_tokens(ref, slot):
        dil = ref.shape[1]
        if dil == 1:
            return ref[0, 0]
        for r in range(dil):
            st_scr[slot, pl.ds(r, tm // dil, stride=dil), :] = ref[0, r]
        return st_scr[slot]

    def out_tokens(ref):
        dil = ref.shape[1]
        n_slab = o_scr.shape[0]
        if dil == 1:
            return ref[0, 0].astype(F32)
        if dil <= INTERLEAVE_STEP:
            for r in range(dil):
                blk = ref[0, r].astype(F32)
                for c in range(n_slab):
                    o_scr[c, pl.ds(r, tm // dil, stride=dil), :] = blk[:, c * LANES:(c + 1) * LANES]
        else:
            outer = dil // INTERLEAVE_STEP
            rows = tm // INTERLEAVE_STEP
            for a in range(INTERLEAVE_STEP):
                for q in range(outer):
                    blk = ref[0, q * INTERLEAVE_STEP + a].astype(F32)
                    for c in range(n_slab):
                        mid_scr[c, pl.ds(q, tm // dil, stride=outer), :] = blk[:, c * LANES:(c + 1) * LANES]
                for c in range(n_slab):
                    o_scr[c, pl.ds(a, rows, stride=INTERLEAVE_STEP), :] = mid_scr[c]
        return jnp.concatenate([o_scr[c] for c in range(n_slab)], axis=1)

    ms = [stat_tokens(r, j) for j, r in enumerate((m0, m1, m2))]
    ls = [stat_tokens(r, 3 + j) for j, r in enumerate((l0, l1, l2))]
    mx = jnp.maximum(jnp.maximum(ms[0], ms[1]), ms[2])
    ws = [jnp.exp2(m - mx) * l for m, l in zip(ms, ls)]
    den = ws[0] + ws[1] + ws[2]
    head_lane = lax.broadcasted_iota(jnp.int32, den.shape, 1) < n_heads

    coefs = []
    for w in ws:
        a = jnp.where(head_lane, w / den, 0.0)
        hi = a.astype(BF16).astype(F32)
        mid = (a - hi).astype(BF16).astype(F32)
        lo = (a - hi - mid).astype(BF16).astype(F32)
        packed = hi + pltpu.roll(mid, n_heads, axis=1) + pltpu.roll(lo, 2 * n_heads, axis=1)
        coefs.append(jnp.dot(packed.astype(BF16), e_ref[...], preferred_element_type=F32))
    o = None
    for coef, o_ref in zip(coefs, (o0, o1, o2)):
        term = coef * out_tokens(o_ref)
        o = term if o is None else o + term
    return o.astype(BF16)


def _head_indicator(d):
    n_heads = d // HEAD_DIM
    e = np.zeros((LANES, d), np.float32)
    for part in range(3):
        for h in range(n_heads):
            e[part * n_heads + h, h * HEAD_DIM:(h + 1) * HEAD_DIM] = 1.0
    return jnp.asarray(e, BF16)


def _a_out_mlp(parts, wo, h, tm, g2, w1, w2):
    b, s, d = h.shape
    spb = s // tm
    blk = lambda dil, width: pl.BlockSpec((1, dil, tm // dil, width),
                                          lambda t: (t // spb, 0, t % spb, 0))
    os_, ms_, ls_ = zip(*parts)
    dils = [o.shape[1] for o in os_]
    tok = pl.BlockSpec((tm, d), lambda t: (t, 0))
    ins = [*os_, *ms_, *ls_, _head_indicator(d), wo, h.reshape(b * s, d)]
    in_specs = ([blk(dil, d) for dil in dils] + [blk(dil, LANES) for dil in dils] * 2
                + [_resident((LANES, d)), _resident((d, d)), tok])
    scratch = [pltpu.VMEM((d // LANES, tm, LANES), F32),
               pltpu.VMEM((d // LANES, tm // INTERLEAVE_STEP, LANES), F32),
               pltpu.VMEM((6, tm, LANES), F32)]
    out = _out_mlp(functools.partial(_a_out_tile, d // HEAD_DIM), ins, in_specs, scratch,
                   b * s, d, tm, g2, w1, w2, "a_out_mlp")
    return out.reshape(b, s, d)


def _mixer_a(h, g1, w_qkv, q_gain, k_gain, w_o, g2, w1, w2):
    d = h.shape[-1]
    parts = []
    for g, dil in enumerate(A_DILATIONS):
        assert A_WINDOWS[g] // (2 * dil) == A_RADIUS
        wt = w_qkv[:, g * 3 * d:(g + 1) * 3 * d].T.astype(BF16)
        parts.append(_a_attn(*_a_proj(h, g1, wt, q_gain[g], k_gain[g], dil)))
    return _a_out_mlp(parts, w_o.astype(BF16), h, TOKEN_TILE, g2, w1, w2)


def _b_in_kernel(x_ref, g_ref, w_ref, bg_ref, u_ref):
    d = x_ref.shape[-1]
    y = _rms_rows(x_ref[...], g_ref[...]).astype(BF16)
    bg_ref[...] = jnp.dot(y, w_ref[:, 0:d], preferred_element_type=F32).astype(BF16)
    cg = jnp.dot(y, w_ref[:, d:2 * d], preferred_element_type=F32)
    xt = jnp.dot(y, w_ref[:, 2 * d:3 * d], preferred_element_type=F32)
    u_ref[...] = (cg * xt).astype(BF16)


def _b_out_tile(tiles_per_seq, n_tiles, u_ref, up_ref, un_ref, bg_ref, cw_ref, scr):
    i = jnp.minimum(pl.program_id(0), n_tiles - 1)
    tm = u_ref.shape[0]
    first = (i % tiles_per_seq) == 0
    last = (i % tiles_per_seq) == tiles_per_seq - 1
    hb = up_ref.shape[0]
    scr[0:hb, :] = jnp.where(first, 0.0, up_ref[...].astype(F32))
    scr[hb:hb + tm, :] = u_ref[...].astype(F32)
    scr[hb + tm:2 * hb + tm, :] = jnp.where(last, 0.0, un_ref[...].astype(F32))
    cw = cw_ref[...]
    y = (scr[hb - 1:hb - 1 + tm, :] * cw[0:1] + scr[hb:hb + tm, :] * cw[1:2]
         + scr[hb + 1:hb + 1 + tm, :] * cw[2:3])
    return (bg_ref[...].astype(F32) * y).astype(BF16)


def _mixer_b(h, g1, w_in, conv_w, w_out, g2, w1, w2):
    b, s, d = h.shape
    t = b * s
    tm = TOKEN_TILE
    h2 = h.reshape(t, d)
    tok = pl.BlockSpec((tm, d), lambda i: (i, 0))
    bg, u = pl.pallas_call(
        _b_in_kernel,
        grid=(t // tm,),
        in_specs=[tok, _resident((1, d)), _resident((d, 3 * d))],
        out_specs=[tok, tok],
        out_shape=[jax.ShapeDtypeStruct((t, d), BF16)] * 2,
        compiler_params=_cparams(("parallel",)),
        name="b_in",
    )(h2, g1, w_in.astype(BF16))
    hb = BF16_SUBLANES
    per = tm // hb
    in_specs = [
        tok,
        pl.BlockSpec((hb, d), lambda i: (jnp.maximum(i * per - 1, 0), 0)),
        pl.BlockSpec((hb, d), lambda i: (jnp.minimum((i + 1) * per, t // hb - 1), 0)),
        tok, _resident((3, d)), _resident((d, d)), tok,
    ]
    out = _out_mlp(functools.partial(_b_out_tile, s // tm, t // tm),
                   [u, u, u, bg, conv_w, w_out.astype(BF16), h2], in_specs,
                   [pltpu.VMEM((tm + 2 * hb, d), F32)], t, d, tm, g2, w1, w2, "b_out_mlp")
    return out.reshape(b, s, d)


def _axial_rope_T(tn, cr, sr, cc, sc):
    q = HEAD_DIM // 4
    a0, b0 = _rot(tn[:, 0:q], tn[:, q:2 * q], cr[None], sr[None])
    a1, b1 = _rot(tn[:, 2 * q:3 * q], tn[:, 3 * q:], cc[None], sc[None])
    return jnp.concatenate([a0, b0, a1, b1], axis=1)


def _c_proj_kernel(x_ref, g_ref, wq_ref, wk_ref, wv_ref, gq_ref, gk_ref, cr_ref, sr_ref, cc_ref, sc_ref,
                   qT_ref, k_ref, vT_ref):
    y = _rms_rows(x_ref[0], g_ref[...]).astype(BF16)
    tabs = (cr_ref[...], sr_ref[...], cc_ref[...], sc_ref[...])
    qT = lax.dot_general(wq_ref[...], y, NT_DIMS, preferred_element_type=F32)
    kT = lax.dot_general(wk_ref[...], y, NT_DIMS, preferred_element_type=F32)
    vT = lax.dot_general(wv_ref[...], y, NT_DIMS, preferred_element_type=F32).astype(BF16)
    qn = _axial_rope_T(_head_norm_T(qT, gq_ref[...]), *tabs) * (SCORE_SCALE * LOG2E)
    qT_ref[0] = qn.reshape(qT.shape).astype(BF16)
    kn = _axial_rope_T(_head_norm_T(kT, gk_ref[...]), *tabs)
    k_ref[0] = kn.reshape(kT.shape).T.astype(BF16)
    hkv, tm = vT_ref.shape[1], vT.shape[-1]
    ones_row = (lax.broadcasted_iota(jnp.int32, (hkv, BF16_SUBLANES, tm), 1) == 0).astype(BF16)
    vT_ref[0, :, 0] = jnp.concatenate([vT.reshape(hkv, HEAD_DIM, tm), ones_row], axis=1)


def _c_attn_kernel(qT_ref, k_ref, vT_ref, oT_ref, qz_scr, acc_scr, p_scr, s_scr, cm_scr,
                   m_scr, l_scr, sacc_scr):
    grp = qT_ref.shape[1] // HEAD_DIM
    tq = qT_ref.shape[-1]
    nc, tk = vT_ref.shape[2], vT_ref.shape[-1]
    n = grp * tq

    half = lax.broadcasted_iota(jnp.int32, (LANES, tq), 0) // HEAD_DIM
    keep = half == pl.program_id(1) % 2
    for g in range(grp):
        q = qT_ref[0, g * HEAD_DIM:(g + 1) * HEAD_DIM, :]
        q2 = jnp.concatenate([q, q], axis=0)
        qz_scr[:, g * tq:(g + 1) * tq] = jnp.where(keep, q2, jnp.zeros_like(q2))

    def key_chunk(c):
        return k_ref[0, pl.ds(pl.multiple_of(c * tk, tk), tk), :]

    def store_out(o):
        for g in range(grp):
            oT_ref[0, g * HEAD_DIM:(g + 1) * HEAD_DIM, :] = o[:, g * tq:(g + 1) * tq].astype(BF16)

    acc_scr[...] = jnp.zeros_like(acc_scr)

    span = PASS1_KEYS // tk
    nq = n // PASS1_QUERIES

    def keys_of(c0):
        return k_ref[0, pl.ds(pl.multiple_of(c0 * tk, PASS1_KEYS), PASS1_KEYS), :]

    def group_values(c0, slot, j):
        pv = None
        for jc in range(span):
            pv_c = jnp.dot(vT_ref[0, 0, c0 + jc], p_scr[slot, j, jc * tk:(jc + 1) * tk, :],
                           preferred_element_type=F32)
            pv = pv_c if pv is None else pv + pv_c
        return pv

    def plain_group(cg, carry):
        pvs = [None] * nq
        pending = None
        for u in range(PASS1_UNROLL):
            c0 = (PASS1_UNROLL * cg + u) * span
            k = keys_of(c0)
            for j in range(nq):
                cols = slice(j * PASS1_QUERIES, (j + 1) * PASS1_QUERIES)
                sT = jnp.dot(k, qz_scr[:, cols], preferred_element_type=F32)
                if pending is not None:
                    pv_j = group_values(*pending, j)
                    pvs[j] = pv_j if pvs[j] is None else pvs[j] + pv_j
                p_scr[u % 2, j] = jnp.exp2(sT).astype(BF16)
            pending = (c0, u % 2)
        for j in range(nq):
            cols = slice(j * PASS1_QUERIES, (j + 1) * PASS1_QUERIES)
            acc_scr[:, cols] += pvs[j] + group_values(*pending, j)
        return carry

    lax.fori_loop(0, nc // (PASS1_UNROLL * span), plain_group, 0)
    den = acc_scr[HEAD_DIM:HEAD_DIM + 1, :]
    in_range = jnp.logical_and(jnp.max(den) <= 2.0 ** SAFE_LOG2_SCORE,
                               jnp.min(den) >= 2.0 ** -SAFE_LOG2_SCORE)

    @pl.when(in_range)
    def _():
        store_out(acc_scr[:HEAD_DIM, :] / den)

    @pl.when(jnp.logical_not(in_range))
    def _():
        m_scr[...] = jnp.full_like(m_scr, -jnp.inf)
        l_scr[...] = jnp.zeros_like(l_scr)
        sacc_scr[...] = jnp.zeros_like(sacc_scr)

        def scores(c, slot):
            sT = jnp.dot(key_chunk(c), qz_scr[...], preferred_element_type=F32)
            s_scr[slot] = sT
            cm_scr[slot] = jnp.max(sT, axis=0, keepdims=True)

        def absorb(c, slot):
            m_old = m_scr[...]
            m_new = jnp.maximum(m_old, cm_scr[slot])
            p = jnp.exp2(s_scr[slot] - m_new)
            alpha = jnp.exp2(m_old - m_new)
            l_scr[...] = alpha * l_scr[...] + jnp.sum(p, axis=0, keepdims=True)
            sacc_scr[...] = alpha * sacc_scr[...] + jnp.dot(
                vT_ref[0, 0, c, :HEAD_DIM, :], p.astype(BF16), preferred_element_type=F32)
            m_scr[...] = m_new

        scores(0, 0)

        def pair(c2, carry):
            c = 2 * c2
            scores(c + 1, 1)
            absorb(c, 0)
            scores(c + 2, 0)
            absorb(c + 1, 1)
            return carry

        lax.fori_loop(0, nc // 2 - 1, pair, 0)
        scores(nc - 1, 1)
        absorb(nc - 2, 0)
        absorb(nc - 1, 1)
        store_out(sacc_scr[...] / l_scr[...])


def _c_out_tile(oT_ref):
    return oT_ref[0].astype(F32).T.astype(BF16)


def _mixer_c(h, g1, w_qkv, q_gain, k_gain, w_o, g2, w1, w2):
    b, s, d = h.shape
    nq = d
    nk = C_KV_HEADS * HEAD_DIM
    tm = TOKEN_TILE
    nc = s // tm
    assert nc % 2 == 0 and nc >= 4 and (nc * tm) % (PASS1_UNROLL * PASS1_KEYS) == 0
    pos = jnp.arange(s)
    cr, sr = _rope_tables_T(pos // GRID_W, HEAD_DIM // 2, C_THETA)
    cc, sc = _rope_tables_T(pos % GRID_W, HEAD_DIM // 2, C_THETA)
    wq = w_qkv[:, :nq].T.astype(BF16)
    wk = w_qkv[:, nq:nq + nk].T.astype(BF16)
    wv = w_qkv[:, nq + nk:].T.astype(BF16)
    gq = jnp.broadcast_to(q_gain[:, None], (HEAD_DIM, tm))
    gk = jnp.broadcast_to(k_gain[:, None], (HEAD_DIM, tm))
    tab = pl.BlockSpec((HEAD_DIM // 4, tm), lambda bi, i: (0, i))
    qT, k, vT = pl.pallas_call(
        _c_proj_kernel,
        grid=(b, nc),
        in_specs=[
            pl.BlockSpec((1, tm, d), lambda bi, i: (bi, i, 0)),
            _resident((1, d)), _resident((nq, d)), _resident((nk, d)), _resident((nk, d)),
            _resident((HEAD_DIM, tm)), _resident((HEAD_DIM, tm)),
            tab, tab, tab, tab,
        ],
        out_specs=[
            pl.BlockSpec((1, nq, tm), lambda bi, i: (bi, 0, i)),
            pl.BlockSpec((1, tm, nk), lambda bi, i: (bi, i, 0)),
            pl.BlockSpec((1, C_KV_HEADS, 1, V_ROWS, tm), lambda bi, i: (bi, 0, i, 0, 0)),
        ],
        out_shape=[
            jax.ShapeDtypeStruct((b, nq, s), BF16),
            jax.ShapeDtypeStruct((b, s, nk), BF16),
            jax.ShapeDtypeStruct((b, C_KV_HEADS, nc, V_ROWS, tm), BF16),
        ],
        compiler_params=_cparams(("parallel", "parallel")),
        name="c_proj",
    )(h, g1, wq, wk, wv, gq, gk, cr, sr, cc, sc)

    grp = nq // nk
    tq = C_QUERY_TILE
    gw = grp * HEAD_DIM
    oT = pl.pallas_call(
        _c_attn_kernel,
        grid=(b, C_KV_HEADS, s // tq),
        in_specs=[
            pl.BlockSpec((1, gw, tq), lambda bi, hk, i: (bi, hk, i)),
            pl.BlockSpec((1, s, LANES), lambda bi, hk, i: (bi, 0, hk // 2)),
            pl.BlockSpec((1, 1, nc, V_ROWS, tm), lambda bi, hk, i: (bi, hk, 0, 0, 0)),
        ],
        out_specs=pl.BlockSpec((1, gw, tq), lambda bi, hk, i: (bi, hk, i)),
        out_shape=jax.ShapeDtypeStruct((b, nq, s), BF16),
        scratch_shapes=[pltpu.VMEM((LANES, grp * tq), BF16),
                        pltpu.VMEM((V_ROWS, grp * tq), F32),
                        pltpu.VMEM((2, grp * tq // PASS1_QUERIES, PASS1_KEYS, PASS1_QUERIES), BF16),
                        pltpu.VMEM((2, tm, grp * tq), F32), pltpu.VMEM((2, 1, grp * tq), F32),
                        pltpu.VMEM((1, grp * tq), F32), pltpu.VMEM((1, grp * tq), F32),
                        pltpu.VMEM((HEAD_DIM, grp * tq), F32)],
        compiler_params=_cparams(("parallel", "parallel", "parallel")),
        name="c_attn",
    )(qT, k, vT)

    in_specs = [
        pl.BlockSpec((1, d, tm), lambda t: (t // nc, 0, t % nc)),
        _resident((d, d)),
        pl.BlockSpec((tm, d), lambda t: (t, 0)),
    ]
    out = _out_mlp(_c_out_tile, [oT, w_o.astype(BF16), h.reshape(b * s, d)], in_specs, [],
                   b * s, d, tm, g2, w1, w2, "c_out_mlp")
    return out.reshape(b, s, d)


def kernel(x, norm1, norm2, a_wqkv, a_q_gain, a_k_gain, a_wo, b_win, b_conv, b_wout,
           c_wqkv, c_q_gain, c_k_gain, c_wo, mlp_w1, mlp_w2):
    h = x
    for i in range(norm1.shape[0]):
        kind, j = i % N_MIXERS, i // N_MIXERS
        g1 = norm1[i][None, :]
        mlp = (norm2[i][None, :], mlp_w1[i].astype(BF16), mlp_w2[i].astype(BF16))
        if kind == 0:
            h = _mixer_a(h, g1, a_wqkv[j], a_q_gain[j], a_k_gain[j], a_wo[j], *mlp)
        elif kind == 1:
            h = _mixer_b(h, g1, b_win[j], b_conv[j], b_wout[j], *mlp)
        else:
            h = _mixer_c(h, g1, c_wqkv[j], c_q_gain[j], c_k_gain[j], c_wo[j], *mlp)
    return h
```
